```python
import math
import jax
import jax.numpy as jnp
from jax import lax
import numpy as np

D_MODEL = 1024
BATCH = 2
SEQ = 8192
DEPTH = 4
DEC_BATCH = 32
DEC_SEQ = 1
PAST_LEN = 8192
PAGE_SIZE = 128

N_MIXERS = 3
N_GDN = (DEPTH + 2) // 3
N_SSM = (DEPTH + 1) // 3
N_ATT = DEPTH // 3
N_DENSE = (DEPTH + 1) // 2
N_MOE = DEPTH // 2
CONV_W = 4

GDN_HEADS = 8
GDN_DK = 128
GDN_DV = 128
GDN_KD = GDN_HEADS * GDN_DK
GDN_VD = GDN_HEADS * GDN_DV
GDN_CONV_DIM = 2 * GDN_KD + GDN_VD
GDN_PROJ = GDN_CONV_DIM + GDN_VD + 2 * GDN_HEADS
GDN_CHUNK = 64

SSM_D_INNER = 2 * D_MODEL
SSM_HEAD_DIM = 64
SSM_HEADS = SSM_D_INNER // SSM_HEAD_DIM
SSM_GROUPS = 4
SSM_HPG = SSM_HEADS // SSM_GROUPS
SSM_STATE = 128
SSM_CONV_DIM = SSM_D_INNER + 2 * SSM_GROUPS * SSM_STATE
SSM_PROJ = SSM_D_INNER + SSM_CONV_DIM + SSM_HEADS
SSM_CHUNK = 128

ATT_HEADS = 8
ATT_HEAD_DIM = D_MODEL // (2 * ATT_HEADS)
N_BUCKETS = 32
MAX_DISTANCE = 128
Q_BLOCK = 128

D_FF = 3584
N_EXPERTS = 8
TOP_K = 2

DEEPNORM_ALPHA = (2.0 * DEPTH) ** 0.25
DEEPNORM_BETA = (8.0 * DEPTH) ** -0.25
LN_EPS = 1e-5
RMS_EPS = 1e-6

kernel_name = 'hybrid_gdn_ssd_diffattn_step'


def layer_norm(x, g, b):
    xf = x.astype(jnp.float32)
    mu = jnp.mean(xf, -1, keepdims=True)
    var = jnp.mean(jnp.square(xf - mu), -1, keepdims=True)
    return ((xf - mu) * lax.rsqrt(var + LN_EPS) * g + b).astype(x.dtype)


def rms_norm(x, g):
    xf = x.astype(jnp.float32)
    return xf * lax.rsqrt(jnp.mean(jnp.square(xf), -1, keepdims=True) + RMS_EPS) * g


def l2_normalize(x):
    return x * lax.rsqrt(jnp.sum(jnp.square(x), -1, keepdims=True) + 1e-6)


def causal_dwconv(x, buf, w, b):
    L = x.shape[1]
    xp = jnp.concatenate([buf.astype(x.dtype), x], axis=1)
    y = sum(xp[:, i:i + L] * w[i] for i in range(CONV_W))
    if b is not None:
        y = y + b
    return y, xp[:, L:]


def to_chunks(t, c, pad):
    t = jnp.pad(t, [(0, 0), (0, pad)] + [(0, 0)] * (t.ndim - 2))
    t = t.reshape((t.shape[0], t.shape[1] // c, c) + t.shape[2:])
    return jnp.swapaxes(t, 0, 1)


def gated_delta_rule(q, k, v, g, beta, S0):
    Bn, L, H, DK = q.shape
    DV = v.shape[-1]
    C = min(GDN_CHUNK, L)
    pad = (-L) % C
    qc, kc, vc, gc, bc = [jnp.swapaxes(to_chunks(t, C, pad), 2, 3) for t in (q, k, v, g, beta)]
    gcum = jnp.cumsum(gc, axis=-1)
    incl = jnp.tril(jnp.ones((C, C), bool))
    strict = jnp.tril(jnp.ones((C, C), bool), -1)
    diff = gcum[..., :, None] - gcum[..., None, :]
    decay = jnp.where(incl, jnp.exp(jnp.where(incl, diff, 0.0)), 0.0)
    m = jnp.where(strict, bc[..., :, None] * jnp.einsum('zbhid,zbhjd->zbhij', kc, kc) * decay, 0.0)
    a = m + jnp.eye(C, dtype=m.dtype)
    rhs = jnp.concatenate([bc[..., None] * vc, (bc * jnp.exp(gcum))[..., None] * kc], axis=-1)
    w = lax.linalg.triangular_solve(a, rhs, left_side=True, lower=True, unit_diagonal=True)
    u_v, u_k = w[..., :DV], w[..., DV:]
    qk = jnp.where(incl, jnp.einsum('zbhid,zbhjd->zbhij', qc, kc) * decay, 0.0)
    q_dec = qc * jnp.exp(gcum)[..., None]
    k_end = kc * jnp.exp(gcum[..., -1:] - gcum)[..., None]
    g_end = jnp.exp(gcum[..., -1])

    def step(S, inp):
        uv, uk, qkc, qd, ke, ge = inp
        u = uv - jnp.einsum('bhcd,bhde->bhce', uk, S)
        o = jnp.einsum('bhcd,bhde->bhce', qd, S) + jnp.einsum('bhij,bhje->bhie', qkc, u)
        S = ge[..., None, None] * S + jnp.einsum('bhcd,bhce->bhde', ke, u)
        return S, o

    S_fin, o = lax.scan(step, S0, (u_v, u_k, qk, q_dec, k_end, g_end))
    o = jnp.swapaxes(jnp.swapaxes(o, 2, 3), 0, 1)
    o = o.reshape((Bn, o.shape[1] * C, H, DV))[:, :L]
    return o, S_fin


def gdn_mixer(x, S0, conv0, w_in, conv_w, a_log, dt_bias, norm_g, w_out):
    Bn, L, _ = x.shape
    proj = x @ w_in
    qkv, z, b, a = jnp.split(proj, [GDN_CONV_DIM, GDN_CONV_DIM + GDN_VD, GDN_CONV_DIM + GDN_VD + GDN_HEADS], axis=-1)
    qkv, conv_new = causal_dwconv(qkv, conv0, conv_w, None)
    qkv = jax.nn.silu(qkv).astype(jnp.float32)
    q, k, v = jnp.split(qkv, [GDN_KD, 2 * GDN_KD], axis=-1)
    q = l2_normalize(q.reshape(Bn, L, GDN_HEADS, GDN_DK)) * GDN_DK ** -0.5
    k = l2_normalize(k.reshape(Bn, L, GDN_HEADS, GDN_DK))
    v = v.reshape(Bn, L, GDN_HEADS, GDN_DV)
    beta = jax.nn.sigmoid(b.astype(jnp.float32))
    g = -jnp.exp(a_log.astype(jnp.float32)) * jax.nn.softplus(a.astype(jnp.float32) + dt_bias)
    o, S = gated_delta_rule(q, k, v, g, beta, S0.astype(jnp.float32))
    o = rms_norm(o, norm_g) * jax.nn.silu(z.astype(jnp.float32)).reshape(Bn, L, GDN_HEADS, GDN_DV)
    y = o.reshape(Bn, L, GDN_VD).astype(x.dtype) @ w_out
    return y, S.astype(S0.dtype), conv_new


def ssd_chunked(x, dt, A, Bm, Cm, h0):
    Bn, L = x.shape[:2]
    C = min(SSM_CHUNK, L)
    pad = (-L) % C
    xc, dtc, Bc, Cc = [to_chunks(t, C, pad) for t in (x, dt, Bm, Cm)]
    acum = jnp.cumsum(dtc * A, axis=2)
    incl = jnp.tril(jnp.ones((C, C), bool))[:, :, None, None]
    diff = acum[:, :, :, None] - acum[:, :, None, :]
    lmat = jnp.where(incl, jnp.exp(jnp.where(incl, diff, 0.0)), 0.0)
    cb = jnp.einsum('zbign,zbjgn->zbijg', Cc, Bc)
    scores = cb[..., None] * lmat * dtc[:, :, None]
    y_intra = jnp.einsum('zbijgr,zbjgrp->zbigrp', scores, xc)
    a_end = acum[:, :, -1]
    w_end = jnp.exp(a_end[:, :, None] - acum) * dtc
    dstate = jnp.einsum('zbjgr,zbjgn,zbjgrp->zbgrpn', w_end, Bc, xc)

    def step(h, inp):
        c, ac, ds, ae = inp
        y = jnp.exp(ac)[..., None] * jnp.einsum('bign,bgrpn->bigrp', c, h)
        h = jnp.exp(ae)[..., None, None] * h + ds
        return h, y

    h_fin, y_inter = lax.scan(step, h0, (Cc, acum, dstate, a_end))
    y = jnp.swapaxes(y_intra + y_inter, 0, 1)
    y = y.reshape((Bn, y.shape[1] * C) + y.shape[3:])[:, :L]
    return y, h_fin


def mamba2_mixer(x, h0, conv0, w_in, conv_w, conv_b, a_log, dt_bias, d_skip, norm_g, w_out):
    Bn, L, _ = x.shape
    gn = SSM_GROUPS * SSM_STATE
    z, xbc, dt = jnp.split(x @ w_in, [SSM_D_INNER, SSM_D_INNER + SSM_CONV_DIM], axis=-1)
    xbc, conv_new = causal_dwconv(xbc, conv0, conv_w, conv_b)
    xbc = jax.nn.silu(xbc).astype(jnp.float32)
    xs, Bm, Cm = jnp.split(xbc, [SSM_D_INNER, SSM_D_INNER + gn], axis=-1)
    xs = xs.reshape(Bn, L, SSM_GROUPS, SSM_HPG, SSM_HEAD_DIM)
    Bm = Bm.reshape(Bn, L, SSM_GROUPS, SSM_STATE)
    Cm = Cm.reshape(Bn, L, SSM_GROUPS, SSM_STATE)
    dt = jax.nn.softplus(dt.astype(jnp.float32) + dt_bias).reshape(Bn, L, SSM_GROUPS, SSM_HPG)
    A = -jnp.exp(a_log.astype(jnp.float32)).reshape(SSM_GROUPS, SSM_HPG)
    h0g = h0.astype(jnp.float32).reshape(Bn, SSM_GROUPS, SSM_HPG, SSM_HEAD_DIM, SSM_STATE)
    y, h = ssd_chunked(xs, dt, A, Bm, Cm, h0g)
    y = y + d_skip.reshape(SSM_GROUPS, SSM_HPG)[..., None] * xs
    y = y.reshape(Bn, L, SSM_D_INNER) * jax.nn.silu(z.astype(jnp.float32))
    y = rms_norm(y.reshape(Bn, L, SSM_GROUPS, SSM_D_INNER // SSM_GROUPS),
                 norm_g.reshape(SSM_GROUPS, SSM_D_INNER // SSM_GROUPS)).reshape(Bn, L, SSM_D_INNER)
    h = h.reshape(Bn, SSM_HEADS, SSM_HEAD_DIM, SSM_STATE).astype(h0.dtype)
    return y.astype(x.dtype) @ w_out, h, conv_new


def t5_bias(rel_table, q_pos, k_pos):
    n = jnp.maximum(q_pos[:, None] - k_pos[None, :], 0)
    max_exact = N_BUCKETS // 2
    large = max_exact + (jnp.log(jnp.maximum(n, max_exact).astype(jnp.float32) / max_exact)
                         / math.log(MAX_DISTANCE / max_exact) * (N_BUCKETS - max_exact)).astype(jnp.int32)
    bucket = jnp.where(n < max_exact, n, jnp.minimum(large, N_BUCKETS - 1))
    return jnp.transpose(rel_table[bucket], (2, 0, 1)).astype(jnp.float32)


def diff_scores(q, k, q_pos, k_pos, rel_table):
    s = jnp.einsum('bqhcd,bkhcd->bchqk', q, k, preferred_element_type=jnp.float32) * ATT_HEAD_DIM ** -0.5
    s = s + t5_bias(rel_table, q_pos, k_pos)
    return jnp.where(k_pos[None, :] <= q_pos[:, None], s, -jnp.inf)


def diff_weights(s, lam):
    p = jax.nn.softmax(s, axis=-1)
    return p[:, 0] - lam * p[:, 1]


def diff_lambda(lq1, lk1, lq2, lk2, lam_init):
    f = jnp.float32
    return (jnp.exp(jnp.sum(lq1.astype(f) * lk1.astype(f))) - jnp.exp(jnp.sum(lq2.astype(f) * lk2.astype(f)))
            + lam_init)


def diff_qkv(x, w_qkv):
    Bn, L, _ = x.shape
    q, k, v = jnp.split(x @ w_qkv, 3, axis=-1)
    q = q.reshape(Bn, L, ATT_HEADS, 2, ATT_HEAD_DIM)
    k = k.reshape(Bn, L, ATT_HEADS, 2, ATT_HEAD_DIM)
    v = v.reshape(Bn, L, ATT_HEADS, 2 * ATT_HEAD_DIM)
    return q, k, v


def diff_out(o, lam_init, subln_g, w_out, dtype):
    Bn, L = o.shape[:2]
    o = rms_norm(o, subln_g) * (1.0 - lam_init)
    return o.reshape(Bn, L, D_MODEL).astype(dtype) @ w_out


def diff_attn_prompt(x, lam, lam_init, rel_table, w_qkv, subln_g, w_out):
    Bn, L, _ = x.shape
    q, k, v = diff_qkv(x, w_qkv)
    pos = jnp.arange(L, dtype=jnp.int32)
    qb = min(Q_BLOCK, L)
    nb = L // qb
    q_blocks = jnp.swapaxes(q.reshape(Bn, nb, qb, ATT_HEADS, 2, ATT_HEAD_DIM), 0, 1)
    p_blocks = pos.reshape(nb, qb)

    def one_block(args):
        q_blk, q_pos = args
        a = diff_weights(diff_scores(q_blk, k, q_pos, pos, rel_table), lam)
        return jnp.einsum('bhqk,bkhe->bqhe', a.astype(v.dtype), v)

    o = lax.map(one_block, (q_blocks, p_blocks))
    o = jnp.swapaxes(o, 0, 1).reshape(Bn, L, ATT_HEADS, 2 * ATT_HEAD_DIM)
    y = diff_out(o, lam_init, subln_g, w_out, x.dtype)
    return y, k.reshape(Bn, L, ATT_HEADS, 2 * ATT_HEAD_DIM), v


def diff_attn_sample(x, k_pages, v_pages, page_table, lam, lam_init, rel_table, w_qkv, subln_g, w_out):
    Bn, L, _ = x.shape
    q, k, v = diff_qkv(x, w_qkv)
    past = page_table.shape[1] * PAGE_SIZE
    k_past = k_pages[page_table].reshape(Bn, past, ATT_HEADS, 2, ATT_HEAD_DIM).astype(k.dtype)
    v_past = v_pages[page_table].reshape(Bn, past, ATT_HEADS, 2 * ATT_HEAD_DIM).astype(v.dtype)
    q_pos = past + jnp.arange(L, dtype=jnp.int32)
    s = jnp.concatenate([diff_scores(q, k_past, q_pos, jnp.arange(past, dtype=jnp.int32), rel_table),
                         diff_scores(q, k, q_pos, q_pos, rel_table)], axis=-1)
    a = diff_weights(s, lam).astype(v.dtype)
    o = (jnp.einsum('bhqk,bkhe->bqhe', a[..., :past], v_past)
         + jnp.einsum('bhqk,bkhe->bqhe', a[..., past:], v))
    y = diff_out(o, lam_init, subln_g, w_out, x.dtype)
    return y, k.reshape(Bn, L, ATT_HEADS, 2 * ATT_HEAD_DIM), v


def swiglu(x, w_gu, w_down):
    g, u = jnp.split(x @ w_gu, 2, axis=-1)
    return (jax.nn.silu(g) * u) @ w_down


def moe_ffn(x, w_router, w_gu, w_down):
    shp = x.shape
    t = x.reshape(-1, shp[-1])
    logits = (t @ w_router).astype(jnp.float32)
    top_v, top_i = lax.top_k(logits, TOP_K)
    gates = jax.nn.softmax(top_v, axis=-1)
    combine = jnp.sum(jax.nn.one_hot(top_i, N_EXPERTS, dtype=jnp.float32) * gates[..., None], axis=1)
    y = jnp.zeros_like(t)
    for e in range(N_EXPERTS):
        y = y + combine[:, e:e + 1].astype(t.dtype) * swiglu(t, w_gu[e], w_down[e])
    return y.reshape(shp)


def setup_inputs(seed: int = 0) -> dict:
    key = jax.random.key(seed)
    ks = iter(jax.random.split(key, 64))

    def nrm(shape, scale=1.0):
        return jax.random.normal(next(ks), shape, jnp.float32) * scale

    def unif(shape, lo, hi):
        return jax.random.uniform(next(ks), shape, jnp.float32, lo, hi)

    def dt_bias_init(shape):
        dt = jnp.exp(unif(shape, math.log(1e-3), math.log(1e-1)))
        return dt + jnp.log(-jnp.expm1(-dt))

    n_pages = PAST_LEN // PAGE_SIZE
    n_used = DEC_BATCH * n_pages
    n_phys = n_used + max(1, n_used // 4)
    page_table = jax.random.permutation(next(ks), n_phys)[:n_used].reshape(DEC_BATCH, n_pages).astype(jnp.int32)
    sd = D_MODEL ** -0.5
    return {
        'x_prompt': nrm((BATCH, SEQ, D_MODEL)),
        'x_sample': nrm((DEC_BATCH, DEC_SEQ, D_MODEL)),
        'state_gdn_S': nrm((N_GDN, DEC_BATCH, GDN_HEADS, GDN_DK, GDN_DV), 0.1),
        'state_gdn_conv': nrm((N_GDN, DEC_BATCH, CONV_W - 1, GDN_CONV_DIM)),
        'state_ssm_h': nrm((N_SSM, DEC_BATCH, SSM_HEADS, SSM_HEAD_DIM, SSM_STATE), 0.1),
        'state_ssm_conv': nrm((N_SSM, DEC_BATCH, CONV_W - 1, SSM_CONV_DIM)),
        'cache_attn_k': nrm((N_ATT, n_phys, PAGE_SIZE, ATT_HEADS, 2 * ATT_HEAD_DIM)),
        'cache_attn_v': nrm((N_ATT, n_phys, PAGE_SIZE, ATT_HEADS, 2 * ATT_HEAD_DIM)),
        'page_table': page_table,
        'gdn_w_in': nrm((N_GDN, D_MODEL, GDN_PROJ), sd),
        'gdn_conv_w': nrm((N_GDN, CONV_W, GDN_CONV_DIM), CONV_W ** -0.5),
        'gdn_a_log': jnp.log(unif((N_GDN, GDN_HEADS), 1.0, 16.0)),
        'gdn_dt_bias': dt_bias_init((N_GDN, GDN_HEADS)),
        'gdn_norm_g': 1.0 + nrm((N_GDN, GDN_DV), 0.02),
        'gdn_w_out': nrm((N_GDN, GDN_VD, D_MODEL), GDN_VD ** -0.5 * DEEPNORM_BETA),
        'ssm_w_in': nrm((N_SSM, D_MODEL, SSM_PROJ), sd),
        'ssm_conv_w': nrm((N_SSM, CONV_W, SSM_CONV_DIM), CONV_W ** -0.5),
        'ssm_conv_b': nrm((N_SSM, SSM_CONV_DIM), 0.02),
        'ssm_a_log': jnp.log(unif((N_SSM, SSM_HEADS), 1.0, 16.0)),
        'ssm_dt_bias': dt_bias_init((N_SSM, SSM_HEADS)),
        'ssm_d_skip': 1.0 + nrm((N_SSM, SSM_HEADS), 0.1),
        'ssm_norm_g': 1.0 + nrm((N_SSM, SSM_D_INNER), 0.02),
        'ssm_w_out': nrm((N_SSM, SSM_D_INNER, D_MODEL), SSM_D_INNER ** -0.5 * DEEPNORM_BETA),
        'att_w_qkv': nrm((N_ATT, D_MODEL, 3 * D_MODEL), sd),
        'att_lam_q1': nrm((N_ATT, ATT_HEAD_DIM), 0.1),
        'att_lam_k1': nrm((N_ATT, ATT_HEAD_DIM), 0.1),
        'att_lam_q2': nrm((N_ATT, ATT_HEAD_DIM), 0.1),
        'att_lam_k2': nrm((N_ATT, ATT_HEAD_DIM), 0.1),
        'att_subln_g': 1.0 + nrm((N_ATT, 2 * ATT_HEAD_DIM), 0.02),
        'att_w_out': nrm((N_ATT, D_MODEL, D_MODEL), sd * DEEPNORM_BETA),
        'rel_bias': nrm((N_BUCKETS, ATT_HEADS), 0.5),
        'ffn_w_gu': nrm((N_DENSE, D_MODEL, 2 * D_FF), sd),
        'ffn_w_down': nrm((N_DENSE, D_FF, D_MODEL), D_FF ** -0.5 * DEEPNORM_BETA),
        'moe_w_router': nrm((N_MOE, D_MODEL, N_EXPERTS), sd),
        'moe_w_gu': nrm((N_MOE, N_EXPERTS, D_MODEL, 2 * D_FF), sd),
        'moe_w_down': nrm((N_MOE, N_EXPERTS, D_FF, D_MODEL), D_FF ** -0.5 * DEEPNORM_BETA),
        'ln_g': 1.0 + nrm((DEPTH, 2, D_MODEL), 0.02),
        'ln_b': nrm((DEPTH, 2, D_MODEL), 0.02),
    }


def reference(x_prompt, x_sample, state_gdn_S, state_gdn_conv, state_ssm_h, state_ssm_conv,
              cache_attn_k, cache_attn_v, page_table,
              gdn_w_in, gdn_conv_w, gdn_a_log, gdn_dt_bias, gdn_norm_g, gdn_w_out,
              ssm_w_in, ssm_conv_w, ssm_conv_b, ssm_a_log, ssm_dt_bias, ssm_d_skip, ssm_norm_g, ssm_w_out,
              att_w_qkv, att_lam_q1, att_lam_k1, att_lam_q2, att_lam_k2, att_subln_g, att_w_out, rel_bias,
              ffn_w_gu, ffn_w_down, moe_w_router, moe_w_gu, moe_w_down, ln_g, ln_b):
    xp, xs = x_prompt, x_sample
    bp = xp.shape[0]
    gdn_S_p, gdn_S_s, gdn_c_p, gdn_c_s = [], [], [], []
    ssm_h_p, ssm_h_s, ssm_c_p, ssm_c_s = [], [], [], []
    att_k_p, att_k_s, att_v_p, att_v_s = [], [], [], []
    for i in range(DEPTH):
        j = i // N_MIXERS
        if i % N_MIXERS == 0:
            prm = (gdn_w_in[j], gdn_conv_w[j], gdn_a_log[j], gdn_dt_bias[j], gdn_norm_g[j], gdn_w_out[j])
            s0 = jnp.zeros((bp,) + state_gdn_S.shape[2:], state_gdn_S.dtype)
            c0 = jnp.zeros((bp,) + state_gdn_conv.shape[2:], xp.dtype)
            mp, s_p, c_p = gdn_mixer(xp, s0, c0, *prm)
            ms, s_s, c_s = gdn_mixer(xs, state_gdn_S[j], state_gdn_conv[j], *prm)
            gdn_S_p.append(s_p); gdn_S_s.append(s_s); gdn_c_p.append(c_p); gdn_c_s.append(c_s)
        elif i % N_MIXERS == 1:
            prm = (ssm_w_in[j], ssm_conv_w[j], ssm_conv_b[j], ssm_a_log[j], ssm_dt_bias[j], ssm_d_skip[j],
                   ssm_norm_g[j], ssm_w_out[j])
            h0 = jnp.zeros((bp,) + state_ssm_h.shape[2:], state_ssm_h.dtype)
            c0 = jnp.zeros((bp,) + state_ssm_conv.shape[2:], xp.dtype)
            mp, h_p, c_p = mamba2_mixer(xp, h0, c0, *prm)
            ms, h_s, c_s = mamba2_mixer(xs, state_ssm_h[j], state_ssm_conv[j], *prm)
            ssm_h_p.append(h_p); ssm_h_s.append(h_s); ssm_c_p.append(c_p); ssm_c_s.append(c_s)
        else:
            lam_init = 0.8 - 0.6 * math.exp(-0.3 * i)
            lam = diff_lambda(att_lam_q1[j], att_lam_k1[j], att_lam_q2[j], att_lam_k2[j], lam_init)
            prm = (att_w_qkv[j], att_subln_g[j], att_w_out[j])
            mp, k_p, v_p = diff_attn_prompt(xp, lam, lam_init, rel_bias, *prm)
            ms, k_s, v_s = diff_attn_sample(xs, cache_attn_k[j], cache_attn_v[j], page_table, lam, lam_init,
                                            rel_bias, *prm)
            att_k_p.append(k_p); att_k_s.append(k_s); att_v_p.append(v_p); att_v_s.append(v_s)
        xp = layer_norm(DEEPNORM_ALPHA * xp + mp, ln_g[i, 0], ln_b[i, 0])
        xs = layer_norm(DEEPNORM_ALPHA * xs + ms, ln_g[i, 0], ln_b[i, 0])
        f = i // 2
        if i % 2 == 0:
            fp = swiglu(xp, ffn_w_gu[f], ffn_w_down[f])
            fs = swiglu(xs, ffn_w_gu[f], ffn_w_down[f])
        else:
            fp = moe_ffn(xp, moe_w_router[f], moe_w_gu[f], moe_w_down[f])
            fs = moe_ffn(xs, moe_w_router[f], moe_w_gu[f], moe_w_down[f])
        xp = layer_norm(DEEPNORM_ALPHA * xp + fp, ln_g[i, 1], ln_b[i, 1])
        xs = layer_norm(DEEPNORM_ALPHA * xs + fs, ln_g[i, 1], ln_b[i, 1])
    return (xp, xs,
            jnp.stack(gdn_S_p), jnp.stack(gdn_S_s), jnp.stack(gdn_c_p), jnp.stack(gdn_c_s),
            jnp.stack(ssm_h_p), jnp.stack(ssm_h_s), jnp.stack(ssm_c_p), jnp.stack(ssm_c_s),
            jnp.stack(att_k_p), jnp.stack(att_k_s), jnp.stack(att_v_p), jnp.stack(att_v_s))
```

```python
import functools
import math

import numpy as np
import jax
import jax.numpy as jnp
from jax import lax
from jax.experimental import pallas as pl
from jax.experimental.pallas import tpu as pltpu

F32 = jnp.float32
BF16 = jnp.bfloat16

D_MODEL = 1024
DEPTH = 4
PAGE_SIZE = 128
N_MIXERS = 3
CONV_W = 4

GDN_HEADS = 8
GDN_DK = 128
GDN_DV = 128
GDN_KD = GDN_HEADS * GDN_DK
GDN_VD = GDN_HEADS * GDN_DV
GDN_CONV_DIM = 2 * GDN_KD + GDN_VD
GDN_CHUNK = 64

SSM_D_INNER = 2 * D_MODEL
SSM_HEAD_DIM = 64
SSM_HEADS = SSM_D_INNER // SSM_HEAD_DIM
SSM_GROUPS = 4
SSM_HPG = SSM_HEADS // SSM_GROUPS
SSM_STATE = 128
SSM_CONV_DIM = SSM_D_INNER + 2 * SSM_GROUPS * SSM_STATE
SSM_CHUNK = 128

ATT_HEADS = 8
ATT_HEAD_DIM = D_MODEL // (2 * ATT_HEADS)
N_BUCKETS = 32
MAX_DISTANCE = 128
Q_BLOCK = 128

D_FF = 3584
N_EXPERTS = 8
TOP_K = 2

DEEPNORM_ALPHA = (2.0 * DEPTH) ** 0.25
LN_EPS = 1e-5
RMS_EPS = 1e-6

V7X_VMEM_BYTES = 64 * 1024 * 1024
VMEM_LIMIT_BYTES = V7X_VMEM_BYTES - 8 * 1024 * 1024
LANES = 128

FF_CHUNK = 512
N_FF_CHUNKS = D_FF // FF_CHUNK


def _cparams(n_axes):
    return pltpu.CompilerParams(dimension_semantics=("arbitrary",) * n_axes,
                                vmem_limit_bytes=VMEM_LIMIT_BYTES)


def _layer_norm_rows(x, g, b):
    mu = jnp.mean(x, axis=-1, keepdims=True)
    xc = x - mu
    var = jnp.mean(xc * xc, axis=-1, keepdims=True)
    return xc * lax.rsqrt(var + LN_EPS) * g + b


def _silu(x):
    return x * (1.0 / (1.0 + jnp.exp(-x)))


def _matmul_kernel(x_ref, w_ref, o_ref, xb_ref):
    @pl.when(pl.program_id(1) == 0)
    def _():
        xb_ref[...] = x_ref[...].astype(BF16)

    o_ref[...] = jnp.dot(xb_ref[...], w_ref[...].astype(BF16), preferred_element_type=F32)


def matmul_cols(x, w, n_cols, tm, tn):
    m, k = x.shape
    return pl.pallas_call(
        _matmul_kernel,
        grid=(m // tm, n_cols // tn),
        in_specs=[pl.BlockSpec((tm, k), lambda i, j: (i, 0)),
                  pl.BlockSpec((k, tn), lambda i, j: (0, j))],
        out_specs=pl.BlockSpec((tm, tn), lambda i, j: (i, j)),
        out_shape=jax.ShapeDtypeStruct((m, n_cols), F32),
        scratch_shapes=[pltpu.VMEM((tm, k), BF16)],
        compiler_params=_cparams(2),
        name="matmul_cols",
    )(x, w)


def _matmul_f32_kernel(x_ref, w_ref, o_ref):
    o_ref[...] = jnp.dot(x_ref[...], w_ref[...], preferred_element_type=F32,
                         precision=lax.Precision.HIGHEST)


def matmul_narrow_f32(x, w_pad, tm):
    m, k = x.shape
    return pl.pallas_call(
        _matmul_f32_kernel,
        grid=(m // tm,),
        in_specs=[pl.BlockSpec((tm, k), lambda i: (i, 0)),
                  pl.BlockSpec((k, LANES), lambda i: (0, 0))],
        out_specs=pl.BlockSpec((tm, LANES), lambda i: (i, 0)),
        out_shape=jax.ShapeDtypeStruct((m, LANES), F32),
        compiler_params=_cparams(1),
        name="matmul_narrow_f32",
    )(x, w_pad)


def _proj_ln_kernel(o_ref, w_ref, r_ref, g_ref, b_ref, y_ref, wb_ref):
    @pl.when(pl.program_id(0) == 0)
    def _():
        wb_ref[...] = w_ref[...].astype(BF16)

    m = jnp.dot(o_ref[...].astype(BF16), wb_ref[...], preferred_element_type=F32)
    y_ref[...] = _layer_norm_rows(DEEPNORM_ALPHA * r_ref[...] + m, g_ref[...], b_ref[...])


def proj_residual_ln(o, w, resid, g, b, tm):
    m, k = o.shape
    d = w.shape[1]
    return pl.pallas_call(
        _proj_ln_kernel,
        grid=(m // tm,),
        in_specs=[pl.BlockSpec((tm, k), lambda i: (i, 0)),
                  pl.BlockSpec((k, d), lambda i: (0, 0)),
                  pl.BlockSpec((tm, d), lambda i: (i, 0)),
                  pl.BlockSpec((1, d), lambda i: (0, 0)),
                  pl.BlockSpec((1, d), lambda i: (0, 0))],
        out_specs=pl.BlockSpec((tm, d), lambda i: (i, 0)),
        out_shape=jax.ShapeDtypeStruct((m, d), F32),
        scratch_shapes=[pltpu.VMEM((k, d), BF16)],
        compiler_params=_cparams(1),
        name="proj_residual_ln",
    )(o, w, resid, g.reshape(1, d), b.reshape(1, d))


def _ffn_kernel(te_ref, na_ref, x_ref, wg_ref, wu_ref, wd_ref, g_ref, b_ref, o_ref,
                xb_ref, acc_ref, *, fuse_ln):
    i = pl.program_id(0)
    j = pl.program_id(1)
    active = i < na_ref[0]

    @pl.when(jnp.logical_and(active, j == 0))
    def _():
        xb_ref[...] = x_ref[...].astype(BF16)

    @pl.when(active)
    def _():
        xb = xb_ref[...]
        h = jnp.dot(xb, wg_ref[...].astype(BF16), preferred_element_type=F32)
        u = jnp.dot(xb, wu_ref[...].astype(BF16), preferred_element_type=F32)
        a = (_silu(h) * u).astype(BF16)
        c = jnp.dot(a, wd_ref[...].astype(BF16), preferred_element_type=F32)

        @pl.when(j == 0)
        def _():
            acc_ref[...] = c

        @pl.when(j > 0)
        def _():
            acc_ref[...] += c

    @pl.when(jnp.logical_and(active, j == N_FF_CHUNKS - 1))
    def _():
        if fuse_ln:
            o_ref[...] = _layer_norm_rows(DEEPNORM_ALPHA * x_ref[...] + acc_ref[...],
                                          g_ref[...], b_ref[...])
        else:
            o_ref[...] = acc_ref[...]

    @pl.when(jnp.logical_and(jnp.logical_not(active), j == N_FF_CHUNKS - 1))
    def _():
        o_ref[...] = jnp.zeros_like(o_ref)


def swiglu_tiles(x, w_gu, w_down, tile_expert, n_active, ln_g, ln_b, tm, fuse_ln):
    m, d = x.shape
    n_tiles = m // tm

    def row_map(i, j, te, na):
        return (jnp.minimum(i, na[0] - 1), 0)

    def wg_map(i, j, te, na):
        return (te[jnp.minimum(i, na[0] - 1)], 0, j)

    def wu_map(i, j, te, na):
        return (te[jnp.minimum(i, na[0] - 1)], 0, j + N_FF_CHUNKS)

    def wd_map(i, j, te, na):
        return (te[jnp.minimum(i, na[0] - 1)], j, 0)

    grid_spec = pltpu.PrefetchScalarGridSpec(
        num_scalar_prefetch=2,
        grid=(n_tiles, N_FF_CHUNKS),
        in_specs=[pl.BlockSpec((tm, d), row_map),
                  pl.BlockSpec((None, d, FF_CHUNK), wg_map),
                  pl.BlockSpec((None, d, FF_CHUNK), wu_map),
                  pl.BlockSpec((None, FF_CHUNK, d), wd_map),
                  pl.BlockSpec((1, d), lambda i, j, te, na: (0, 0)),
                  pl.BlockSpec((1, d), lambda i, j, te, na: (0, 0))],
        out_specs=pl.BlockSpec((tm, d), lambda i, j, te, na: (i, 0)),
        scratch_shapes=[pltpu.VMEM((tm, d), BF16), pltpu.VMEM((tm, d), F32)],
    )
    return pl.pallas_call(
        functools.partial(_ffn_kernel, fuse_ln=fuse_ln),
        grid_spec=grid_spec,
        out_shape=jax.ShapeDtypeStruct((m, d), F32),
        compiler_params=_cparams(2),
        name="swiglu_ln" if fuse_ln else "swiglu_grouped",
    )(tile_expert, n_active, x, w_gu, w_gu, w_down, ln_g.reshape(1, d), ln_b.reshape(1, d))


def dense_ffn_ln(x, w_gu_all, w_down_all, layer, ln_g, ln_b, tm):
    n_tiles = x.shape[0] // tm
    te = jnp.full((n_tiles,), layer, jnp.int32)
    na = jnp.full((1,), n_tiles, jnp.int32)
    return swiglu_tiles(x, w_gu_all, w_down_all, te, na, ln_g, ln_b, tm, True)


def _router_kernel(x_ref, w_ref, meta_ref, cnt_ref, carry_ref, *, tm):
    i = pl.program_id(0)

    @pl.when(i == 0)
    def _():
        carry_ref[...] = jnp.zeros_like(carry_ref)

    logits = jnp.dot(x_ref[...], w_ref[...], preferred_element_type=F32,
                     precision=lax.Precision.HIGHEST)
    lane = lax.broadcasted_iota(jnp.int32, (tm, LANES), 1)
    neg = jnp.float32(-jnp.inf)
    logits = jnp.where(lane < N_EXPERTS, logits, neg)
    m1 = jnp.max(logits, axis=-1, keepdims=True)
    i1 = jnp.min(jnp.where(logits == m1, lane, LANES), axis=-1, keepdims=True)
    rest = jnp.where(lane == i1, neg, logits)
    m2 = jnp.max(rest, axis=-1, keepdims=True)
    i2 = jnp.min(jnp.where(rest == m2, lane, LANES), axis=-1, keepdims=True)
    e = jnp.exp(m2 - m1)
    g1 = 1.0 / (1.0 + e)
    g2 = e / (1.0 + e)

    chosen = jnp.logical_or(lane == i1, lane == i2)
    onehot = jnp.where(chosen, 1.0, 0.0).astype(BF16)
    row = lax.broadcasted_iota(jnp.int32, (tm, tm), 0)
    col = lax.broadcasted_iota(jnp.int32, (tm, tm), 1)
    strict_lower = jnp.where(col < row, 1.0, 0.0).astype(BF16)
    before = jnp.dot(strict_lower, onehot, preferred_element_type=F32) + carry_ref[...]
    r1 = jnp.sum(jnp.where(lane == i1, before, 0.0), axis=-1, keepdims=True)
    r2 = jnp.sum(jnp.where(lane == i2, before, 0.0), axis=-1, keepdims=True)
    carry_ref[...] += jnp.sum(onehot.astype(F32), axis=0, keepdims=True)

    meta = jnp.where(lane == 0, i1.astype(F32),
           jnp.where(lane == 1, i2.astype(F32),
           jnp.where(lane == 2, g1,
           jnp.where(lane == 3, g2,
           jnp.where(lane == 4, r1,
           jnp.where(lane == 5, r2, 0.0))))))
    meta_ref[...] = meta
    cnt_ref[...] = carry_ref[...]


def route_top2(x, w_router, tm):
    t, d = x.shape
    w_pad = jnp.pad(w_router, ((0, 0), (0, LANES - N_EXPERTS)))
    return pl.pallas_call(
        functools.partial(_router_kernel, tm=tm),
        grid=(t // tm,),
        in_specs=[pl.BlockSpec((tm, d), lambda i: (i, 0)),
                  pl.BlockSpec((d, LANES), lambda i: (0, 0))],
        out_specs=[pl.BlockSpec((tm, LANES), lambda i: (i, 0)),
                   pl.BlockSpec((1, LANES), lambda i: (0, 0))],
        out_shape=[jax.ShapeDtypeStruct((t, LANES), F32),
                   jax.ShapeDtypeStruct((1, LANES), F32)],
        scratch_shapes=[pltpu.VMEM((1, LANES), F32)],
        compiler_params=_cparams(1),
        name="route_top2",
    )(x, w_pad)


def _dispatch_kernel(pos_ref, x_hbm, xs_in_hbm, xs_hbm, sem, *, tokens_per_step):
    del xs_in_hbm
    base = pl.program_id(0) * tokens_per_step

    def row_copy(t, k):
        return pltpu.make_async_copy(x_hbm.at[pl.ds(base + t, 1)],
                                     xs_hbm.at[pl.ds(pos_ref[0, 2 * t + k], 1)], sem)

    def issue(t, carry):
        row_copy(t, 0).start()
        row_copy(t, 1).start()
        return carry

    lax.fori_loop(0, tokens_per_step, issue, 0)

    def drain(t, carry):
        row_copy(t, 0).wait()
        row_copy(t, 1).wait()
        return carry

    lax.fori_loop(0, tokens_per_step, drain, 0)


def dispatch_rows(x, pos, n_rows, tokens_per_step):
    t, d = x.shape
    n_steps = t // tokens_per_step
    pos2 = pos.reshape(n_steps, 1, 2 * tokens_per_step)
    zeros = jnp.zeros((n_rows, d), x.dtype)
    return pl.pallas_call(
        functools.partial(_dispatch_kernel, tokens_per_step=tokens_per_step),
        grid=(n_steps,),
        in_specs=[pl.BlockSpec((None, 1, 2 * tokens_per_step), lambda i: (i, 0, 0),
                               memory_space=pltpu.SMEM),
                  pl.BlockSpec(memory_space=pl.ANY),
                  pl.BlockSpec(memory_space=pl.ANY)],
        out_specs=pl.BlockSpec(memory_space=pl.ANY),
        out_shape=jax.ShapeDtypeStruct((n_rows, d), x.dtype),
        scratch_shapes=[pltpu.SemaphoreType.DMA(())],
        input_output_aliases={2: 0},
        compiler_params=_cparams(1),
        name="dispatch_rows",
    )(pos2, x, zeros)


def _combine_ln_kernel(pos_ref, x_ref, gate_ref, g_ref, b_ref, ys_hbm, o_ref, buf_ref, sem, *, tm):
    def row_copy(t, k):
        return pltpu.make_async_copy(ys_hbm.at[pl.ds(pos_ref[0, 2 * t + k], 1)],
                                     buf_ref.at[k, pl.ds(t, 1)], sem)

    def issue(t, carry):
        row_copy(t, 0).start()
        row_copy(t, 1).start()
        return carry

    lax.fori_loop(0, tm, issue, 0)

    def drain(t, carry):
        row_copy(t, 0).wait()
        row_copy(t, 1).wait()
        return carry

    lax.fori_loop(0, tm, drain, 0)

    gates = gate_ref[...]
    y = gates[:, 2:3] * buf_ref[0] + gates[:, 3:4] * buf_ref[1]
    o_ref[...] = _layer_norm_rows(DEEPNORM_ALPHA * x_ref[...] + y, g_ref[...], b_ref[...])


def combine_ln(x, ys, pos, meta, ln_g, ln_b, tm):
    t, d = x.shape
    n_steps = t // tm
    pos2 = pos.reshape(n_steps, 1, 2 * tm)
    return pl.pallas_call(
        functools.partial(_combine_ln_kernel, tm=tm),
        grid=(n_steps,),
        in_specs=[pl.BlockSpec((None, 1, 2 * tm), lambda i: (i, 0, 0), memory_space=pltpu.SMEM),
                  pl.BlockSpec((tm, d), lambda i: (i, 0)),
                  pl.BlockSpec((tm, LANES), lambda i: (i, 0)),
                  pl.BlockSpec((1, d), lambda i: (0, 0)),
                  pl.BlockSpec((1, d), lambda i: (0, 0)),
                  pl.BlockSpec(memory_space=pl.ANY)],
        out_specs=pl.BlockSpec((tm, d), lambda i: (i, 0)),
        out_shape=jax.ShapeDtypeStruct((t, d), F32),
        scratch_shapes=[pltpu.VMEM((2, tm, d), F32), pltpu.SemaphoreType.DMA(())],
        compiler_params=_cparams(1),
        name="combine_ln",
    )(pos2, x, meta, ln_g.reshape(1, d), ln_b.reshape(1, d), ys)


def moe_ffn_ln(x, w_router, w_gu_all, w_down_all, layer, ln_g, ln_b, tm_route, tm_group, tm_combine):
    t, d = x.shape
    meta, counts = route_top2(x, w_router, tm_route)
    ids = meta[:, 0:2].astype(jnp.int32)
    ranks = meta[:, 4:6].astype(jnp.int32)
    cnt = counts[0, :N_EXPERTS].astype(jnp.int32)
    tiles_per = (cnt + tm_group - 1) // tm_group
    tile_end = jnp.cumsum(tiles_per)
    group_start = (tile_end - tiles_per) * tm_group
    pos = group_start[ids] + ranks
    n_tiles = min((TOP_K * t + N_EXPERTS * (tm_group - 1)) // tm_group,
                  N_EXPERTS * ((t + tm_group - 1) // tm_group))
    n_active = tile_end[N_EXPERTS - 1:N_EXPERTS]
    tile_ids = jnp.arange(n_tiles, dtype=jnp.int32)
    tile_expert = jnp.sum((tile_ids[:, None] >= tile_end[None, :]).astype(jnp.int32), axis=1)
    tile_expert = jnp.minimum(tile_expert, N_EXPERTS - 1) + layer * N_EXPERTS
    xs = dispatch_rows(x, pos, n_tiles * tm_group, min(t, 1024))
    ys = swiglu_tiles(xs, w_gu_all, w_down_all, tile_expert.astype(jnp.int32),
                      n_active.astype(jnp.int32), ln_g, ln_b, tm_group, False)
    return combine_ln(x, ys, pos, meta, ln_g, ln_b, tm_combine)


def _rms_norm(x, g):
    return x * lax.rsqrt(jnp.mean(jnp.square(x), -1, keepdims=True) + RMS_EPS) * g


def _l2_normalize(x):
    return x * lax.rsqrt(jnp.sum(jnp.square(x), -1, keepdims=True) + 1e-6)


def _causal_dwconv(x, buf, w, b):
    l = x.shape[1]
    xp = jnp.concatenate([buf.astype(x.dtype), x], axis=1)
    y = sum(xp[:, i:i + l] * w[i] for i in range(CONV_W))
    if b is not None:
        y = y + b
    return y, xp[:, l:]


def _to_chunks(t, c, pad):
    t = jnp.pad(t, [(0, 0), (0, pad)] + [(0, 0)] * (t.ndim - 2))
    t = t.reshape((t.shape[0], t.shape[1] // c, c) + t.shape[2:])
    return jnp.swapaxes(t, 0, 1)


def _gated_delta_rule(q, k, v, g, beta, s0):
    bn, l, h, dk = q.shape
    dv = v.shape[-1]
    c = min(GDN_CHUNK, l)
    pad = (-l) % c
    qc, kc, vc, gc, bc = [jnp.swapaxes(_to_chunks(t, c, pad), 2, 3) for t in (q, k, v, g, beta)]
    gcum = jnp.cumsum(gc, axis=-1)
    incl = jnp.tril(jnp.ones((c, c), bool))
    strict = jnp.tril(jnp.ones((c, c), bool), -1)
    diff = gcum[..., :, None] - gcum[..., None, :]
    decay = jnp.where(incl, jnp.exp(jnp.where(incl, diff, 0.0)), 0.0)
    m = jnp.where(strict, bc[..., :, None] * jnp.einsum('zbhid,zbhjd->zbhij', kc, kc) * decay, 0.0)
    a = m + jnp.eye(c, dtype=m.dtype)
    rhs = jnp.concatenate([bc[..., None] * vc, (bc * jnp.exp(gcum))[..., None] * kc], axis=-1)
    w = lax.linalg.triangular_solve(a, rhs, left_side=True, lower=True, unit_diagonal=True)
    u_v, u_k = w[..., :dv], w[..., dv:]
    qk = jnp.where(incl, jnp.einsum('zbhid,zbhjd->zbhij', qc, kc) * decay, 0.0)
    q_dec = qc * jnp.exp(gcum)[..., None]
    k_end = kc * jnp.exp(gcum[..., -1:] - gcum)[..., None]
    g_end = jnp.exp(gcum[..., -1])

    def step(s, inp):
        uv, uk, qkc, qd, ke, ge = inp
        u = uv - jnp.einsum('bhcd,bhde->bhce', uk, s)
        o = jnp.einsum('bhcd,bhde->bhce', qd, s) + jnp.einsum('bhij,bhje->bhie', qkc, u)
        s = ge[..., None, None] * s + jnp.einsum('bhcd,bhce->bhde', ke, u)
        return s, o

    s_fin, o = lax.scan(step, s0, (u_v, u_k, qk, q_dec, k_end, g_end))
    o = jnp.swapaxes(jnp.swapaxes(o, 2, 3), 0, 1)
    o = o.reshape((bn, o.shape[1] * c, h, dv))[:, :l]
    return o, s_fin


def gdn_core(proj, ba, s0, conv0, conv_w, a_log, dt_bias, norm_g):
    bn, l, _ = proj.shape
    qkv, z = proj[..., :GDN_CONV_DIM], proj[..., GDN_CONV_DIM:]
    b, a = ba[..., :GDN_HEADS], ba[..., GDN_HEADS:2 * GDN_HEADS]
    qkv, conv_new = _causal_dwconv(qkv, conv0, conv_w, None)
    qkv = jax.nn.silu(qkv)
    q, k, v = jnp.split(qkv, [GDN_KD, 2 * GDN_KD], axis=-1)
    q = _l2_normalize(q.reshape(bn, l, GDN_HEADS, GDN_DK)) * GDN_DK ** -0.5
    k = _l2_normalize(k.reshape(bn, l, GDN_HEADS, GDN_DK))
    v = v.reshape(bn, l, GDN_HEADS, GDN_DV)
    beta = jax.nn.sigmoid(b)
    g = -jnp.exp(a_log) * jax.nn.softplus(a + dt_bias)
    o, s = _gated_delta_rule(q, k, v, g, beta, s0)
    o = _rms_norm(o, norm_g) * jax.nn.silu(z).reshape(bn, l, GDN_HEADS, GDN_DV)
    return o.reshape(bn, l, GDN_VD), s, conv_new


def _ssd_chunked(x, dt, a_neg, bm, cm, h0):
    bn, l = x.shape[:2]
    c = min(SSM_CHUNK, l)
    pad = (-l) % c
    xc, dtc, bc, cc = [_to_chunks(t, c, pad) for t in (x, dt, bm, cm)]
    acum = jnp.cumsum(dtc * a_neg, axis=2)
    incl = jnp.tril(jnp.ones((c, c), bool))[:, :, None, None]
    diff = acum[:, :, :, None] - acum[:, :, None, :]
    lmat = jnp.where(incl, jnp.exp(jnp.where(incl, diff, 0.0)), 0.0)
    cb = jnp.einsum('zbign,zbjgn->zbijg', cc, bc)
    scores = cb[..., None] * lmat * dtc[:, :, None]
    y_intra = jnp.einsum('zbijgr,zbjgrp->zbigrp', scores, xc)
    a_end = acum[:, :, -1]
    w_end = jnp.exp(a_end[:, :, None] - acum) * dtc
    dstate = jnp.einsum('zbjgr,zbjgn,zbjgrp->zbgrpn', w_end, bc, xc)

    def step(h, inp):
        cq, ac, ds, ae = inp
        y = jnp.exp(ac)[..., None] * jnp.einsum('bign,bgrpn->bigrp', cq, h)
        h = jnp.exp(ae)[..., None, None] * h + ds
        return h, y

    h_fin, y_inter = lax.scan(step, h0, (cc, acum, dstate, a_end))
    y = jnp.swapaxes(y_intra + y_inter, 0, 1)
    y = y.reshape((bn, y.shape[1] * c) + y.shape[3:])[:, :l]
    return y, h_fin


def ssd_core(proj, dtp, h0, conv0, conv_w, conv_b, a_log, dt_bias, d_skip, norm_g):
    bn, l, _ = proj.shape
    gn = SSM_GROUPS * SSM_STATE
    z, xbc = proj[..., :SSM_D_INNER], proj[..., SSM_D_INNER:]
    dt = dtp[..., :SSM_HEADS]
    xbc, conv_new = _causal_dwconv(xbc, conv0, conv_w, conv_b)
    xbc = jax.nn.silu(xbc)
    xs, bm, cm = jnp.split(xbc, [SSM_D_INNER, SSM_D_INNER + gn], axis=-1)
    xs = xs.reshape(bn, l, SSM_GROUPS, SSM_HPG, SSM_HEAD_DIM)
    bm = bm.reshape(bn, l, SSM_GROUPS, SSM_STATE)
    cm = cm.reshape(bn, l, SSM_GROUPS, SSM_STATE)
    dt = jax.nn.softplus(dt + dt_bias).reshape(bn, l, SSM_GROUPS, SSM_HPG)
    a_neg = -jnp.exp(a_log).reshape(SSM_GROUPS, SSM_HPG)
    h0g = h0.reshape(bn, SSM_GROUPS, SSM_HPG, SSM_HEAD_DIM, SSM_STATE)
    y, h = _ssd_chunked(xs, dt, a_neg, bm, cm, h0g)
    y = y + d_skip.reshape(SSM_GROUPS, SSM_HPG)[..., None] * xs
    y = y.reshape(bn, l, SSM_D_INNER) * jax.nn.silu(z)
    y = _rms_norm(y.reshape(bn, l, SSM_GROUPS, SSM_D_INNER // SSM_GROUPS),
                  norm_g.reshape(SSM_GROUPS, SSM_D_INNER // SSM_GROUPS)).reshape(bn, l, SSM_D_INNER)
    h = h.reshape(bn, SSM_HEADS, SSM_HEAD_DIM, SSM_STATE)
    return y, h, conv_new


def _t5_bias(rel_table, q_pos, k_pos):
    n = jnp.maximum(q_pos[:, None] - k_pos[None, :], 0)
    max_exact = N_BUCKETS // 2
    large = max_exact + (jnp.log(jnp.maximum(n, max_exact).astype(F32) / max_exact)
                         / math.log(MAX_DISTANCE / max_exact) * (N_BUCKETS - max_exact)).astype(jnp.int32)
    bucket = jnp.where(n < max_exact, n, jnp.minimum(large, N_BUCKETS - 1))
    return jnp.transpose(rel_table[bucket], (2, 0, 1)).astype(F32)


def _diff_scores(q, k, q_pos, k_pos, rel_table):
    s = jnp.einsum('bqhcd,bkhcd->bchqk', q, k, preferred_element_type=F32) * ATT_HEAD_DIM ** -0.5
    s = s + _t5_bias(rel_table, q_pos, k_pos)
    return jnp.where(k_pos[None, :] <= q_pos[:, None], s, -jnp.inf)


def _diff_weights(s, lam):
    p = jax.nn.softmax(s, axis=-1)
    return p[:, 0] - lam * p[:, 1]


def _split_qkv(qkv):
    bn, l, _ = qkv.shape
    q, k, v = jnp.split(qkv, 3, axis=-1)
    q = q.reshape(bn, l, ATT_HEADS, 2, ATT_HEAD_DIM)
    k = k.reshape(bn, l, ATT_HEADS, 2, ATT_HEAD_DIM)
    v = v.reshape(bn, l, ATT_HEADS, 2 * ATT_HEAD_DIM)
    return q, k, v


def attn_prompt_core(qkv, lam, lam_init, rel_table, subln_g):
    bn, l, _ = qkv.shape
    q, k, v = _split_qkv(qkv)
    pos = jnp.arange(l, dtype=jnp.int32)
    qb = min(Q_BLOCK, l)
    nb = l // qb
    q_blocks = jnp.swapaxes(q.reshape(bn, nb, qb, ATT_HEADS, 2, ATT_HEAD_DIM), 0, 1)
    p_blocks = pos.reshape(nb, qb)

    def one_block(args):
        q_blk, q_pos = args
        a = _diff_weights(_diff_scores(q_blk, k, q_pos, pos, rel_table), lam)
        return jnp.einsum('bhqk,bkhe->bqhe', a, v)

    o = lax.map(one_block, (q_blocks, p_blocks))
    o = jnp.swapaxes(o, 0, 1).reshape(bn, l, ATT_HEADS, 2 * ATT_HEAD_DIM)
    o = _rms_norm(o, subln_g) * (1.0 - lam_init)
    return o.reshape(bn, l, D_MODEL), k.reshape(bn, l, ATT_HEADS, 2 * ATT_HEAD_DIM), v


def attn_sample_core(qkv, k_pages, v_pages, page_table, lam, lam_init, rel_table, subln_g):
    bn, l, _ = qkv.shape
    q, k, v = _split_qkv(qkv)
    past = page_table.shape[1] * PAGE_SIZE
    k_past = k_pages[page_table].reshape(bn, past, ATT_HEADS, 2, ATT_HEAD_DIM)
    v_past = v_pages[page_table].reshape(bn, past, ATT_HEADS, 2 * ATT_HEAD_DIM)
    q_pos = past + jnp.arange(l, dtype=jnp.int32)
    s = jnp.concatenate([_diff_scores(q, k_past, q_pos, jnp.arange(past, dtype=jnp.int32), rel_table),
                         _diff_scores(q, k, q_pos, q_pos, rel_table)], axis=-1)
    a = _diff_weights(s, lam)
    o = (jnp.einsum('bhqk,bkhe->bqhe', a[..., :past], v_past)
         + jnp.einsum('bhqk,bkhe->bqhe', a[..., past:], v))
    o = _rms_norm(o, subln_g) * (1.0 - lam_init)
    return o.reshape(bn, l, D_MODEL), k.reshape(bn, l, ATT_HEADS, 2 * ATT_HEAD_DIM), v


def _pad_cols(w):
    return jnp.pad(w, ((0, 0), (0, LANES - w.shape[1])))


def kernel(x_prompt, x_sample, state_gdn_S, state_gdn_conv, state_ssm_h, state_ssm_conv,
           cache_attn_k, cache_attn_v, page_table,
           gdn_w_in, gdn_conv_w, gdn_a_log, gdn_dt_bias, gdn_norm_g, gdn_w_out,
           ssm_w_in, ssm_conv_w, ssm_conv_b, ssm_a_log, ssm_dt_bias, ssm_d_skip, ssm_norm_g, ssm_w_out,
           att_w_qkv, att_lam_q1, att_lam_k1, att_lam_q2, att_lam_k2, att_subln_g, att_w_out, rel_bias,
           ffn_w_gu, ffn_w_down, moe_w_router, moe_w_gu, moe_w_down, ln_g, ln_b):
    bp, lp, d = x_prompt.shape
    bs, ls, _ = x_sample.shape
    tp = bp * lp
    ts = bs * ls
    xp = x_prompt.reshape(tp, d)
    xs = x_sample.reshape(ts, d)
    moe_gu = moe_w_gu.reshape((-1,) + moe_w_gu.shape[2:])
    moe_down = moe_w_down.reshape((-1,) + moe_w_down.shape[2:])

    tm_p = 1024
    tm_s = ts

    outs = {name: [] for name in ("gS_p", "gS_s", "gc_p", "gc_s", "sh_p", "sh_s", "sc_p", "sc_s",
                                  "ak_p", "ak_s", "av_p", "av_s")}
    for i in range(DEPTH):
        j = i // N_MIXERS
        if i % N_MIXERS == 0:
            n_main = GDN_CONV_DIM + GDN_VD
            w_ba = _pad_cols(gdn_w_in[j][:, n_main:])
            prm = (gdn_conv_w[j], gdn_a_log[j], gdn_dt_bias[j], gdn_norm_g[j])
            proj_p = matmul_cols(xp, gdn_w_in[j], n_main, tm_p, 512).reshape(bp, lp, n_main)
            ba_p = matmul_narrow_f32(xp, w_ba, tm_p).reshape(bp, lp, LANES)
            s0 = jnp.zeros((bp,) + state_gdn_S.shape[2:], F32)
            c0 = jnp.zeros((bp,) + state_gdn_conv.shape[2:], F32)
            op, s_p, c_p = gdn_core(proj_p, ba_p, s0, c0, *prm)
            proj_s = matmul_cols(xs, gdn_w_in[j], n_main, tm_s, 512).reshape(bs, ls, n_main)
            ba_s = matmul_narrow_f32(xs, w_ba, tm_s).reshape(bs, ls, LANES)
            os_, s_s, c_s = gdn_core(proj_s, ba_s, state_gdn_S[j], state_gdn_conv[j], *prm)
            outs["gS_p"].append(s_p); outs["gS_s"].append(s_s)
            outs["gc_p"].append(c_p); outs["gc_s"].append(c_s)
            w_out = gdn_w_out[j]
        elif i % N_MIXERS == 1:
            n_main = SSM_D_INNER + SSM_CONV_DIM
            w_dt = _pad_cols(ssm_w_in[j][:, n_main:])
            prm = (ssm_conv_w[j], ssm_conv_b[j], ssm_a_log[j], ssm_dt_bias[j], ssm_d_skip[j], ssm_norm_g[j])
            proj_p = matmul_cols(xp, ssm_w_in[j], n_main, tm_p, 512).reshape(bp, lp, n_main)
            dt_p = matmul_narrow_f32(xp, w_dt, tm_p).reshape(bp, lp, LANES)
            h0 = jnp.zeros((bp,) + state_ssm_h.shape[2:], F32)
            c0 = jnp.zeros((bp,) + state_ssm_conv.shape[2:], F32)
            op, h_p, c_p = ssd_core(proj_p, dt_p, h0, c0, *prm)
            proj_s = matmul_cols(xs, ssm_w_in[j], n_main, tm_s, 512).reshape(bs, ls, n_main)
            dt_s = matmul_narrow_f32(xs, w_dt, tm_s).reshape(bs, ls, LANES)
            os_, h_s, c_s = ssd_core(proj_s, dt_s, state_ssm_h[j], state_ssm_conv[j], *prm)
            outs["sh_p"].append(h_p); outs["sh_s"].append(h_s)
            outs["sc_p"].append(c_p); outs["sc_s"].append(c_s)
            w_out = ssm_w_out[j]
        else:
            lam_init = 0.8 - 0.6 * math.exp(-0.3 * i)
            lam = (jnp.exp(jnp.sum(att_lam_q1[j] * att_lam_k1[j]))
                   - jnp.exp(jnp.sum(att_lam_q2[j] * att_lam_k2[j])) + lam_init)
            qkv_p = matmul_cols(xp, att_w_qkv[j], 3 * d, tm_p, 512).reshape(bp, lp, 3 * d)
            op, k_p, v_p = attn_prompt_core(qkv_p, lam, lam_init, rel_bias, att_subln_g[j])
            qkv_s = matmul_cols(xs, att_w_qkv[j], 3 * d, tm_s, 512).reshape(bs, ls, 3 * d)
            os_, k_s, v_s = attn_sample_core(qkv_s, cache_attn_k[j], cache_attn_v[j], page_table,
                                             lam, lam_init, rel_bias, att_subln_g[j])
            outs["ak_p"].append(k_p); outs["ak_s"].append(k_s)
            outs["av_p"].append(v_p); outs["av_s"].append(v_s)
            w_out = att_w_out[j]
        xp = proj_residual_ln(op.reshape(tp, -1), w_out, xp, ln_g[i, 0], ln_b[i, 0], 512)
        xs = proj_residual_ln(os_.reshape(ts, -1), w_out, xs, ln_g[i, 0], ln_b[i, 0], tm_s)
        f = i // 2
        if i % 2 == 0:
            xp = dense_ffn_ln(xp, ffn_w_gu, ffn_w_down, f, ln_g[i, 1], ln_b[i, 1], tm_p)
            xs = dense_ffn_ln(xs, ffn_w_gu, ffn_w_down, f, ln_g[i, 1], ln_b[i, 1], tm_s)
        else:
            xp = moe_ffn_ln(xp, moe_w_router[f], moe_gu, moe_down, f, ln_g[i, 1], ln_b[i, 1],
                            512, 1024, 256)
            xs = moe_ffn_ln(xs, moe_w_router[f], moe_gu, moe_down, f, ln_g[i, 1], ln_b[i, 1],
                            ts, ts, ts)
    st = lambda name: jnp.stack(outs[name])
    return (xp.reshape(bp, lp, d), xs.reshape(bs, ls, d),
            st("gS_p"), st("gS_s"), st("gc_p"), st("gc_s"),
            st("sh_p"), st("sh_s"), st("sc_p"), st("sc_s"),
            st("ak_p"), st("ak_s"), st("av_p"), st("av_s"))
```

```python
import functools
import math

import numpy as np
import jax
import jax.numpy as jnp
from jax import lax
from jax.experimental import pallas as pl
from jax.experimental.pallas import tpu as pltpu

F32 = jnp.float32
BF16 = jnp.bfloat16

D_MODEL = 1024
DEPTH = 4
PAGE_SIZE = 128
N_MIXERS = 3
CONV_W = 4

GDN_HEADS = 8
GDN_DK = 128
GDN_DV = 128
GDN_KD = GDN_HEADS * GDN_DK
GDN_VD = GDN_HEADS * GDN_DV
GDN_CONV_DIM = 2 * GDN_KD + GDN_VD
GDN_CHUNK = 64

SSM_D_INNER = 2 * D_MODEL
SSM_HEAD_DIM = 64
SSM_HEADS = SSM_D_INNER // SSM_HEAD_DIM
SSM_GROUPS = 4
SSM_HPG = SSM_HEADS // SSM_GROUPS
SSM_STATE = 128
SSM_CONV_DIM = SSM_D_INNER + 2 * SSM_GROUPS * SSM_STATE
SSM_CHUNK = 128

ATT_HEADS = 8
ATT_HEAD_DIM = D_MODEL // (2 * ATT_HEADS)
N_BUCKETS = 32
MAX_DISTANCE = 128
Q_BLOCK = 128

D_FF = 3584
N_EXPERTS = 8
TOP_K = 2

DEEPNORM_ALPHA = (2.0 * DEPTH) ** 0.25
LN_EPS = 1e-5
RMS_EPS = 1e-6

V7X_VMEM_BYTES = 64 * 1024 * 1024
VMEM_LIMIT_BYTES = V7X_VMEM_BYTES - 8 * 1024 * 1024
LANES = 128

FF_CHUNK = 512
N_FF_CHUNKS = D_FF // FF_CHUNK


def _cparams(n_axes):
    return pltpu.CompilerParams(dimension_semantics=("arbitrary",) * n_axes,
                                vmem_limit_bytes=VMEM_LIMIT_BYTES)


def _layer_norm_rows(x, g, b):
    mu = jnp.mean(x, axis=-1, keepdims=True)
    xc = x - mu
    var = jnp.mean(xc * xc, axis=-1, keepdims=True)
    return xc * lax.rsqrt(var + LN_EPS) * g + b


def _silu(x):
    return x * (1.0 / (1.0 + jnp.exp(-x)))


def _matmul_kernel(x_ref, w_ref, o_ref, xb_ref):
    @pl.when(pl.program_id(1) == 0)
    def _():
        xb_ref[...] = x_ref[...].astype(BF16)

    o_ref[...] = jnp.dot(xb_ref[...], w_ref[...].astype(BF16), preferred_element_type=F32)


def matmul_cols(x, w, n_cols, tm, tn):
    m, k = x.shape
    return pl.pallas_call(
        _matmul_kernel,
        grid=(m // tm, n_cols // tn),
        in_specs=[pl.BlockSpec((tm, k), lambda i, j: (i, 0)),
                  pl.BlockSpec((k, tn), lambda i, j: (0, j))],
        out_specs=pl.BlockSpec((tm, tn), lambda i, j: (i, j)),
        out_shape=jax.ShapeDtypeStruct((m, n_cols), F32),
        scratch_shapes=[pltpu.VMEM((tm, k), BF16)],
        compiler_params=_cparams(2),
        name="matmul_cols",
    )(x, w)


def _matmul_f32_kernel(x_ref, w_ref, o_ref):
    o_ref[...] = jnp.dot(x_ref[...], w_ref[...], preferred_element_type=F32,
                         precision=lax.Precision.HIGHEST)


def matmul_narrow_f32(x, w_pad, tm):
    m, k = x.shape
    return pl.pallas_call(
        _matmul_f32_kernel,
        grid=(m // tm,),
        in_specs=[pl.BlockSpec((tm, k), lambda i: (i, 0)),
                  pl.BlockSpec((k, LANES), lambda i: (0, 0))],
        out_specs=pl.BlockSpec((tm, LANES), lambda i: (i, 0)),
        out_shape=jax.ShapeDtypeStruct((m, LANES), F32),
        compiler_params=_cparams(1),
        name="matmul_narrow_f32",
    )(x, w_pad)


def _proj_ln_kernel(o_ref, w_ref, r_ref, g_ref, b_ref, y_ref, wb_ref):
    @pl.when(pl.program_id(0) == 0)
    def _():
        wb_ref[...] = w_ref[...].astype(BF16)

    m = jnp.dot(o_ref[...].astype(BF16), wb_ref[...], preferred_element_type=F32)
    y_ref[...] = _layer_norm_rows(DEEPNORM_ALPHA * r_ref[...] + m, g_ref[...], b_ref[...])


def proj_residual_ln(o, w, resid, g, b, tm):
    m, k = o.shape
    d = w.shape[1]
    return pl.pallas_call(
        _proj_ln_kernel,
        grid=(m // tm,),
        in_specs=[pl.BlockSpec((tm, k), lambda i: (i, 0)),
                  pl.BlockSpec((k, d), lambda i: (0, 0)),
                  pl.BlockSpec((tm, d), lambda i: (i, 0)),
                  pl.BlockSpec((1, d), lambda i: (0, 0)),
                  pl.BlockSpec((1, d), lambda i: (0, 0))],
        out_specs=pl.BlockSpec((tm, d), lambda i: (i, 0)),
        out_shape=jax.ShapeDtypeStruct((m, d), F32),
        scratch_shapes=[pltpu.VMEM((k, d), BF16)],
        compiler_params=_cparams(1),
        name="proj_residual_ln",
    )(o, w, resid, g.reshape(1, d), b.reshape(1, d))


def _ffn_kernel(te_ref, na_ref, x_ref, wg_ref, wu_ref, wd_ref, g_ref, b_ref, o_ref,
                xb_ref, acc_ref, *, fuse_ln):
    i = pl.program_id(0)
    j = pl.program_id(1)
    active = i < na_ref[0]

    @pl.when(jnp.logical_and(active, j == 0))
    def _():
        xb_ref[...] = x_ref[...].astype(BF16)

    @pl.when(active)
    def _():
        xb = xb_ref[...]
        h = jnp.dot(xb, wg_ref[...].astype(BF16), preferred_element_type=F32)
        u = jnp.dot(xb, wu_ref[...].astype(BF16), preferred_element_type=F32)
        a = (_silu(h) * u).astype(BF16)
        c = jnp.dot(a, wd_ref[...].astype(BF16), preferred_element_type=F32)

        @pl.when(j == 0)
        def _():
            acc_ref[...] = c

        @pl.when(j > 0)
        def _():
            acc_ref[...] += c

    @pl.when(jnp.logical_and(active, j == N_FF_CHUNKS - 1))
    def _():
        if fuse_ln:
            o_ref[...] = _layer_norm_rows(DEEPNORM_ALPHA * x_ref[...] + acc_ref[...],
                                          g_ref[...], b_ref[...])
        else:
            o_ref[...] = acc_ref[...]

    @pl.when(jnp.logical_and(jnp.logical_not(active), j == N_FF_CHUNKS - 1))
    def _():
        o_ref[...] = jnp.zeros_like(o_ref)


def swiglu_tiles(x, w_gu, w_down, tile_expert, n_active, ln_g, ln_b, tm, fuse_ln):
    m, d = x.shape
    n_tiles = m // tm

    def row_map(i, j, te, na):
        return (jnp.minimum(i, na[0] - 1), 0)

    def wg_map(i, j, te, na):
        return (te[jnp.minimum(i, na[0] - 1)], 0, j)

    def wu_map(i, j, te, na):
        return (te[jnp.minimum(i, na[0] - 1)], 0, j + N_FF_CHUNKS)

    def wd_map(i, j, te, na):
        return (te[jnp.minimum(i, na[0] - 1)], j, 0)

    grid_spec = pltpu.PrefetchScalarGridSpec(
        num_scalar_prefetch=2,
        grid=(n_tiles, N_FF_CHUNKS),
        in_specs=[pl.BlockSpec((tm, d), row_map),
                  pl.BlockSpec((None, d, FF_CHUNK), wg_map),
                  pl.BlockSpec((None, d, FF_CHUNK), wu_map),
                  pl.BlockSpec((None, FF_CHUNK, d), wd_map),
                  pl.BlockSpec((1, d), lambda i, j, te, na: (0, 0)),
                  pl.BlockSpec((1, d), lambda i, j, te, na: (0, 0))],
        out_specs=pl.BlockSpec((tm, d), lambda i, j, te, na: (i, 0)),
        scratch_shapes=[pltpu.VMEM((tm, d), BF16), pltpu.VMEM((tm, d), F32)],
    )
    return pl.pallas_call(
        functools.partial(_ffn_kernel, fuse_ln=fuse_ln),
        grid_spec=grid_spec,
        out_shape=jax.ShapeDtypeStruct((m, d), F32),
        compiler_params=_cparams(2),
        name="swiglu_ln" if fuse_ln else "swiglu_grouped",
    )(tile_expert, n_active, x, w_gu, w_gu, w_down, ln_g.reshape(1, d), ln_b.reshape(1, d))


def dense_ffn_ln(x, w_gu_all, w_down_all, layer, ln_g, ln_b, tm):
    n_tiles = x.shape[0] // tm
    te = jnp.full((n_tiles,), layer, jnp.int32)
    na = jnp.full((1,), n_tiles, jnp.int32)
    return swiglu_tiles(x, w_gu_all, w_down_all, te, na, ln_g, ln_b, tm, True)


def _router_kernel(x_ref, w_ref, meta_ref, cnt_ref, carry_ref, *, tm):
    i = pl.program_id(0)

    @pl.when(i == 0)
    def _():
        carry_ref[...] = jnp.zeros_like(carry_ref)

    logits = jnp.dot(x_ref[...], w_ref[...], preferred_element_type=F32,
                     precision=lax.Precision.HIGHEST)
    lane = lax.broadcasted_iota(jnp.int32, (tm, LANES), 1)
    neg = jnp.float32(-jnp.inf)
    logits = jnp.where(lane < N_EXPERTS, logits, neg)
    m1 = jnp.max(logits, axis=-1, keepdims=True)
    i1 = jnp.min(jnp.where(logits == m1, lane, LANES), axis=-1, keepdims=True)
    rest = jnp.where(lane == i1, neg, logits)
    m2 = jnp.max(rest, axis=-1, keepdims=True)
    i2 = jnp.min(jnp.where(rest == m2, lane, LANES), axis=-1, keepdims=True)
    e = jnp.exp(m2 - m1)
    g1 = 1.0 / (1.0 + e)
    g2 = e / (1.0 + e)

    chosen = jnp.logical_or(lane == i1, lane == i2)
    onehot = jnp.where(chosen, 1.0, 0.0).astype(BF16)
    row = lax.broadcasted_iota(jnp.int32, (tm, tm), 0)
    col = lax.broadcasted_iota(jnp.int32, (tm, tm), 1)
    strict_lower = jnp.where(col < row, 1.0, 0.0).astype(BF16)
    before = jnp.dot(strict_lower, onehot, preferred_element_type=F32) + carry_ref[...]
    r1 = jnp.sum(jnp.where(lane == i1, before, 0.0), axis=-1, keepdims=True)
    r2 = jnp.sum(jnp.where(lane == i2, before, 0.0), axis=-1, keepdims=True)
    carry_ref[...] += jnp.sum(onehot.astype(F32), axis=0, keepdims=True)

    meta = jnp.where(lane == 0, i1.astype(F32),
           jnp.where(lane == 1, i2.astype(F32),
           jnp.where(lane == 2, g1,
           jnp.where(lane == 3, g2,
           jnp.where(lane == 4, r1,
           jnp.where(lane == 5, r2, 0.0))))))
    meta_ref[...] = meta
    cnt_ref[...] = carry_ref[...]


def route_top2(x, w_router, tm):
    t, d = x.shape
    w_pad = jnp.pad(w_router, ((0, 0), (0, LANES - N_EXPERTS)))
    return pl.pallas_call(
        functools.partial(_router_kernel, tm=tm),
        grid=(t // tm,),
        in_specs=[pl.BlockSpec((tm, d), lambda i: (i, 0)),
                  pl.BlockSpec((d, LANES), lambda i: (0, 0))],
        out_specs=[pl.BlockSpec((tm, LANES), lambda i: (i, 0)),
                   pl.BlockSpec((1, LANES), lambda i: (0, 0))],
        out_shape=[jax.ShapeDtypeStruct((t, LANES), F32),
                   jax.ShapeDtypeStruct((1, LANES), F32)],
        scratch_shapes=[pltpu.VMEM((1, LANES), F32)],
        compiler_params=_cparams(1),
        name="route_top2",
    )(x, w_pad)


def _dispatch_kernel(pos_ref, x_ref, xs_in_hbm, xs_hbm, sem, *, tokens_per_step):
    del xs_in_hbm

    def row_copy(t, k):
        return pltpu.make_async_copy(x_ref.at[pl.ds(t, 1)],
                                     xs_hbm.at[pl.ds(pos_ref[0, 2 * t + k], 1)], sem)

    def issue(t, carry):
        row_copy(t, 0).start()
        row_copy(t, 1).start()
        return carry

    lax.fori_loop(0, tokens_per_step, issue, 0, unroll=8)

    def drain(t, carry):
        row_copy(t, 0).wait()
        row_copy(t, 1).wait()
        return carry

    lax.fori_loop(0, tokens_per_step, drain, 0, unroll=8)


def dispatch_rows(x, pos, n_rows, tokens_per_step):
    t, d = x.shape
    n_steps = t // tokens_per_step
    pos2 = pos.reshape(n_steps, 1, 2 * tokens_per_step)
    zeros = jnp.zeros((n_rows, d), x.dtype)
    return pl.pallas_call(
        functools.partial(_dispatch_kernel, tokens_per_step=tokens_per_step),
        grid=(n_steps,),
        in_specs=[pl.BlockSpec((None, 1, 2 * tokens_per_step), lambda i: (i, 0, 0),
                               memory_space=pltpu.SMEM),
                  pl.BlockSpec((tokens_per_step, d), lambda i: (i, 0)),
                  pl.BlockSpec(memory_space=pl.ANY)],
        out_specs=pl.BlockSpec(memory_space=pl.ANY),
        out_shape=jax.ShapeDtypeStruct((n_rows, d), x.dtype),
        scratch_shapes=[pltpu.SemaphoreType.DMA(())],
        input_output_aliases={2: 0},
        compiler_params=_cparams(1),
        name="dispatch_rows",
    )(pos2, x, zeros)


def _combine_ln_kernel(pos_ref, x_ref, gate_ref, g_ref, b_ref, ys_hbm, o_ref, buf_ref, sem, *, tm):
    def row_copy(t, k):
        return pltpu.make_async_copy(ys_hbm.at[pl.ds(pos_ref[0, 2 * t + k], 1)],
                                     buf_ref.at[k, pl.ds(t, 1)], sem)

    def issue(t, carry):
        row_copy(t, 0).start()
        row_copy(t, 1).start()
        return carry

    lax.fori_loop(0, tm, issue, 0, unroll=8)

    def drain(t, carry):
        row_copy(t, 0).wait()
        row_copy(t, 1).wait()
        return carry

    lax.fori_loop(0, tm, drain, 0, unroll=8)

    gates = gate_ref[...]
    y = gates[:, 2:3] * buf_ref[0] + gates[:, 3:4] * buf_ref[1]
    o_ref[...] = _layer_norm_rows(DEEPNORM_ALPHA * x_ref[...] + y, g_ref[...], b_ref[...])


def combine_ln(x, ys, pos, meta, ln_g, ln_b, tm):
    t, d = x.shape
    n_steps = t // tm
    pos2 = pos.reshape(n_steps, 1, 2 * tm)
    return pl.pallas_call(
        functools.partial(_combine_ln_kernel, tm=tm),
        grid=(n_steps,),
        in_specs=[pl.BlockSpec((None, 1, 2 * tm), lambda i: (i, 0, 0), memory_space=pltpu.SMEM),
                  pl.BlockSpec((tm, d), lambda i: (i, 0)),
                  pl.BlockSpec((tm, LANES), lambda i: (i, 0)),
                  pl.BlockSpec((1, d), lambda i: (0, 0)),
                  pl.BlockSpec((1, d), lambda i: (0, 0)),
                  pl.BlockSpec(memory_space=pl.ANY)],
        out_specs=pl.BlockSpec((tm, d), lambda i: (i, 0)),
        out_shape=jax.ShapeDtypeStruct((t, d), F32),
        scratch_shapes=[pltpu.VMEM((2, tm, d), F32), pltpu.SemaphoreType.DMA(())],
        compiler_params=_cparams(1),
        name="combine_ln",
    )(pos2, x, meta, ln_g.reshape(1, d), ln_b.reshape(1, d), ys)


def moe_ffn_ln(x, w_router, w_gu_all, w_down_all, layer, ln_g, ln_b, tm_route, tm_group, tm_combine):
    t, d = x.shape
    meta, counts = route_top2(x, w_router, tm_route)
    ids = meta[:, 0:2].astype(jnp.int32)
    ranks = meta[:, 4:6].astype(jnp.int32)
    cnt = counts[0, :N_EXPERTS].astype(jnp.int32)
    tiles_per = (cnt + tm_group - 1) // tm_group
    tile_end = jnp.cumsum(tiles_per)
    group_start = (tile_end - tiles_per) * tm_group
    pos = group_start[ids] + ranks
    n_tiles = min((TOP_K * t + N_EXPERTS * (tm_group - 1)) // tm_group,
                  N_EXPERTS * ((t + tm_group - 1) // tm_group))
    n_active = tile_end[N_EXPERTS - 1:N_EXPERTS]
    tile_ids = jnp.arange(n_tiles, dtype=jnp.int32)
    tile_expert = jnp.sum((tile_ids[:, None] >= tile_end[None, :]).astype(jnp.int32), axis=1)
    tile_expert = jnp.minimum(tile_expert, N_EXPERTS - 1) + layer * N_EXPERTS
    xs = dispatch_rows(x, pos, n_tiles * tm_group, min(t, 1024))
    ys = swiglu_tiles(xs, w_gu_all, w_down_all, tile_expert.astype(jnp.int32),
                      n_active.astype(jnp.int32), ln_g, ln_b, tm_group, False)
    return combine_ln(x, ys, pos, meta, ln_g, ln_b, tm_combine)


def _t5_bucket_runs():
    max_exact = N_BUCKETS // 2
    n = np.arange(MAX_DISTANCE)
    scaled = (np.log(np.maximum(n, max_exact).astype(np.float64) / max_exact)
              / math.log(MAX_DISTANCE / max_exact) * (N_BUCKETS - max_exact))
    frac = scaled - np.floor(scaled)
    assert np.all((np.minimum(frac, 1 - frac) > 1e-3) | (n <= max_exact))
    bucket = np.where(n < max_exact, n, np.minimum(max_exact + scaled.astype(np.int64), N_BUCKETS - 1))
    runs = []
    for dist in range(MAX_DISTANCE):
        if runs and runs[-1][1] == int(bucket[dist]):
            runs[-1] = (dist, int(bucket[dist]))
        else:
            runs.append((dist, int(bucket[dist])))
    return runs, bucket


T5_RUNS, T5_BUCKETS = _t5_bucket_runs()
ATT_TQ = 512


def _bias_tile_kernel(rel_ref, o_ref):
    h = pl.program_id(0)
    tq, tk2 = o_ref.shape
    row = lax.broadcasted_iota(jnp.int32, (tq, tk2), 0)
    col = lax.broadcasted_iota(jnp.int32, (tq, tk2), 1)
    dist = row - col + tq
    far = rel_ref[N_BUCKETS - 1, h]
    acc = jnp.zeros((tq, tk2), F32)
    for last, bucket in reversed(T5_RUNS):
        acc = jnp.where(dist <= last, rel_ref[bucket, h] - far, acc)
    o_ref[...] = jnp.where(dist < 0, -jnp.inf, acc)


def attn_bias_tiles(rel_bias):
    return pl.pallas_call(
        _bias_tile_kernel,
        grid=(ATT_HEADS,),
        in_specs=[pl.BlockSpec(memory_space=pltpu.SMEM)],
        out_specs=pl.BlockSpec((None, ATT_TQ, 2 * ATT_TQ), lambda h: (h, 0, 0)),
        out_shape=jax.ShapeDtypeStruct((ATT_HEADS, ATT_TQ, 2 * ATT_TQ), F32),
        compiler_params=_cparams(1),
        name="attn_bias_tiles",
    )(rel_bias)


def _softmax_step(s, v, m_ref, l_ref, acc_ref):
    m_prev = m_ref[...]
    m_new = jnp.maximum(m_prev, jnp.max(s, axis=-1, keepdims=True))
    alpha = jnp.exp(m_prev - m_new)
    p = jnp.exp(s - m_new[:, 0:1])
    l_ref[...] = alpha * l_ref[...] + jnp.sum(p, axis=-1, keepdims=True)
    acc_ref[...] = alpha * acc_ref[...] + jnp.dot(p.astype(BF16), v, preferred_element_type=F32)
    m_ref[...] = m_new


def _attn_prompt_kernel(lam_ref, q_ref, k_ref, v_ref, bt_ref, g_ref, o_ref,
                        kb_ref, vb_ref, m1_ref, l1_ref, a1_ref, m2_ref, l2_ref, a2_ref, *, out_scale):
    qi = pl.program_id(2)
    tq = q_ref.shape[0]
    half = ATT_HEAD_DIM

    @pl.when(qi == 0)
    def _():
        kb_ref[...] = k_ref[...].astype(BF16)
        vb_ref[...] = v_ref[...].astype(BF16)

    q = q_ref[...] * (ATT_HEAD_DIM ** -0.5)
    lane = lax.broadcasted_iota(jnp.int32, q.shape, 1)
    q1 = jnp.where(lane < half, q, 0.0).astype(BF16)
    q2 = jnp.where(lane >= half, q, 0.0).astype(BF16)

    for m_ref, l_ref, a_ref in ((m1_ref, l1_ref, a1_ref), (m2_ref, l2_ref, a2_ref)):
        m_ref[...] = jnp.full(m_ref.shape, -jnp.inf, F32)
        l_ref[...] = jnp.zeros(l_ref.shape, F32)
        a_ref[...] = jnp.zeros(a_ref.shape, F32)

    nt = (((1,), (1,)), ((), ()))

    def chunk(start, bias):
        k = kb_ref[pl.ds(start, tq), :]
        v = vb_ref[pl.ds(start, tq), :]
        s1 = lax.dot_general(q1, k, nt, preferred_element_type=F32)
        s2 = lax.dot_general(q2, k, nt, preferred_element_type=F32)
        if bias is not None:
            s1 = s1 + bias
            s2 = s2 + bias
        _softmax_step(s1, v, m1_ref, l1_ref, a1_ref)
        _softmax_step(s2, v, m2_ref, l2_ref, a2_ref)

    def far_chunk(c, carry):
        chunk(pl.multiple_of(c * tq, tq), None)
        return carry

    lax.fori_loop(0, qi - 1, far_chunk, 0)

    @pl.when(qi > 0)
    def _():
        chunk(pl.multiple_of((qi - 1) * tq, tq), bt_ref[:, 0:tq])

    chunk(pl.multiple_of(qi * tq, tq), bt_ref[:, tq:2 * tq])

    o = a1_ref[...] / l1_ref[...] - lam_ref[0] * (a2_ref[...] / l2_ref[...])
    o = o * lax.rsqrt(jnp.mean(o * o, axis=-1, keepdims=True) + RMS_EPS) * g_ref[...]
    o_ref[...] = o * out_scale


def attn_prompt(qkv, lam, lam_init, bias_tiles, subln_g):
    bn, l, _ = qkv.shape
    hd = 2 * ATT_HEAD_DIM
    tq = ATT_TQ
    stat = pltpu.VMEM((tq, hd), F32)
    return pl.pallas_call(
        functools.partial(_attn_prompt_kernel, out_scale=1.0 - lam_init),
        grid=(bn, ATT_HEADS, l // tq),
        in_specs=[pl.BlockSpec(memory_space=pltpu.SMEM),
                  pl.BlockSpec((None, tq, hd), lambda b, h, i: (b, i, h)),
                  pl.BlockSpec((None, l, hd), lambda b, h, i: (b, 0, ATT_HEADS + h)),
                  pl.BlockSpec((None, l, hd), lambda b, h, i: (b, 0, 2 * ATT_HEADS + h)),
                  pl.BlockSpec((None, tq, 2 * tq), lambda b, h, i: (h, 0, 0)),
                  pl.BlockSpec((1, hd), lambda b, h, i: (0, 0))],
        out_specs=pl.BlockSpec((None, tq, hd), lambda b, h, i: (b, i, h)),
        out_shape=jax.ShapeDtypeStruct((bn, l, ATT_HEADS * hd), F32),
        scratch_shapes=[pltpu.VMEM((l, hd), BF16), pltpu.VMEM((l, hd), BF16),
                        stat, stat, stat, stat, stat, stat],
        compiler_params=_cparams(3),
        name="attn_prompt",
    )(lam.reshape(1), qkv, qkv, qkv, bias_tiles, subln_g.reshape(1, hd))


def _rms_norm(x, g):
    return x * lax.rsqrt(jnp.mean(jnp.square(x), -1, keepdims=True) + RMS_EPS) * g


def _l2_normalize(x):
    return x * lax.rsqrt(jnp.sum(jnp.square(x), -1, keepdims=True) + 1e-6)


def _causal_dwconv(x, buf, w, b):
    l = x.shape[1]
    xp = jnp.concatenate([buf.astype(x.dtype), x], axis=1)
    y = sum(xp[:, i:i + l] * w[i] for i in range(CONV_W))
    if b is not None:
        y = y + b
    return y, xp[:, l:]


def _to_chunks(t, c, pad):
    t = jnp.pad(t, [(0, 0), (0, pad)] + [(0, 0)] * (t.ndim - 2))
    t = t.reshape((t.shape[0], t.shape[1] // c, c) + t.shape[2:])
    return jnp.swapaxes(t, 0, 1)


def _gated_delta_rule(q, k, v, g, beta, s0):
    bn, l, h, dk = q.shape
    dv = v.shape[-1]
    c = min(GDN_CHUNK, l)
    pad = (-l) % c
    qc, kc, vc, gc, bc = [jnp.swapaxes(_to_chunks(t, c, pad), 2, 3) for t in (q, k, v, g, beta)]
    gcum = jnp.cumsum(gc, axis=-1)
    incl = jnp.tril(jnp.ones((c, c), bool))
    strict = jnp.tril(jnp.ones((c, c), bool), -1)
    diff = gcum[..., :, None] - gcum[..., None, :]
    decay = jnp.where(incl, jnp.exp(jnp.where(incl, diff, 0.0)), 0.0)
    m = jnp.where(strict, bc[..., :, None] * jnp.einsum('zbhid,zbhjd->zbhij', kc, kc) * decay, 0.0)
    a = m + jnp.eye(c, dtype=m.dtype)
    rhs = jnp.concatenate([bc[..., None] * vc, (bc * jnp.exp(gcum))[..., None] * kc], axis=-1)
    w = lax.linalg.triangular_solve(a, rhs, left_side=True, lower=True, unit_diagonal=True)
    u_v, u_k = w[..., :dv], w[..., dv:]
    qk = jnp.where(incl, jnp.einsum('zbhid,zbhjd->zbhij', qc, kc) * decay, 0.0)
    q_dec = qc * jnp.exp(gcum)[..., None]
    k_end = kc * jnp.exp(gcum[..., -1:] - gcum)[..., None]
    g_end = jnp.exp(gcum[..., -1])

    def step(s, inp):
        uv, uk, qkc, qd, ke, ge = inp
        u = uv - jnp.einsum('bhcd,bhde->bhce', uk, s)
        o = jnp.einsum('bhcd,bhde->bhce', qd, s) + jnp.einsum('bhij,bhje->bhie', qkc, u)
        s = ge[..., None, None] * s + jnp.einsum('bhcd,bhce->bhde', ke, u)
        return s, o

    s_fin, o = lax.scan(step, s0, (u_v, u_k, qk, q_dec, k_end, g_end))
    o = jnp.swapaxes(jnp.swapaxes(o, 2, 3), 0, 1)
    o = o.reshape((bn, o.shape[1] * c, h, dv))[:, :l]
    return o, s_fin


def gdn_core(proj, ba, s0, conv0, conv_w, a_log, dt_bias, norm_g):
    bn, l, _ = proj.shape
    qkv, z = proj[..., :GDN_CONV_DIM], proj[..., GDN_CONV_DIM:]
    b, a = ba[..., :GDN_HEADS], ba[..., GDN_HEADS:2 * GDN_HEADS]
    qkv, conv_new = _causal_dwconv(qkv, conv0, conv_w, None)
    qkv = jax.nn.silu(qkv)
    q, k, v = jnp.split(qkv, [GDN_KD, 2 * GDN_KD], axis=-1)
    q = _l2_normalize(q.reshape(bn, l, GDN_HEADS, GDN_DK)) * GDN_DK ** -0.5
    k = _l2_normalize(k.reshape(bn, l, GDN_HEADS, GDN_DK))
    v = v.reshape(bn, l, GDN_HEADS, GDN_DV)
    beta = jax.nn.sigmoid(b)
    g = -jnp.exp(a_log) * jax.nn.softplus(a + dt_bias)
    o, s = _gated_delta_rule(q, k, v, g, beta, s0)
    o = _rms_norm(o, norm_g) * jax.nn.silu(z).reshape(bn, l, GDN_HEADS, GDN_DV)
    return o.reshape(bn, l, GDN_VD), s, conv_new


def _ssd_chunked(x, dt, a_neg, bm, cm, h0):
    bn, l = x.shape[:2]
    c = min(SSM_CHUNK, l)
    pad = (-l) % c
    xc, dtc, bc, cc = [_to_chunks(t, c, pad) for t in (x, dt, bm, cm)]
    acum = jnp.cumsum(dtc * a_neg, axis=2)
    incl = jnp.tril(jnp.ones((c, c), bool))[:, :, None, None]
    diff = acum[:, :, :, None] - acum[:, :, None, :]
    lmat = jnp.where(incl, jnp.exp(jnp.where(incl, diff, 0.0)), 0.0)
    cb = jnp.einsum('zbign,zbjgn->zbijg', cc, bc)
    scores = cb[..., None] * lmat * dtc[:, :, None]
    y_intra = jnp.einsum('zbijgr,zbjgrp->zbigrp', scores, xc)
    a_end = acum[:, :, -1]
    w_end = jnp.exp(a_end[:, :, None] - acum) * dtc
    dstate = jnp.einsum('zbjgr,zbjgn,zbjgrp->zbgrpn', w_end, bc, xc)

    def step(h, inp):
        cq, ac, ds, ae = inp
        y = jnp.exp(ac)[..., None] * jnp.einsum('bign,bgrpn->bigrp', cq, h)
        h = jnp.exp(ae)[..., None, None] * h + ds
        return h, y

    h_fin, y_inter = lax.scan(step, h0, (cc, acum, dstate, a_end))
    y = jnp.swapaxes(y_intra + y_inter, 0, 1)
    y = y.reshape((bn, y.shape[1] * c) + y.shape[3:])[:, :l]
    return y, h_fin


def ssd_core(proj, dtp, h0, conv0, conv_w, conv_b, a_log, dt_bias, d_skip, norm_g):
    bn, l, _ = proj.shape
    gn = SSM_GROUPS * SSM_STATE
    z, xbc = proj[..., :SSM_D_INNER], proj[..., SSM_D_INNER:]
    dt = dtp[..., :SSM_HEADS]
    xbc, conv_new = _causal_dwconv(xbc, conv0, conv_w, conv_b)
    xbc = jax.nn.silu(xbc)
    xs, bm, cm = jnp.split(xbc, [SSM_D_INNER, SSM_D_INNER + gn], axis=-1)
    xs = xs.reshape(bn, l, SSM_GROUPS, SSM_HPG, SSM_HEAD_DIM)
    bm = bm.reshape(bn, l, SSM_GROUPS, SSM_STATE)
    cm = cm.reshape(bn, l, SSM_GROUPS, SSM_STATE)
    dt = jax.nn.softplus(dt + dt_bias).reshape(bn, l, SSM_GROUPS, SSM_HPG)
    a_neg = -jnp.exp(a_log).reshape(SSM_GROUPS, SSM_HPG)
    h0g = h0.reshape(bn, SSM_GROUPS, SSM_HPG, SSM_HEAD_DIM, SSM_STATE)
    y, h = _ssd_chunked(xs, dt, a_neg, bm, cm, h0g)
    y = y + d_skip.reshape(SSM_GROUPS, SSM_HPG)[..., None] * xs
    y = y.reshape(bn, l, SSM_D_INNER) * jax.nn.silu(z)
    y = _rms_norm(y.reshape(bn, l, SSM_GROUPS, SSM_D_INNER // SSM_GROUPS),
                  norm_g.reshape(SSM_GROUPS, SSM_D_INNER // SSM_GROUPS)).reshape(bn, l, SSM_D_INNER)
    h = h.reshape(bn, SSM_HEADS, SSM_HEAD_DIM, SSM_STATE)
    return y, h, conv_new


def _t5_bias(rel_table, q_pos, k_pos):
    n = jnp.maximum(q_pos[:, None] - k_pos[None, :], 0)
    max_exact = N_BUCKETS // 2
    large = max_exact + (jnp.log(jnp.maximum(n, max_exact).astype(F32) / max_exact)
                         / math.log(MAX_DISTANCE / max_exact) * (N_BUCKETS - max_exact)).astype(jnp.int32)
    bucket = jnp.where(n < max_exact, n, jnp.minimum(large, N_BUCKETS - 1))
    return jnp.transpose(rel_table[bucket], (2, 0, 1)).astype(F32)


def _diff_scores(q, k, q_pos, k_pos, rel_table):
    s = jnp.einsum('bqhcd,bkhcd->bchqk', q, k, preferred_element_type=F32) * ATT_HEAD_DIM ** -0.5
    s = s + _t5_bias(rel_table, q_pos, k_pos)
    return jnp.where(k_pos[None, :] <= q_pos[:, None], s, -jnp.inf)


def _diff_weights(s, lam):
    p = jax.nn.softmax(s, axis=-1)
    return p[:, 0] - lam * p[:, 1]


def _split_qkv(qkv):
    bn, l, _ = qkv.shape
    q, k, v = jnp.split(qkv, 3, axis=-1)
    q = q.reshape(bn, l, ATT_HEADS, 2, ATT_HEAD_DIM)
    k = k.reshape(bn, l, ATT_HEADS, 2, ATT_HEAD_DIM)
    v = v.reshape(bn, l, ATT_HEADS, 2 * ATT_HEAD_DIM)
    return q, k, v


def attn_sample_core(qkv, k_pages, v_pages, page_table, lam, lam_init, rel_table, subln_g):
    bn, l, _ = qkv.shape
    q, k, v = _split_qkv(qkv)
    past = page_table.shape[1] * PAGE_SIZE
    k_past = k_pages[page_table].reshape(bn, past, ATT_HEADS, 2, ATT_HEAD_DIM)
    v_past = v_pages[page_table].reshape(bn, past, ATT_HEADS, 2 * ATT_HEAD_DIM)
    q_pos = past + jnp.arange(l, dtype=jnp.int32)
    s = jnp.concatenate([_diff_scores(q, k_past, q_pos, jnp.arange(past, dtype=jnp.int32), rel_table),
                         _diff_scores(q, k, q_pos, q_pos, rel_table)], axis=-1)
    a = _diff_weights(s, lam)
    o = (jnp.einsum('bhqk,bkhe->bqhe', a[..., :past], v_past)
         + jnp.einsum('bhqk,bkhe->bqhe', a[..., past:], v))
    o = _rms_norm(o, subln_g) * (1.0 - lam_init)
    return o.reshape(bn, l, D_MODEL), k.reshape(bn, l, ATT_HEADS, 2 * ATT_HEAD_DIM), v


def _pad_cols(w):
    return jnp.pad(w, ((0, 0), (0, LANES - w.shape[1])))


def kernel(x_prompt, x_sample, state_gdn_S, state_gdn_conv, state_ssm_h, state_ssm_conv,
           cache_attn_k, cache_attn_v, page_table,
           gdn_w_in, gdn_conv_w, gdn_a_log, gdn_dt_bias, gdn_norm_g, gdn_w_out,
           ssm_w_in, ssm_conv_w, ssm_conv_b, ssm_a_log, ssm_dt_bias, ssm_d_skip, ssm_norm_g, ssm_w_out,
           att_w_qkv, att_lam_q1, att_lam_k1, att_lam_q2, att_lam_k2, att_subln_g, att_w_out, rel_bias,
           ffn_w_gu, ffn_w_down, moe_w_router, moe_w_gu, moe_w_down, ln_g, ln_b):
    bp, lp, d = x_prompt.shape
    bs, ls, _ = x_sample.shape
    tp = bp * lp
    ts = bs * ls
    xp = x_prompt.reshape(tp, d)
    xs = x_sample.reshape(ts, d)
    moe_gu = moe_w_gu.reshape((-1,) + moe_w_gu.shape[2:])
    moe_down = moe_w_down.reshape((-1,) + moe_w_down.shape[2:])

    tm_p = 1024
    tm_s = ts

    outs = {name: [] for name in ("gS_p", "gS_s", "gc_p", "gc_s", "sh_p", "sh_s", "sc_p", "sc_s",
                                  "ak_p", "ak_s", "av_p", "av_s")}
    for i in range(DEPTH):
        j = i // N_MIXERS
        if i % N_MIXERS == 0:
            n_main = GDN_CONV_DIM + GDN_VD
            w_ba = _pad_cols(gdn_w_in[j][:, n_main:])
            prm = (gdn_conv_w[j], gdn_a_log[j], gdn_dt_bias[j], gdn_norm_g[j])
            proj_p = matmul_cols(xp, gdn_w_in[j], n_main, tm_p, 512).reshape(bp, lp, n_main)
            ba_p = matmul_narrow_f32(xp, w_ba, tm_p).reshape(bp, lp, LANES)
            s0 = jnp.zeros((bp,) + state_gdn_S.shape[2:], F32)
            c0 = jnp.zeros((bp,) + state_gdn_conv.shape[2:], F32)
            op, s_p, c_p = gdn_core(proj_p, ba_p, s0, c0, *prm)
            proj_s = matmul_cols(xs, gdn_w_in[j], n_main, tm_s, 512).reshape(bs, ls, n_main)
            ba_s = matmul_narrow_f32(xs, w_ba, tm_s).reshape(bs, ls, LANES)
            os_, s_s, c_s = gdn_core(proj_s, ba_s, state_gdn_S[j], state_gdn_conv[j], *prm)
            outs["gS_p"].append(s_p); outs["gS_s"].append(s_s)
            outs["gc_p"].append(c_p); outs["gc_s"].append(c_s)
            w_out = gdn_w_out[j]
        elif i % N_MIXERS == 1:
            n_main = SSM_D_INNER + SSM_CONV_DIM
            w_dt = _pad_cols(ssm_w_in[j][:, n_main:])
            prm = (ssm_conv_w[j], ssm_conv_b[j], ssm_a_log[j], ssm_dt_bias[j], ssm_d_skip[j], ssm_norm_g[j])
            proj_p = matmul_cols(xp, ssm_w_in[j], n_main, tm_p, 512).reshape(bp, lp, n_main)
            dt_p = matmul_narrow_f32(xp, w_dt, tm_p).reshape(bp, lp, LANES)
            h0 = jnp.zeros((bp,) + state_ssm_h.shape[2:], F32)
            c0 = jnp.zeros((bp,) + state_ssm_conv.shape[2:], F32)
            op, h_p, c_p = ssd_core(proj_p, dt_p, h0, c0, *prm)
            proj_s = matmul_cols(xs, ssm_w_in[j], n_main, tm_s, 512).reshape(bs, ls, n_main)
            dt_s = matmul_narrow_f32(xs, w_dt, tm_s).reshape(bs, ls, LANES)
            os_, h_s, c_s = ssd_core(proj_s, dt_s, state_ssm_h[j], state_ssm_conv[j], *prm)
            outs["sh_p"].append(h_p); outs["sh_s"].append(h_s)
            outs["sc_p"].append(c_p); outs["sc_s"].append(c_s)
            w_out = ssm_w_out[j]
        else:
            lam_init = 0.8 - 0.6 * math.exp(-0.3 * i)
            lam = (jnp.exp(jnp.sum(att_lam_q1[j] * att_lam_k1[j]))
                   - jnp.exp(jnp.sum(att_lam_q2[j] * att_lam_k2[j])) + lam_init)
            qkv_p = matmul_cols(xp, att_w_qkv[j], 3 * d, tm_p, 512).reshape(bp, lp, 3 * d)
            op = attn_prompt(qkv_p, lam, lam_init, attn_bias_tiles(rel_bias), att_subln_g[j])
            k_p = qkv_p[..., d:2 * d].reshape(bp, lp, ATT_HEADS, 2 * ATT_HEAD_DIM)
            v_p = qkv_p[..., 2 * d:].reshape(bp, lp, ATT_HEADS, 2 * ATT_HEAD_DIM)
            qkv_s = matmul_cols(xs, att_w_qkv[j], 3 * d, tm_s, 512).reshape(bs, ls, 3 * d)
            os_, k_s, v_s = attn_sample_core(qkv_s, cache_attn_k[j], cache_attn_v[j], page_table,
                                             lam, lam_init, rel_bias, att_subln_g[j])
            outs["ak_p"].append(k_p); outs["ak_s"].append(k_s)
            outs["av_p"].append(v_p); outs["av_s"].append(v_s)
            w_out = att_w_out[j]
        xp = proj_residual_ln(op.reshape(tp, -1), w_out, xp, ln_g[i, 0], ln_b[i, 0], 512)
        xs = proj_residual_ln(os_.reshape(ts, -1), w_out, xs, ln_g[i, 0], ln_b[i, 0], tm_s)
        f = i // 2
        if i % 2 == 0:
            xp = dense_ffn_ln(xp, ffn_w_gu, ffn_w_down, f, ln_g[i, 1], ln_b[i, 1], tm_p)
            xs = dense_ffn_ln(xs, ffn_w_gu, ffn_w_down, f, ln_g[i, 1], ln_b[i, 1], tm_s)
        else:
            xp = moe_ffn_ln(xp, moe_w_router[f], moe_gu, moe_down, f, ln_g[i, 1], ln_b[i, 1],
                            512, 1024, 256)
            xs = moe_ffn_ln(xs, moe_w_router[f], moe_gu, moe_down, f, ln_g[i, 1], ln_b[i, 1],
                            ts, ts, ts)
    st = lambda name: jnp.stack(outs[name])
    return (xp.reshape(bp, lp, d), xs.reshape(bs, ls, d),
            st("gS_p"), st("gS_s"), st("gc_p"), st("gc_s"),
            st("sh_p"), st("sh_s"), st("sc_p"), st("sc_s"),
            st("ak_p"), st("ak_s"), st("av_p"), st("av_s"))
```

```python
import functools
import math

import numpy as np
import jax
import jax.numpy as jnp
from jax import lax
from jax.experimental import pallas as pl
from jax.experimental.pallas import tpu as pltpu

F32 = jnp.float32
BF16 = jnp.bfloat16

D_MODEL = 1024
DEPTH = 4
PAGE_SIZE = 128
N_MIXERS = 3
CONV_W = 4

GDN_HEADS = 8
GDN_DK = 128
GDN_DV = 128
GDN_KD = GDN_HEADS * GDN_DK
GDN_VD = GDN_HEADS * GDN_DV
GDN_CONV_DIM = 2 * GDN_KD + GDN_VD
GDN_CHUNK = 64

SSM_D_INNER = 2 * D_MODEL
SSM_HEAD_DIM = 64
SSM_HEADS = SSM_D_INNER // SSM_HEAD_DIM
SSM_GROUPS = 4
SSM_HPG = SSM_HEADS // SSM_GROUPS
SSM_STATE = 128
SSM_CONV_DIM = SSM_D_INNER + 2 * SSM_GROUPS * SSM_STATE
SSM_CHUNK = 128

ATT_HEADS = 8
ATT_HEAD_DIM = D_MODEL // (2 * ATT_HEADS)
N_BUCKETS = 32
MAX_DISTANCE = 128
Q_BLOCK = 128

D_FF = 3584
N_EXPERTS = 8
TOP_K = 2

DEEPNORM_ALPHA = (2.0 * DEPTH) ** 0.25
LN_EPS = 1e-5
RMS_EPS = 1e-6

V7X_VMEM_BYTES = 64 * 1024 * 1024
VMEM_LIMIT_BYTES = V7X_VMEM_BYTES - 8 * 1024 * 1024
LANES = 128

FF_CHUNK = 512
N_FF_CHUNKS = D_FF // FF_CHUNK


def _cparams(n_axes):
    return pltpu.CompilerParams(dimension_semantics=("arbitrary",) * n_axes,
                                vmem_limit_bytes=VMEM_LIMIT_BYTES)


def _layer_norm_rows(x, g, b):
    mu = jnp.mean(x, axis=-1, keepdims=True)
    xc = x - mu
    var = jnp.mean(xc * xc, axis=-1, keepdims=True)
    return xc * lax.rsqrt(var + LN_EPS) * g + b


def _silu(x):
    return x * (1.0 / (1.0 + jnp.exp(-x)))


def _matmul_kernel(x_ref, w_ref, o_ref, xb_ref):
    @pl.when(pl.program_id(1) == 0)
    def _():
        xb_ref[...] = x_ref[...].astype(BF16)

    o_ref[...] = jnp.dot(xb_ref[...], w_ref[...].astype(BF16), preferred_element_type=F32)


def matmul_cols(x, w, n_cols, tm, tn):
    m, k = x.shape
    return pl.pallas_call(
        _matmul_kernel,
        grid=(m // tm, n_cols // tn),
        in_specs=[pl.BlockSpec((tm, k), lambda i, j: (i, 0)),
                  pl.BlockSpec((k, tn), lambda i, j: (0, j))],
        out_specs=pl.BlockSpec((tm, tn), lambda i, j: (i, j)),
        out_shape=jax.ShapeDtypeStruct((m, n_cols), F32),
        scratch_shapes=[pltpu.VMEM((tm, k), BF16)],
        compiler_params=_cparams(2),
        name="matmul_cols",
    )(x, w)


def _matmul_f32_kernel(x_ref, w_ref, o_ref):
    o_ref[...] = jnp.dot(x_ref[...], w_ref[...], preferred_element_type=F32,
                         precision=lax.Precision.HIGHEST)


def matmul_narrow_f32(x, w_pad, tm):
    m, k = x.shape
    return pl.pallas_call(
        _matmul_f32_kernel,
        grid=(m // tm,),
        in_specs=[pl.BlockSpec((tm, k), lambda i: (i, 0)),
                  pl.BlockSpec((k, LANES), lambda i: (0, 0))],
        out_specs=pl.BlockSpec((tm, LANES), lambda i: (i, 0)),
        out_shape=jax.ShapeDtypeStruct((m, LANES), F32),
        compiler_params=_cparams(1),
        name="matmul_narrow_f32",
    )(x, w_pad)


def _proj_ln_kernel(o_ref, w_ref, r_ref, g_ref, b_ref, y_ref, wb_ref):
    @pl.when(pl.program_id(0) == 0)
    def _():
        wb_ref[...] = w_ref[...].astype(BF16)

    m = jnp.dot(o_ref[...].astype(BF16), wb_ref[...], preferred_element_type=F32)
    y_ref[...] = _layer_norm_rows(DEEPNORM_ALPHA * r_ref[...] + m, g_ref[...], b_ref[...])


def proj_residual_ln(o, w, resid, g, b, tm):
    m, k = o.shape
    d = w.shape[1]
    return pl.pallas_call(
        _proj_ln_kernel,
        grid=(m // tm,),
        in_specs=[pl.BlockSpec((tm, k), lambda i: (i, 0)),
                  pl.BlockSpec((k, d), lambda i: (0, 0)),
                  pl.BlockSpec((tm, d), lambda i: (i, 0)),
                  pl.BlockSpec((1, d), lambda i: (0, 0)),
                  pl.BlockSpec((1, d), lambda i: (0, 0))],
        out_specs=pl.BlockSpec((tm, d), lambda i: (i, 0)),
        out_shape=jax.ShapeDtypeStruct((m, d), F32),
        scratch_shapes=[pltpu.VMEM((k, d), BF16)],
        compiler_params=_cparams(1),
        name="proj_residual_ln",
    )(o, w, resid, g.reshape(1, d), b.reshape(1, d))


def _ffn_kernel(te_ref, na_ref, x_ref, wg_ref, wu_ref, wd_ref, g_ref, b_ref, o_ref,
                xb_ref, acc_ref, *, fuse_ln):
    i = pl.program_id(0)
    j = pl.program_id(1)
    active = i < na_ref[0]

    @pl.when(jnp.logical_and(active, j == 0))
    def _():
        xb_ref[...] = x_ref[...].astype(BF16)

    @pl.when(active)
    def _():
        xb = xb_ref[...]
        h = jnp.dot(xb, wg_ref[...].astype(BF16), preferred_element_type=F32)
        u = jnp.dot(xb, wu_ref[...].astype(BF16), preferred_element_type=F32)
        a = (_silu(h) * u).astype(BF16)
        c = jnp.dot(a, wd_ref[...].astype(BF16), preferred_element_type=F32)

        @pl.when(j == 0)
        def _():
            acc_ref[...] = c

        @pl.when(j > 0)
        def _():
            acc_ref[...] += c

    @pl.when(jnp.logical_and(active, j == N_FF_CHUNKS - 1))
    def _():
        if fuse_ln:
            o_ref[...] = _layer_norm_rows(DEEPNORM_ALPHA * x_ref[...] + acc_ref[...],
                                          g_ref[...], b_ref[...])
        else:
            o_ref[...] = acc_ref[...]

    @pl.when(jnp.logical_and(jnp.logical_not(active), j == N_FF_CHUNKS - 1))
    def _():
        o_ref[...] = jnp.zeros_like(o_ref)


def swiglu_tiles(x, w_gu, w_down, tile_expert, n_active, ln_g, ln_b, tm, fuse_ln):
    m, d = x.shape
    n_tiles = m // tm

    def row_map(i, j, te, na):
        return (jnp.minimum(i, na[0] - 1), 0)

    def wg_map(i, j, te, na):
        return (te[jnp.minimum(i, na[0] - 1)], 0, j)

    def wu_map(i, j, te, na):
        return (te[jnp.minimum(i, na[0] - 1)], 0, j + N_FF_CHUNKS)

    def wd_map(i, j, te, na):
        return (te[jnp.minimum(i, na[0] - 1)], j, 0)

    grid_spec = pltpu.PrefetchScalarGridSpec(
        num_scalar_prefetch=2,
        grid=(n_tiles, N_FF_CHUNKS),
        in_specs=[pl.BlockSpec((tm, d), row_map),
                  pl.BlockSpec((None, d, FF_CHUNK), wg_map),
                  pl.BlockSpec((None, d, FF_CHUNK), wu_map),
                  pl.BlockSpec((None, FF_CHUNK, d), wd_map),
                  pl.BlockSpec((1, d), lambda i, j, te, na: (0, 0)),
                  pl.BlockSpec((1, d), lambda i, j, te, na: (0, 0))],
        out_specs=pl.BlockSpec((tm, d), lambda i, j, te, na: (i, 0)),
        scratch_shapes=[pltpu.VMEM((tm, d), BF16), pltpu.VMEM((tm, d), F32)],
    )
    return pl.pallas_call(
        functools.partial(_ffn_kernel, fuse_ln=fuse_ln),
        grid_spec=grid_spec,
        out_shape=jax.ShapeDtypeStruct((m, d), F32),
        compiler_params=_cparams(2),
        name="swiglu_ln" if fuse_ln else "swiglu_grouped",
    )(tile_expert, n_active, x, w_gu, w_gu, w_down, ln_g.reshape(1, d), ln_b.reshape(1, d))


def dense_ffn_ln(x, w_gu_all, w_down_all, layer, ln_g, ln_b, tm):
    n_tiles = x.shape[0] // tm
    te = jnp.full((n_tiles,), layer, jnp.int32)
    na = jnp.full((1,), n_tiles, jnp.int32)
    return swiglu_tiles(x, w_gu_all, w_down_all, te, na, ln_g, ln_b, tm, True)


def _router_kernel(x_ref, w_ref, meta_ref, cnt_ref, carry_ref, *, tm):
    i = pl.program_id(0)

    @pl.when(i == 0)
    def _():
        carry_ref[...] = jnp.zeros_like(carry_ref)

    logits = jnp.dot(x_ref[...], w_ref[...], preferred_element_type=F32,
                     precision=lax.Precision.HIGHEST)
    lane = lax.broadcasted_iota(jnp.int32, (tm, LANES), 1)
    neg = jnp.float32(-jnp.inf)
    logits = jnp.where(lane < N_EXPERTS, logits, neg)
    m1 = jnp.max(logits, axis=-1, keepdims=True)
    i1 = jnp.min(jnp.where(logits == m1, lane, LANES), axis=-1, keepdims=True)
    rest = jnp.where(lane == i1, neg, logits)
    m2 = jnp.max(rest, axis=-1, keepdims=True)
    i2 = jnp.min(jnp.where(rest == m2, lane, LANES), axis=-1, keepdims=True)
    e = jnp.exp(m2 - m1)
    g1 = 1.0 / (1.0 + e)
    g2 = e / (1.0 + e)

    chosen = jnp.logical_or(lane == i1, lane == i2)
    onehot = jnp.where(chosen, 1.0, 0.0).astype(BF16)
    row = lax.broadcasted_iota(jnp.int32, (tm, tm), 0)
    col = lax.broadcasted_iota(jnp.int32, (tm, tm), 1)
    strict_lower = jnp.where(col < row, 1.0, 0.0).astype(BF16)
    before = jnp.dot(strict_lower, onehot, preferred_element_type=F32) + carry_ref[...]
    r1 = jnp.sum(jnp.where(lane == i1, before, 0.0), axis=-1, keepdims=True)
    r2 = jnp.sum(jnp.where(lane == i2, before, 0.0), axis=-1, keepdims=True)
    carry_ref[...] += jnp.sum(onehot.astype(F32), axis=0, keepdims=True)

    meta = jnp.where(lane == 0, i1.astype(F32),
           jnp.where(lane == 1, i2.astype(F32),
           jnp.where(lane == 2, g1,
           jnp.where(lane == 3, g2,
           jnp.where(lane == 4, r1,
           jnp.where(lane == 5, r2, 0.0))))))
    meta_ref[...] = meta
    cnt_ref[...] = carry_ref[...]


def route_top2(x, w_router, tm):
    t, d = x.shape
    w_pad = jnp.pad(w_router, ((0, 0), (0, LANES - N_EXPERTS)))
    return pl.pallas_call(
        functools.partial(_router_kernel, tm=tm),
        grid=(t // tm,),
        in_specs=[pl.BlockSpec((tm, d), lambda i: (i, 0)),
                  pl.BlockSpec((d, LANES), lambda i: (0, 0))],
        out_specs=[pl.BlockSpec((tm, LANES), lambda i: (i, 0)),
                   pl.BlockSpec((1, LANES), lambda i: (0, 0))],
        out_shape=[jax.ShapeDtypeStruct((t, LANES), F32),
                   jax.ShapeDtypeStruct((1, LANES), F32)],
        scratch_shapes=[pltpu.VMEM((1, LANES), F32)],
        compiler_params=_cparams(1),
        name="route_top2",
    )(x, w_pad)


def _dispatch_kernel(pos_ref, x_ref, xs_in_hbm, xs_hbm, sem, *, tokens_per_step):
    del xs_in_hbm

    def row_copy(t, k):
        return pltpu.make_async_copy(x_ref.at[pl.ds(t, 1)],
                                     xs_hbm.at[pl.ds(pos_ref[0, 2 * t + k], 1)], sem)

    def issue(t, carry):
        row_copy(t, 0).start()
        row_copy(t, 1).start()
        return carry

    lax.fori_loop(0, tokens_per_step, issue, 0, unroll=8)

    def drain(t, carry):
        row_copy(t, 0).wait()
        row_copy(t, 1).wait()
        return carry

    lax.fori_loop(0, tokens_per_step, drain, 0, unroll=8)


def dispatch_rows(x, pos, n_rows, tokens_per_step):
    t, d = x.shape
    n_steps = t // tokens_per_step
    pos2 = pos.reshape(n_steps, 1, 2 * tokens_per_step)
    zeros = jnp.zeros((n_rows, d), x.dtype)
    return pl.pallas_call(
        functools.partial(_dispatch_kernel, tokens_per_step=tokens_per_step),
        grid=(n_steps,),
        in_specs=[pl.BlockSpec((None, 1, 2 * tokens_per_step), lambda i: (i, 0, 0),
                               memory_space=pltpu.SMEM),
                  pl.BlockSpec((tokens_per_step, d), lambda i: (i, 0)),
                  pl.BlockSpec(memory_space=pl.ANY)],
        out_specs=pl.BlockSpec(memory_space=pl.ANY),
        out_shape=jax.ShapeDtypeStruct((n_rows, d), x.dtype),
        scratch_shapes=[pltpu.SemaphoreType.DMA(())],
        input_output_aliases={2: 0},
        compiler_params=_cparams(1),
        name="dispatch_rows",
    )(pos2, x, zeros)


def _combine_ln_kernel(pos_ref, x_ref, gate_ref, g_ref, b_ref, ys_hbm, o_ref, buf_ref, sem, *, tm):
    def row_copy(t, k):
        return pltpu.make_async_copy(ys_hbm.at[pl.ds(pos_ref[0, 2 * t + k], 1)],
                                     buf_ref.at[k, pl.ds(t, 1)], sem)

    def issue(t, carry):
        row_copy(t, 0).start()
        row_copy(t, 1).start()
        return carry

    lax.fori_loop(0, tm, issue, 0, unroll=8)

    def drain(t, carry):
        row_copy(t, 0).wait()
        row_copy(t, 1).wait()
        return carry

    lax.fori_loop(0, tm, drain, 0, unroll=8)

    gates = gate_ref[...]
    y = gates[:, 2:3] * buf_ref[0] + gates[:, 3:4] * buf_ref[1]
    o_ref[...] = _layer_norm_rows(DEEPNORM_ALPHA * x_ref[...] + y, g_ref[...], b_ref[...])


def combine_ln(x, ys, pos, meta, ln_g, ln_b, tm):
    t, d = x.shape
    n_steps = t // tm
    pos2 = pos.reshape(n_steps, 1, 2 * tm)
    return pl.pallas_call(
        functools.partial(_combine_ln_kernel, tm=tm),
        grid=(n_steps,),
        in_specs=[pl.BlockSpec((None, 1, 2 * tm), lambda i: (i, 0, 0), memory_space=pltpu.SMEM),
                  pl.BlockSpec((tm, d), lambda i: (i, 0)),
                  pl.BlockSpec((tm, LANES), lambda i: (i, 0)),
                  pl.BlockSpec((1, d), lambda i: (0, 0)),
                  pl.BlockSpec((1, d), lambda i: (0, 0)),
                  pl.BlockSpec(memory_space=pl.ANY)],
        out_specs=pl.BlockSpec((tm, d), lambda i: (i, 0)),
        out_shape=jax.ShapeDtypeStruct((t, d), F32),
        scratch_shapes=[pltpu.VMEM((2, tm, d), F32), pltpu.SemaphoreType.DMA(())],
        compiler_params=_cparams(1),
        name="combine_ln",
    )(pos2, x, meta, ln_g.reshape(1, d), ln_b.reshape(1, d), ys)


def moe_ffn_ln(x, w_router, w_gu_all, w_down_all, layer, ln_g, ln_b, tm_route, tm_group, tm_combine):
    t, d = x.shape
    meta, counts = route_top2(x, w_router, tm_route)
    ids = meta[:, 0:2].astype(jnp.int32)
    ranks = meta[:, 4:6].astype(jnp.int32)
    cnt = counts[0, :N_EXPERTS].astype(jnp.int32)
    tiles_per = (cnt + tm_group - 1) // tm_group
    tile_end = jnp.cumsum(tiles_per)
    group_start = (tile_end - tiles_per) * tm_group
    pos = group_start[ids] + ranks
    n_tiles = min((TOP_K * t + N_EXPERTS * (tm_group - 1)) // tm_group,
                  N_EXPERTS * ((t + tm_group - 1) // tm_group))
    n_active = tile_end[N_EXPERTS - 1:N_EXPERTS]
    tile_ids = jnp.arange(n_tiles, dtype=jnp.int32)
    tile_expert = jnp.sum((tile_ids[:, None] >= tile_end[None, :]).astype(jnp.int32), axis=1)
    tile_expert = jnp.minimum(tile_expert, N_EXPERTS - 1) + layer * N_EXPERTS
    xs = dispatch_rows(x, pos, n_tiles * tm_group, min(t, 1024))
    ys = swiglu_tiles(xs, w_gu_all, w_down_all, tile_expert.astype(jnp.int32),
                      n_active.astype(jnp.int32), ln_g, ln_b, tm_group, False)
    return combine_ln(x, ys, pos, meta, ln_g, ln_b, tm_combine)


def _t5_bucket_runs():
    max_exact = N_BUCKETS // 2
    n = np.arange(MAX_DISTANCE)
    scaled = (np.log(np.maximum(n, max_exact).astype(np.float64) / max_exact)
              / math.log(MAX_DISTANCE / max_exact) * (N_BUCKETS - max_exact))
    frac = scaled - np.floor(scaled)
    assert np.all((np.minimum(frac, 1 - frac) > 1e-3) | (n <= max_exact))
    bucket = np.where(n < max_exact, n, np.minimum(max_exact + scaled.astype(np.int64), N_BUCKETS - 1))
    runs = []
    for dist in range(MAX_DISTANCE):
        if runs and runs[-1][1] == int(bucket[dist]):
            runs[-1] = (dist, int(bucket[dist]))
        else:
            runs.append((dist, int(bucket[dist])))
    return runs, bucket


T5_RUNS, T5_BUCKETS = _t5_bucket_runs()
ATT_TQ = 512


def _bias_tile_kernel(rel_ref, o_ref):
    h = pl.program_id(0)
    tq, tk2 = o_ref.shape
    row = lax.broadcasted_iota(jnp.int32, (tq, tk2), 0)
    col = lax.broadcasted_iota(jnp.int32, (tq, tk2), 1)
    dist = row - col + tq
    far = rel_ref[N_BUCKETS - 1, h]
    acc = jnp.zeros((tq, tk2), F32)
    for last, bucket in reversed(T5_RUNS):
        acc = jnp.where(dist <= last, rel_ref[bucket, h] - far, acc)
    o_ref[...] = jnp.where(dist < 0, -jnp.inf, acc)


def attn_bias_tiles(rel_bias):
    return pl.pallas_call(
        _bias_tile_kernel,
        grid=(ATT_HEADS,),
        in_specs=[pl.BlockSpec(memory_space=pltpu.SMEM)],
        out_specs=pl.BlockSpec((None, ATT_TQ, 2 * ATT_TQ), lambda h: (h, 0, 0)),
        out_shape=jax.ShapeDtypeStruct((ATT_HEADS, ATT_TQ, 2 * ATT_TQ), F32),
        compiler_params=_cparams(1),
        name="attn_bias_tiles",
    )(rel_bias)


def _softmax_step(s, v, m_ref, l_ref, acc_ref):
    m_prev = m_ref[...]
    m_new = jnp.maximum(m_prev, jnp.max(s, axis=-1, keepdims=True))
    alpha = jnp.exp(m_prev - m_new)
    p = jnp.exp(s - m_new[:, 0:1])
    l_ref[...] = alpha * l_ref[...] + jnp.sum(p, axis=-1, keepdims=True)
    acc_ref[...] = alpha * acc_ref[...] + jnp.dot(p.astype(BF16), v, preferred_element_type=F32)
    m_ref[...] = m_new


def _attn_prompt_kernel(lam_ref, q_ref, k_ref, v_ref, bt_ref, g_ref, o_ref,
                        kb_ref, vb_ref, m1_ref, l1_ref, a1_ref, m2_ref, l2_ref, a2_ref, *, out_scale):
    qi = pl.program_id(2)
    tq = q_ref.shape[0]
    half = ATT_HEAD_DIM

    @pl.when(qi == 0)
    def _():
        kb_ref[...] = k_ref[...].astype(BF16)
        vb_ref[...] = v_ref[...].astype(BF16)

    q = q_ref[...] * (ATT_HEAD_DIM ** -0.5)
    lane = lax.broadcasted_iota(jnp.int32, q.shape, 1)
    q1 = jnp.where(lane < half, q, 0.0).astype(BF16)
    q2 = jnp.where(lane >= half, q, 0.0).astype(BF16)

    for m_ref, l_ref, a_ref in ((m1_ref, l1_ref, a1_ref), (m2_ref, l2_ref, a2_ref)):
        m_ref[...] = jnp.full(m_ref.shape, -jnp.inf, F32)
        l_ref[...] = jnp.zeros(l_ref.shape, F32)
        a_ref[...] = jnp.zeros(a_ref.shape, F32)

    nt = (((1,), (1,)), ((), ()))

    def chunk(start, bias):
        k = kb_ref[pl.ds(start, tq), :]
        v = vb_ref[pl.ds(start, tq), :]
        s1 = lax.dot_general(q1, k, nt, preferred_element_type=F32)
        s2 = lax.dot_general(q2, k, nt, preferred_element_type=F32)
        if bias is not None:
            s1 = s1 + bias
            s2 = s2 + bias
        _softmax_step(s1, v, m1_ref, l1_ref, a1_ref)
        _softmax_step(s2, v, m2_ref, l2_ref, a2_ref)

    def far_chunk(c, carry):
        chunk(pl.multiple_of(c * tq, tq), None)
        return carry

    lax.fori_loop(0, qi - 1, far_chunk, 0)

    @pl.when(qi > 0)
    def _():
        chunk(pl.multiple_of((qi - 1) * tq, tq), bt_ref[:, 0:tq])

    chunk(pl.multiple_of(qi * tq, tq), bt_ref[:, tq:2 * tq])

    o = a1_ref[...] / l1_ref[...] - lam_ref[0] * (a2_ref[...] / l2_ref[...])
    o = o * lax.rsqrt(jnp.mean(o * o, axis=-1, keepdims=True) + RMS_EPS) * g_ref[...]
    o_ref[...] = o * out_scale


def attn_prompt(qkv, lam, lam_init, bias_tiles, subln_g):
    bn, l, _ = qkv.shape
    hd = 2 * ATT_HEAD_DIM
    tq = ATT_TQ
    stat = pltpu.VMEM((tq, hd), F32)
    return pl.pallas_call(
        functools.partial(_attn_prompt_kernel, out_scale=1.0 - lam_init),
        grid=(bn, ATT_HEADS, l // tq),
        in_specs=[pl.BlockSpec(memory_space=pltpu.SMEM),
                  pl.BlockSpec((None, tq, hd), lambda b, h, i: (b, i, h)),
                  pl.BlockSpec((None, l, hd), lambda b, h, i: (b, 0, ATT_HEADS + h)),
                  pl.BlockSpec((None, l, hd), lambda b, h, i: (b, 0, 2 * ATT_HEADS + h)),
                  pl.BlockSpec((None, tq, 2 * tq), lambda b, h, i: (h, 0, 0)),
                  pl.BlockSpec((1, hd), lambda b, h, i: (0, 0))],
        out_specs=pl.BlockSpec((None, tq, hd), lambda b, h, i: (b, i, h)),
        out_shape=jax.ShapeDtypeStruct((bn, l, ATT_HEADS * hd), F32),
        scratch_shapes=[pltpu.VMEM((l, hd), BF16), pltpu.VMEM((l, hd), BF16),
                        stat, stat, stat, stat, stat, stat],
        compiler_params=_cparams(3),
        name="attn_prompt",
    )(lam.reshape(1), qkv, qkv, qkv, bias_tiles, subln_g.reshape(1, hd))


ATT_PAGES_PER_STEP = 4
ATT_ROWS = 2 * ATT_HEADS


def _row_select(vec_row, h, lane_lo, lane_hi):
    rid = lax.broadcasted_iota(jnp.int32, (ATT_ROWS, LANES), 0)
    lane = lax.broadcasted_iota(jnp.int32, (ATT_ROWS, LANES), 1)
    full = jnp.broadcast_to(vec_row, (ATT_ROWS, LANES))
    keep = jnp.logical_or(jnp.logical_and(rid == 2 * h, lane < lane_hi),
                          jnp.logical_and(rid == 2 * h + 1, lane >= lane_lo))
    return jnp.where(keep, full, 0.0)


def _attn_decode_kernel(pt_ref, lam_ref, qkv_ref, bl_ref, bc_ref, g_ref, *refs, out_scale):
    npg = ATT_PAGES_PER_STEP
    k_refs, v_refs = refs[:npg], refs[npg:2 * npg]
    o_ref, qs_ref, m_ref, l_ref, acc_ref = refs[2 * npg:]
    del pt_ref
    step = pl.program_id(1)
    last = pl.num_programs(1) - 1
    rid = lax.broadcasted_iota(jnp.int32, (ATT_ROWS, LANES), 0)
    half = ATT_HEAD_DIM
    hd = 2 * ATT_HEAD_DIM

    @pl.when(step == 0)
    def _():
        m_ref[...] = jnp.full(m_ref.shape, -jnp.inf, F32)
        l_ref[...] = jnp.zeros(l_ref.shape, F32)
        acc_ref[...] = jnp.zeros(acc_ref.shape, F32)
        for h in range(ATT_HEADS):
            qh = qkv_ref[:, h * hd:(h + 1) * hd] * (ATT_HEAD_DIM ** -0.5)
            qs_ref[h] = _row_select(qh, h, half, half).astype(BF16)

    def head_rows(page_refs, h):
        return jnp.concatenate(
            [r[pl.ds(h, PAGE_SIZE, stride=ATT_HEADS), :].astype(BF16) for r in page_refs], axis=0)

    s = None
    for h in range(ATT_HEADS):
        t = lax.dot_general(qs_ref[h], head_rows(k_refs, h), _NT, preferred_element_type=F32)
        s = t if s is None else s + t
    tail_bias = jnp.where(step == last, 1.0, 0.0) * bl_ref[...]
    s = jnp.concatenate([s[:, :(npg - 1) * PAGE_SIZE], s[:, (npg - 1) * PAGE_SIZE:] + tail_bias], axis=1)

    m_prev = m_ref[...]
    m_new = jnp.maximum(m_prev, jnp.max(s, axis=-1, keepdims=True))
    alpha = jnp.exp(m_prev - m_new)
    p = jnp.exp(s - m_new[:, 0:1])
    l_ref[...] = alpha * l_ref[...] + jnp.sum(p, axis=-1, keepdims=True)
    pb = p.astype(BF16)
    pv = None
    for h in range(ATT_HEADS):
        t = jnp.dot(pb, head_rows(v_refs, h), preferred_element_type=F32)
        t = jnp.where(jnp.logical_or(rid == 2 * h, rid == 2 * h + 1), t, 0.0)
        pv = t if pv is None else pv + t
    acc_ref[...] = alpha * acc_ref[...] + pv
    m_ref[...] = m_new

    @pl.when(step == last)
    def _():
        s_cur = None
        v_rows = None
        for h in range(ATT_HEADS):
            kn = qkv_ref[:, D_MODEL + h * hd:D_MODEL + (h + 1) * hd].astype(BF16).astype(F32)
            vn = qkv_ref[:, 2 * D_MODEL + h * hd:2 * D_MODEL + (h + 1) * hd].astype(BF16).astype(F32)
            t = jnp.sum(qs_ref[h].astype(F32) * kn, axis=-1, keepdims=True)
            s_cur = t if s_cur is None else s_cur + t
            vr = _row_select(vn, h, 0, LANES)
            v_rows = vr if v_rows is None else v_rows + vr
        s_cur = s_cur + bc_ref[...][:, 0:1]
        m_prev = m_ref[...]
        m_new = jnp.maximum(m_prev, s_cur)
        alpha = jnp.exp(m_prev - m_new)
        p = jnp.exp(s_cur - m_new[:, 0:1])
        l_fin = alpha * l_ref[...] + p
        acc = alpha * acc_ref[...] + p.astype(BF16).astype(F32) * v_rows
        o = acc / l_fin
        er = lax.broadcasted_iota(jnp.int32, (ATT_HEADS, ATT_ROWS), 0)
        ec = lax.broadcasted_iota(jnp.int32, (ATT_HEADS, ATT_ROWS), 1)
        pick1 = jnp.where(ec == 2 * er, 1.0, 0.0).astype(BF16)
        pick2 = jnp.where(ec == 2 * er + 1, 1.0, 0.0).astype(BF16)
        hi, mid, lo = _split3(o)
        o1 = (jnp.dot(pick1, hi, preferred_element_type=F32) + jnp.dot(pick1, mid, preferred_element_type=F32)
              + jnp.dot(pick1, lo, preferred_element_type=F32))
        o2 = (jnp.dot(pick2, hi, preferred_element_type=F32) + jnp.dot(pick2, mid, preferred_element_type=F32)
              + jnp.dot(pick2, lo, preferred_element_type=F32))
        d = o1 - lam_ref[0] * o2
        d = d * lax.rsqrt(jnp.mean(d * d, axis=-1, keepdims=True) + RMS_EPS) * g_ref[...]
        o_ref[...] = d * out_scale


def attn_decode(qkv, k_pages, v_pages, layer, page_table, lam, lam_init, rel_bias, subln_g):
    bn = qkv.shape[0]
    n_pages = page_table.shape[1]
    npg = ATT_PAGES_PER_STEP
    hd = 2 * ATT_HEAD_DIM
    dist = PAGE_SIZE - np.arange(PAGE_SIZE)
    bucket = np.where(dist < MAX_DISTANCE, T5_BUCKETS[np.minimum(dist, MAX_DISTANCE - 1)], N_BUCKETS - 1)
    far = rel_bias[N_BUCKETS - 1]
    bias_last = jnp.repeat((rel_bias[bucket] - far).T, 2, axis=0)
    bias_cur = jnp.broadcast_to(jnp.repeat(rel_bias[0] - far, 2)[:, None], (ATT_ROWS, LANES))

    def rows_of(pages):
        return pages.reshape(pages.shape[0], pages.shape[1], PAGE_SIZE * ATT_HEADS, hd)

    def page_spec(pg):
        return pl.BlockSpec((None, None, PAGE_SIZE * ATT_HEADS, hd),
                            lambda b, s, pt: (layer, pt[b, s * npg + pg], 0, 0))

    full = lambda shape: pl.BlockSpec(shape, lambda b, s, pt: (0,) * len(shape))
    stat = pltpu.VMEM((ATT_ROWS, LANES), F32)
    grid_spec = pltpu.PrefetchScalarGridSpec(
        num_scalar_prefetch=1,
        grid=(bn, n_pages // npg),
        in_specs=[pl.BlockSpec(memory_space=pltpu.SMEM),
                  pl.BlockSpec((None, 1, 3 * D_MODEL), lambda b, s, pt: (b, 0, 0)),
                  full((ATT_ROWS, PAGE_SIZE)), full((ATT_ROWS, LANES)), full((1, hd))]
                 + [page_spec(pg) for pg in range(npg)] * 2,
        out_specs=pl.BlockSpec((None, ATT_HEADS, hd), lambda b, s, pt: (b, 0, 0)),
        scratch_shapes=[pltpu.VMEM((ATT_HEADS, ATT_ROWS, LANES), BF16), stat, stat, stat],
    )
    out = pl.pallas_call(
        functools.partial(_attn_decode_kernel, out_scale=1.0 - lam_init),
        grid_spec=grid_spec,
        out_shape=jax.ShapeDtypeStruct((bn, ATT_HEADS, hd), F32),
        compiler_params=_cparams(2),
        name="attn_decode",
    )(page_table, lam.reshape(1), qkv.reshape(bn, 1, 3 * D_MODEL), bias_last, bias_cur,
      subln_g.reshape(1, hd), *([rows_of(k_pages)] * npg), *([rows_of(v_pages)] * npg))
    return out.reshape(bn, D_MODEL)


def _split3(v):
    hi = v.astype(BF16)
    r1 = v - hi.astype(F32)
    mid = r1.astype(BF16)
    lo = (r1 - mid.astype(F32)).astype(BF16)
    return hi, mid, lo


def _dot3(v, w_bf16, dims=None):
    out = None
    for piece in _split3(v):
        if dims is None:
            t = jnp.dot(piece, w_bf16, preferred_element_type=F32)
        else:
            t = lax.dot_general(piece, w_bf16, dims, preferred_element_type=F32)
        out = t if out is None else out + t
    return out


def _softplus(x):
    return jnp.maximum(x, 0.0) + jnp.log1p(jnp.exp(-jnp.abs(x)))


_NT = (((1,), (1,)), ((), ()))
_TN = (((0,), (0,)), ((), ()))
SSM_GROUP_W = SSM_HPG * SSM_HEAD_DIM
SSM_GN = SSM_GROUPS * SSM_STATE


def _ssd_prompt_kernel(proj_ref, dt_ref, cw_ref, cb_ref, dtb_ref, aneg_ref, dskip_ref, ng_ref, e_ref,
                       y_ref, h_ref, xp_ref, ht_ref):
    ci = pl.program_id(1)
    c = SSM_CHUNK
    di = SSM_D_INNER

    @pl.when(ci == 0)
    def _():
        xp_ref[0:8, :] = jnp.zeros((8, SSM_CONV_DIM), F32)
        ht_ref[...] = jnp.zeros(ht_ref.shape, F32)

    xbc = proj_ref[:, di:di + SSM_CONV_DIM]
    xp_ref[8:8 + c, :] = xbc
    conv = cb_ref[...] + xp_ref[5:5 + c, :] * cw_ref[0:1, :]
    for i in range(1, CONV_W):
        conv = conv + xp_ref[5 + i:5 + i + c, :] * cw_ref[i:i + 1, :]
    xp_ref[0:8, :] = xbc[c - 8:c, :]
    act = _silu(conv)
    xs = act[:, :di]

    dt = _softplus(dt_ref[...] + dtb_ref[...])
    a = dt * aneg_ref[...]
    row = lax.broadcasted_iota(jnp.int32, (c, c), 0)
    col = lax.broadcasted_iota(jnp.int32, (c, c), 1)
    incl = col <= row
    tri = jnp.where(incl, 1.0, 0.0).astype(BF16)
    hi, mid, lo = _split3(a)
    acum = (jnp.dot(tri, hi, preferred_element_type=F32) + jnp.dot(tri, mid, preferred_element_type=F32)
            + jnp.dot(tri, lo, preferred_element_type=F32))
    upper = jnp.where(row <= col, 1.0, 0.0).astype(BF16)
    acum_t = _dot3(a, upper, _TN)
    a_end = acum[c - 1:c, :]
    eac = jnp.exp(acum)
    wend = jnp.exp(a_end - acum) * dt
    ex = _dot3(jnp.concatenate([dt, eac, wend], axis=0), e_ref[...])
    dt_x, eac_x, wend_x = ex[0:c], ex[c:2 * c], ex[2 * c:3 * c]
    xdt = (xs * dt_x).astype(BF16)
    xw = (xs * wend_x).astype(BF16)
    lane = lax.broadcasted_iota(jnp.int32, (c, LANES), 1)
    lo_half = lane < SSM_HEAD_DIM
    neg_inf = jnp.float32(-jnp.inf)

    for g in range(SSM_GROUPS):
        gs = slice(g * SSM_GROUP_W, (g + 1) * SSM_GROUP_W)
        bg = act[:, di + g * SSM_STATE:di + (g + 1) * SSM_STATE].astype(BF16)
        cg = act[:, di + SSM_GN + g * SSM_STATE:di + SSM_GN + (g + 1) * SSM_STATE].astype(BF16)
        cb = lax.dot_general(cg, bg, _NT, preferred_element_type=F32)
        pieces = []
        for pr in range(SSM_GROUP_W // LANES):
            pair = g * (SSM_GROUP_W // LANES) + pr
            xd = xdt[:, pair * LANES:(pair + 1) * LANES]
            acc = None
            for half in range(2):
                h = 2 * pair + half
                diff = acum[:, h:h + 1] - acum_t[h:h + 1, :]
                s = (cb * jnp.exp(jnp.where(incl, diff, neg_inf))).astype(BF16)
                xm = jnp.where(lo_half if half == 0 else jnp.logical_not(lo_half), xd, jnp.zeros_like(xd))
                t = jnp.dot(s, xm, preferred_element_type=F32)
                acc = t if acc is None else acc + t
            pieces.append(acc)
        y_intra = jnp.concatenate(pieces, axis=1)
        h_old = ht_ref[:, gs]
        y_inter = jnp.dot(cg, h_old.astype(BF16), preferred_element_type=F32) * eac_x[:, gs]
        dstate = lax.dot_general(bg, xw[:, gs], _TN, preferred_element_type=F32)
        ht_ref[:, gs] = h_old * eac_x[c - 1:c, gs] + dstate
        y = y_intra + y_inter + dskip_ref[:, gs] * xs[:, gs]
        y = y * _silu(proj_ref[:, gs])
        y = y * lax.rsqrt(jnp.mean(y * y, axis=-1, keepdims=True) + RMS_EPS) * ng_ref[:, gs]
        y_ref[:, gs] = y

    @pl.when(ci == pl.num_programs(1) - 1)
    def _():
        h_ref[...] = ht_ref[...].T


def _head_expander():
    e = np.zeros((LANES, SSM_D_INNER), np.float32)
    for h in range(SSM_HEADS):
        e[h, h * SSM_HEAD_DIM:(h + 1) * SSM_HEAD_DIM] = 1.0
    return jnp.asarray(e, BF16)


def _pad_lanes(v):
    return jnp.pad(v.astype(F32), (0, LANES - v.shape[0])).reshape(1, LANES)


def ssd_prompt(proj, dtp, conv_w, conv_b, a_log, dt_bias, d_skip, norm_g):
    bn, l, pw = proj.shape
    c = SSM_CHUNK
    di = SSM_D_INNER
    full = lambda shape: pl.BlockSpec(shape, lambda b, i: (0,) * len(shape))
    y, h = pl.pallas_call(
        _ssd_prompt_kernel,
        grid=(bn, l // c),
        in_specs=[pl.BlockSpec((None, c, pw), lambda b, i: (b, i, 0)),
                  pl.BlockSpec((None, c, LANES), lambda b, i: (b, i, 0)),
                  full((CONV_W, SSM_CONV_DIM)), full((1, SSM_CONV_DIM)),
                  full((1, LANES)), full((1, LANES)), full((1, di)), full((1, di)),
                  full((LANES, di))],
        out_specs=[pl.BlockSpec((None, c, di), lambda b, i: (b, i, 0)),
                   pl.BlockSpec((None, di, SSM_STATE), lambda b, i: (b, 0, 0))],
        out_shape=[jax.ShapeDtypeStruct((bn, l, di), F32),
                   jax.ShapeDtypeStruct((bn, di, SSM_STATE), F32)],
        scratch_shapes=[pltpu.VMEM((8 + c, SSM_CONV_DIM), F32), pltpu.VMEM((SSM_STATE, di), F32)],
        compiler_params=_cparams(2),
        name="ssd_prompt",
    )(proj, dtp, conv_w, conv_b.reshape(1, -1), _pad_lanes(dt_bias), _pad_lanes(-jnp.exp(a_log)),
      jnp.repeat(d_skip, SSM_HEAD_DIM).reshape(1, di), norm_g.reshape(1, di), _head_expander())
    return y, h.reshape(bn, SSM_HEADS, SSM_HEAD_DIM, SSM_STATE)


GDN_BLK = 512


def _split2(v):
    hi = v.astype(BF16)
    return hi, (v - hi.astype(F32)).astype(BF16)


def _mm_split(a, b):
    (a_hi, a_lo), (b_hi, b_lo) = a, b
    return (jnp.dot(a_hi, b_hi, preferred_element_type=F32) + jnp.dot(a_hi, b_lo, preferred_element_type=F32)
            + jnp.dot(a_lo, b_hi, preferred_element_type=F32))


def _conv_silu_block(x_ref, xp_ref, cw_ref):
    n = x_ref.shape[0]
    x = x_ref[...]
    xp_ref[8:8 + n, :] = x
    y = xp_ref[5:5 + n, :] * cw_ref[0:1, :]
    for i in range(1, CONV_W):
        y = y + xp_ref[5 + i:5 + i + n, :] * cw_ref[i:i + 1, :]
    xp_ref[0:8, :] = x[n - 8:n, :]
    return _silu(y)


def _gdn_prompt_kernel(alog_ref, dtb_ref, q_ref, k_ref, v_ref, z_ref, ba_ref, cwq_ref, cwk_ref, cwv_ref,
                       ng_ref, tri_ref, upper_ref, o_ref, s_out_ref, xq_ref, xk_ref, xv_ref, s_ref):
    h = pl.program_id(1)
    bi = pl.program_id(2)
    blk = q_ref.shape[0]
    c = GDN_CHUNK

    @pl.when(bi == 0)
    def _():
        for r in (xq_ref, xk_ref, xv_ref):
            r[0:8, :] = jnp.zeros((8, LANES), F32)
        s_ref[...] = jnp.zeros(s_ref.shape, F32)

    q = _conv_silu_block(q_ref, xq_ref, cwq_ref)
    k = _conv_silu_block(k_ref, xk_ref, cwk_ref)
    v = _conv_silu_block(v_ref, xv_ref, cwv_ref)
    q = q * lax.rsqrt(jnp.sum(q * q, axis=-1, keepdims=True) + 1e-6) * (GDN_DK ** -0.5)
    k = k * lax.rsqrt(jnp.sum(k * k, axis=-1, keepdims=True) + 1e-6)
    lane = lax.broadcasted_iota(jnp.int32, (blk, LANES), 1)
    ba = ba_ref[...]
    b_col = jnp.sum(jnp.where(lane == h, ba, 0.0), axis=-1, keepdims=True)
    a_col = jnp.sum(jnp.where(lane == GDN_HEADS + h, ba, 0.0), axis=-1, keepdims=True)
    beta = 1.0 / (1.0 + jnp.exp(-b_col))
    g = -jnp.exp(alog_ref[h]) * _softplus(a_col + dtb_ref[h])

    row = lax.broadcasted_iota(jnp.int32, (c, c), 0)
    col = lax.broadcasted_iota(jnp.int32, (c, c), 1)
    incl = col <= row
    strict = col < row
    eye = jnp.where(row == col, 1.0, 0.0)
    neg_inf = jnp.float32(-jnp.inf)
    nch = blk // c
    rows = [slice(n * c, (n + 1) * c) for n in range(nch)]

    g_rep = jnp.broadcast_to(g, (blk, LANES))
    hi, mid, lo = _split3(g_rep)
    tri_bd = tri_ref[...]
    gcol = (jnp.dot(tri_bd, hi, preferred_element_type=F32) + jnp.dot(tri_bd, mid, preferred_element_type=F32)
            + jnp.dot(tri_bd, lo, preferred_element_type=F32))
    grow = _dot3(g_rep[:, :c], upper_ref[...], _TN)
    gcum = gcol[:, 0:1]
    eg = jnp.exp(gcum)

    decays = [jnp.exp(jnp.where(incl, gcol[rows[n], 0:c] - grow[:, n * LANES:n * LANES + c], neg_inf))
              for n in range(nch)]
    kbs = [k[r].astype(BF16) for r in rows]
    kks = [lax.dot_general(kb, kb, _NT, preferred_element_type=F32) for kb in kbs]
    qks = [lax.dot_general(q[rows[n]].astype(BF16), kbs[n], _NT, preferred_element_type=F32) * decays[n]
           for n in range(nch)]
    mps = [jnp.where(strict, beta[rows[n]] * kks[n] * decays[n], 0.0) for n in range(nch)]
    ps = [eye - m for m in mps]
    for _ in range(5):
        mps = [_mm_split(_split2(mp), _split2(mp)) for mp in mps]
        mp_parts = [_split2(mp) for mp in mps]
        ps = [p + _mm_split(_split2(p), mpp) for p, mpp in zip(ps, mp_parts)]
    rhs = [jnp.concatenate([beta[r] * v[r], (beta[r] * eg[r]) * k[r]], axis=1) for r in rows]
    ws = [_mm_split(_split2(p), _split2(x)) for p, x in zip(ps, rhs)]
    uvs = [w[:, :GDN_DV].astype(BF16) for w in ws]
    uks = [w[:, GDN_DV:].astype(BF16) for w in ws]
    g_last = [gcol[n * c + c - 1:n * c + c, 0:1] for n in range(nch)]
    kes = [(k[rows[n]] * jnp.exp(g_last[n] - gcum[rows[n]])).astype(BF16) for n in range(nch)]
    qkb = [x.astype(BF16) for x in qks]
    kmats = [lax.dot_general(kes[n], uks[n], _TN, preferred_element_type=F32).astype(BF16) for n in range(nch)]
    hmats = [lax.dot_general(kes[n], uvs[n], _TN, preferred_element_type=F32) for n in range(nch)]
    qps = [(q[rows[n]] * eg[rows[n]] - jnp.dot(qkb[n], uks[n], preferred_element_type=F32)).astype(BF16)
           for n in range(nch)]
    ops = [jnp.dot(qkb[n], uvs[n], preferred_element_type=F32) for n in range(nch)]

    s = s_ref[...]
    for n in range(nch):
        sb = s.astype(BF16)
        o = jnp.dot(qps[n], sb, preferred_element_type=F32) + ops[n]
        s = jnp.exp(g_last[n]) * s - jnp.dot(kmats[n], sb, preferred_element_type=F32) + hmats[n]
        o = o * lax.rsqrt(jnp.mean(o * o, axis=-1, keepdims=True) + RMS_EPS) * ng_ref[...]
        o_ref[rows[n], :] = o * _silu(z_ref[rows[n], :])
    s_ref[...] = s

    @pl.when(bi == pl.num_programs(2) - 1)
    def _():
        s_out_ref[...] = s


def gdn_prompt(proj, ba, conv_w, a_log, dt_bias, norm_g):
    bn, l, _ = proj.shape
    blk = GDN_BLK
    nh = GDN_HEADS
    hd = GDN_DK
    col = lambda off: pl.BlockSpec((None, blk, hd), lambda b, h, i: (b, i, off + h))
    cw = lambda off: pl.BlockSpec((CONV_W, hd), lambda b, h, i: (0, off + h))
    tail = pltpu.VMEM((8 + blk, hd), F32)
    nch = blk // GDN_CHUNK
    t = np.arange(blk)
    same = (t[:, None] // GDN_CHUNK) == (t[None, :] // GDN_CHUNK)
    tri_bd = jnp.asarray(same & (t[None, :] <= t[:, None]), BF16)
    jj = np.arange(nch * LANES)
    upper = ((t[:, None] // GDN_CHUNK) == (jj[None, :] // LANES)) & \
            ((t[:, None] % GDN_CHUNK) <= (jj[None, :] % LANES)) & ((jj[None, :] % LANES) < GDN_CHUNK)
    upper_bd = jnp.asarray(upper, BF16)
    o, s = pl.pallas_call(
        _gdn_prompt_kernel,
        grid=(bn, nh, l // blk),
        in_specs=[pl.BlockSpec(memory_space=pltpu.SMEM), pl.BlockSpec(memory_space=pltpu.SMEM),
                  col(0), col(nh), col(2 * nh), col(3 * nh),
                  pl.BlockSpec((None, blk, LANES), lambda b, h, i: (b, i, 0)),
                  cw(0), cw(nh), cw(2 * nh),
                  pl.BlockSpec((1, hd), lambda b, h, i: (0, 0)),
                  pl.BlockSpec((blk, blk), lambda b, h, i: (0, 0)),
                  pl.BlockSpec((blk, nch * LANES), lambda b, h, i: (0, 0))],
        out_specs=[pl.BlockSpec((None, blk, hd), lambda b, h, i: (b, i, h)),
                   pl.BlockSpec((None, None, hd, hd), lambda b, h, i: (b, h, 0, 0))],
        out_shape=[jax.ShapeDtypeStruct((bn, l, nh * hd), F32),
                   jax.ShapeDtypeStruct((bn, nh, hd, hd), F32)],
        scratch_shapes=[tail, tail, tail, pltpu.VMEM((hd, hd), F32)],
        compiler_params=_cparams(3),
        name="gdn_prompt",
    )(a_log.astype(F32), dt_bias.astype(F32), proj, proj, proj, proj, ba, conv_w, conv_w, conv_w,
      norm_g.reshape(1, hd), tri_bd, upper_bd)
    return o, s


SUB = 16


def _rows8(row, zero_fill, dtype=F32):
    full = jnp.broadcast_to(row.astype(F32), (SUB, row.shape[1]))
    if zero_fill:
        rid = lax.broadcasted_iota(jnp.int32, full.shape, 0)
        full = jnp.where(rid == 0, full, 0.0)
    return full.astype(dtype)


def _conv_step(x_row, c_ref, cw_ref, cb_row):
    y = x_row * cw_ref[CONV_W - 1:CONV_W, :]
    for i in range(CONV_W - 1):
        y = y + c_ref[i:i + 1, :] * cw_ref[i:i + 1, :]
    if cb_row is not None:
        y = y + cb_row
    return _silu(y)


def _gdn_step_kernel(alog_ref, dtb_ref, proj_ref, ba_ref, c_ref, cw_ref, ng_ref, s_ref, o_ref, s_out_ref):
    hd = GDN_DK
    act = _conv_step(proj_ref[:, :GDN_CONV_DIM], c_ref, cw_ref, None)
    ba = ba_ref[...]
    qs, kbs, us, egs, qks = [], [], [], [], []
    for h in range(GDN_HEADS):
        q = act[:, h * hd:(h + 1) * hd]
        k = act[:, GDN_KD + h * hd:GDN_KD + (h + 1) * hd]
        q = q * lax.rsqrt(jnp.sum(q * q, axis=-1, keepdims=True) + 1e-6) * (GDN_DK ** -0.5)
        k = k * lax.rsqrt(jnp.sum(k * k, axis=-1, keepdims=True) + 1e-6)
        qs.append(q.astype(BF16))
        kbs.append(k.astype(BF16))
        beta = 1.0 / (1.0 + jnp.exp(-ba[:, h:h + 1]))
        g = -jnp.exp(alog_ref[h]) * _softplus(ba[:, GDN_HEADS + h:GDN_HEADS + h + 1] + dtb_ref[h])
        egs.append(jnp.exp(g))
        us.append(beta)
        qks.append(jnp.sum(qs[h].astype(F32) * kbs[h].astype(F32), axis=-1, keepdims=True))
    sbs = [s_ref[h].astype(BF16) for h in range(GDN_HEADS)]
    ks_ = [jnp.dot(_rows8(kbs[h], False, BF16), sbs[h], preferred_element_type=F32)[0:1]
           for h in range(GDN_HEADS)]
    qs_ = [jnp.dot(_rows8(qs[h], False, BF16), sbs[h], preferred_element_type=F32)[0:1]
           for h in range(GDN_HEADS)]
    for h in range(GDN_HEADS):
        v = act[:, 2 * GDN_KD + h * hd:2 * GDN_KD + (h + 1) * hd]
        u = us[h] * (v - egs[h] * ks_[h])
        o = egs[h] * qs_[h] + qks[h] * u
        outer = lax.dot_general(_rows8(kbs[h], True, BF16), _rows8(u, False, BF16), _TN,
                                preferred_element_type=F32)
        s_out_ref[h] = egs[h] * s_ref[h] + outer
        o = o * lax.rsqrt(jnp.mean(o * o, axis=-1, keepdims=True) + RMS_EPS) * ng_ref[...]
        z = proj_ref[:, GDN_CONV_DIM + h * hd:GDN_CONV_DIM + (h + 1) * hd]
        o_ref[:, h * hd:(h + 1) * hd] = o * _silu(z)


def gdn_step(proj, ba, s0, conv0, conv_w, a_log, dt_bias, norm_g):
    bn, pw = proj.shape
    hd = GDN_DK
    o, s = pl.pallas_call(
        _gdn_step_kernel,
        grid=(bn,),
        in_specs=[pl.BlockSpec(memory_space=pltpu.SMEM), pl.BlockSpec(memory_space=pltpu.SMEM),
                  pl.BlockSpec((None, 1, pw), lambda b: (b, 0, 0)),
                  pl.BlockSpec((None, 1, LANES), lambda b: (b, 0, 0)),
                  pl.BlockSpec((None, CONV_W - 1, GDN_CONV_DIM), lambda b: (b, 0, 0)),
                  pl.BlockSpec((CONV_W, GDN_CONV_DIM), lambda b: (0, 0)),
                  pl.BlockSpec((1, hd), lambda b: (0, 0)),
                  pl.BlockSpec((None, GDN_HEADS, hd, hd), lambda b: (b, 0, 0, 0))],
        out_specs=[pl.BlockSpec((None, 1, GDN_VD), lambda b: (b, 0, 0)),
                   pl.BlockSpec((None, GDN_HEADS, hd, hd), lambda b: (b, 0, 0, 0))],
        out_shape=[jax.ShapeDtypeStruct((bn, 1, GDN_VD), F32),
                   jax.ShapeDtypeStruct(s0.shape, F32)],
        compiler_params=_cparams(1),
        name="gdn_step",
    )(a_log.astype(F32), dt_bias.astype(F32), proj.reshape(bn, 1, pw), ba.reshape(bn, 1, LANES),
      conv0, conv_w, norm_g.reshape(1, hd), s0)
    return o.reshape(bn, GDN_VD), s


def _ssd_step_kernel(proj_ref, dt_ref, c_ref, cw_ref, cb_ref, dtb_ref, aneg_ref, dskip_ref, ng_ref, e_ref,
                     h_ref, y_ref, h_out_ref):
    di = SSM_D_INNER
    act = _conv_step(proj_ref[:, di:di + SSM_CONV_DIM], c_ref, cw_ref, cb_ref[...])
    xs = act[:, :di]
    dt = _softplus(dt_ref[...] + dtb_ref[...])
    ea = jnp.exp(dt * aneg_ref[...])
    ex = _dot3(jnp.concatenate([_rows8(dt, True), _rows8(ea, True)], axis=0), e_ref[...])
    dt_x, ea_x = ex[0:1], ex[SUB:SUB + 1]
    xdt = xs * dt_x
    ones = jnp.ones((SUB, SSM_STATE), BF16)
    for g in range(SSM_GROUPS):
        gs = slice(g * SSM_GROUP_W, (g + 1) * SSM_GROUP_W)
        bg = act[:, di + g * SSM_STATE:di + (g + 1) * SSM_STATE].astype(BF16)
        cg = act[:, di + SSM_GN + g * SSM_STATE:di + SSM_GN + (g + 1) * SSM_STATE].astype(BF16)
        dstate = lax.dot_general(_rows8(xdt[:, gs], True, BF16), _rows8(bg, False, BF16), _TN,
                                 preferred_element_type=F32)
        decay = _dot3(_rows8(ea_x[:, gs], True), ones, _TN)
        h_old = h_ref[g * SSM_HPG:(g + 1) * SSM_HPG].reshape(SSM_GROUP_W, SSM_STATE)
        h_new = decay * h_old + dstate
        h_out_ref[g * SSM_HPG:(g + 1) * SSM_HPG] = h_new.reshape(SSM_HPG, SSM_HEAD_DIM, SSM_STATE)
        y = lax.dot_general(_rows8(cg, False, BF16), h_new.astype(BF16), _NT, preferred_element_type=F32)[0:1]
        y = y + dskip_ref[:, gs] * xs[:, gs]
        y = y * _silu(proj_ref[:, gs])
        y = y * lax.rsqrt(jnp.mean(y * y, axis=-1, keepdims=True) + RMS_EPS) * ng_ref[:, gs]
        y_ref[:, gs] = y


def ssd_step(proj, dtp, h0, conv0, conv_w, conv_b, a_log, dt_bias, d_skip, norm_g):
    bn, pw = proj.shape
    di = SSM_D_INNER
    full = lambda shape: pl.BlockSpec(shape, lambda b: (0,) * len(shape))
    hblk = pl.BlockSpec((None, SSM_HEADS, SSM_HEAD_DIM, SSM_STATE), lambda b: (b, 0, 0, 0))
    y, h = pl.pallas_call(
        _ssd_step_kernel,
        grid=(bn,),
        in_specs=[pl.BlockSpec((None, 1, pw), lambda b: (b, 0, 0)),
                  pl.BlockSpec((None, 1, LANES), lambda b: (b, 0, 0)),
                  pl.BlockSpec((None, CONV_W - 1, SSM_CONV_DIM), lambda b: (b, 0, 0)),
                  full((CONV_W, SSM_CONV_DIM)), full((1, SSM_CONV_DIM)),
                  full((1, LANES)), full((1, LANES)), full((1, di)), full((1, di)), full((LANES, di)),
                  hblk],
        out_specs=[pl.BlockSpec((None, 1, di), lambda b: (b, 0, 0)), hblk],
        out_shape=[jax.ShapeDtypeStruct((bn, 1, di), F32), jax.ShapeDtypeStruct(h0.shape, F32)],
        compiler_params=_cparams(1),
        name="ssd_step",
    )(proj.reshape(bn, 1, pw), dtp.reshape(bn, 1, LANES), conv0, conv_w, conv_b.reshape(1, -1),
      _pad_lanes(dt_bias), _pad_lanes(-jnp.exp(a_log)), jnp.repeat(d_skip, SSM_HEAD_DIM).reshape(1, di),
      norm_g.reshape(1, di), _head_expander(), h0)
    return y.reshape(bn, di), h


def _pad_cols(w):
    return jnp.pad(w, ((0, 0), (0, LANES - w.shape[1])))


def kernel(x_prompt, x_sample, state_gdn_S, state_gdn_conv, state_ssm_h, state_ssm_conv,
           cache_attn_k, cache_attn_v, page_table,
           gdn_w_in, gdn_conv_w, gdn_a_log, gdn_dt_bias, gdn_norm_g, gdn_w_out,
           ssm_w_in, ssm_conv_w, ssm_conv_b, ssm_a_log, ssm_dt_bias, ssm_d_skip, ssm_norm_g, ssm_w_out,
           att_w_qkv, att_lam_q1, att_lam_k1, att_lam_q2, att_lam_k2, att_subln_g, att_w_out, rel_bias,
           ffn_w_gu, ffn_w_down, moe_w_router, moe_w_gu, moe_w_down, ln_g, ln_b):
    bp, lp, d = x_prompt.shape
    bs, ls, _ = x_sample.shape
    assert ls == 1, "the sample group advances one token per sequence"
    tp = bp * lp
    ts = bs * ls
    xp = x_prompt.reshape(tp, d)
    xs = x_sample.reshape(ts, d)
    moe_gu = moe_w_gu.reshape((-1,) + moe_w_gu.shape[2:])
    moe_down = moe_w_down.reshape((-1,) + moe_w_down.shape[2:])

    tm_p = 1024
    tm_s = ts

    outs = {name: [] for name in ("gS_p", "gS_s", "gc_p", "gc_s", "sh_p", "sh_s", "sc_p", "sc_s",
                                  "ak_p", "ak_s", "av_p", "av_s")}
    for i in range(DEPTH):
        j = i // N_MIXERS
        if i % N_MIXERS == 0:
            n_main = GDN_CONV_DIM + GDN_VD
            w_ba = _pad_cols(gdn_w_in[j][:, n_main:])
            prm = (gdn_conv_w[j], gdn_a_log[j], gdn_dt_bias[j], gdn_norm_g[j])
            proj_p = matmul_cols(xp, gdn_w_in[j], n_main, tm_p, 512).reshape(bp, lp, n_main)
            ba_p = matmul_narrow_f32(xp, w_ba, tm_p).reshape(bp, lp, LANES)
            op, s_p = gdn_prompt(proj_p, ba_p, *prm)
            c_p = proj_p[:, lp - (CONV_W - 1):, :GDN_CONV_DIM]
            proj_s = matmul_cols(xs, gdn_w_in[j], n_main, tm_s, 512)
            ba_s = matmul_narrow_f32(xs, w_ba, tm_s)
            os_, s_s = gdn_step(proj_s, ba_s, state_gdn_S[j], state_gdn_conv[j], *prm)
            c_s = jnp.concatenate([state_gdn_conv[j][:, 1:], proj_s[:, None, :GDN_CONV_DIM]], axis=1)
            outs["gS_p"].append(s_p); outs["gS_s"].append(s_s)
            outs["gc_p"].append(c_p); outs["gc_s"].append(c_s)
            w_out = gdn_w_out[j]
        elif i % N_MIXERS == 1:
            n_main = SSM_D_INNER + SSM_CONV_DIM
            w_dt = _pad_cols(ssm_w_in[j][:, n_main:])
            prm = (ssm_conv_w[j], ssm_conv_b[j], ssm_a_log[j], ssm_dt_bias[j], ssm_d_skip[j], ssm_norm_g[j])
            proj_p = matmul_cols(xp, ssm_w_in[j], n_main, tm_p, 512).reshape(bp, lp, n_main)
            dt_p = matmul_narrow_f32(xp, w_dt, tm_p).reshape(bp, lp, LANES)
            op, h_p = ssd_prompt(proj_p, dt_p, *prm)
            c_p = proj_p[:, lp - (CONV_W - 1):, SSM_D_INNER:]
            proj_s = matmul_cols(xs, ssm_w_in[j], n_main, tm_s, 512)
            dt_s = matmul_narrow_f32(xs, w_dt, tm_s)
            os_, h_s = ssd_step(proj_s, dt_s, state_ssm_h[j], state_ssm_conv[j], *prm)
            c_s = jnp.concatenate([state_ssm_conv[j][:, 1:], proj_s[:, None, SSM_D_INNER:]], axis=1)
            outs["sh_p"].append(h_p); outs["sh_s"].append(h_s)
            outs["sc_p"].append(c_p); outs["sc_s"].append(c_s)
            w_out = ssm_w_out[j]
        else:
            lam_init = 0.8 - 0.6 * math.exp(-0.3 * i)
            lam = (jnp.exp(jnp.sum(att_lam_q1[j] * att_lam_k1[j]))
                   - jnp.exp(jnp.sum(att_lam_q2[j] * att_lam_k2[j])) + lam_init)
            qkv_p = matmul_cols(xp, att_w_qkv[j], 3 * d, tm_p, 512).reshape(bp, lp, 3 * d)
            op = attn_prompt(qkv_p, lam, lam_init, attn_bias_tiles(rel_bias), att_subln_g[j])
            k_p = qkv_p[..., d:2 * d].reshape(bp, lp, ATT_HEADS, 2 * ATT_HEAD_DIM)
            v_p = qkv_p[..., 2 * d:].reshape(bp, lp, ATT_HEADS, 2 * ATT_HEAD_DIM)
            qkv_s = matmul_cols(xs, att_w_qkv[j], 3 * d, tm_s, 512).reshape(bs, ls, 3 * d)
            os_ = attn_decode(qkv_s.reshape(ts, 3 * d), cache_attn_k, cache_attn_v, j, page_table,
                              lam, lam_init, rel_bias, att_subln_g[j])
            k_s = qkv_s[..., d:2 * d].reshape(bs, ls, ATT_HEADS, 2 * ATT_HEAD_DIM)
            v_s = qkv_s[..., 2 * d:].reshape(bs, ls, ATT_HEADS, 2 * ATT_HEAD_DIM)
            outs["ak_p"].append(k_p); outs["ak_s"].append(k_s)
            outs["av_p"].append(v_p); outs["av_s"].append(v_s)
            w_out = att_w_out[j]
        xp = proj_residual_ln(op.reshape(tp, -1), w_out, xp, ln_g[i, 0], ln_b[i, 0], 512)
        xs = proj_residual_ln(os_.reshape(ts, -1), w_out, xs, ln_g[i, 0], ln_b[i, 0], tm_s)
        f = i // 2
        if i % 2 == 0:
            xp = dense_ffn_ln(xp, ffn_w_gu, ffn_w_down, f, ln_g[i, 1], ln_b[i, 1], tm_p)
            xs = dense_ffn_ln(xs, ffn_w_gu, ffn_w_down, f, ln_g[i, 1], ln_b[i, 1], tm_s)
        else:
            xp = moe_ffn_ln(xp, moe_w_router[f], moe_gu, moe_down, f, ln_g[i, 1], ln_b[i, 1],
                            512, 1024, 256)
            xs = moe_ffn_ln(xs, moe_w_router[f], moe_gu, moe_down, f, ln_g[i, 1], ln_b[i, 1],
                            ts, ts, ts)
    st = lambda name: jnp.stack(outs[name])
    return (xp.reshape(bp, lp, d), xs.reshape(bs, ls, d),
            st("gS_p"), st("gS_s"), st("gc_p"), st("gc_s"),
            st("sh_p"), st("sh_s"), st("sc_p"), st("sc_s"),
            st("ak_p"), st("ak_s"), st("av_p"), st("av_s"))
```

```python
import functools
import math

import numpy as np
import jax
import jax.numpy as jnp
from jax import lax
from jax.experimental import pallas as pl
from jax.experimental.pallas import tpu as pltpu

F32 = jnp.float32
BF16 = jnp.bfloat16

D_MODEL = 1024
DEPTH = 4
PAGE_SIZE = 128
N_MIXERS = 3
CONV_W = 4

GDN_HEADS = 8
GDN_DK = 128
GDN_DV = 128
GDN_KD = GDN_HEADS * GDN_DK
GDN_VD = GDN_HEADS * GDN_DV
GDN_CONV_DIM = 2 * GDN_KD + GDN_VD
GDN_CHUNK = 64

SSM_D_INNER = 2 * D_MODEL
SSM_HEAD_DIM = 64
SSM_HEADS = SSM_D_INNER // SSM_HEAD_DIM
SSM_GROUPS = 4
SSM_HPG = SSM_HEADS // SSM_GROUPS
SSM_STATE = 128
SSM_CONV_DIM = SSM_D_INNER + 2 * SSM_GROUPS * SSM_STATE
SSM_CHUNK = 128

ATT_HEADS = 8
ATT_HEAD_DIM = D_MODEL // (2 * ATT_HEADS)
N_BUCKETS = 32
MAX_DISTANCE = 128
Q_BLOCK = 128

D_FF = 3584
N_EXPERTS = 8
TOP_K = 2

DEEPNORM_ALPHA = (2.0 * DEPTH) ** 0.25
LN_EPS = 1e-5
RMS_EPS = 1e-6

V7X_VMEM_BYTES = 64 * 1024 * 1024
VMEM_LIMIT_BYTES = V7X_VMEM_BYTES - 8 * 1024 * 1024
LANES = 128

FF_CHUNK = 512
N_FF_CHUNKS = D_FF // FF_CHUNK


def _cparams(n_axes, flags=None):
    return pltpu.CompilerParams(dimension_semantics=("arbitrary",) * n_axes,
                                vmem_limit_bytes=VMEM_LIMIT_BYTES, flags=flags)


def _layer_norm_rows(x, g, b):
    mu = jnp.mean(x, axis=-1, keepdims=True)
    xc = x - mu
    var = jnp.mean(xc * xc, axis=-1, keepdims=True)
    return xc * lax.rsqrt(var + LN_EPS) * g + b


def _silu(x):
    return x * (1.0 / (1.0 + jnp.exp(-x)))


_HIGHEST = lax.Precision.HIGHEST


def _matmul_kernel(x_ref, w_ref, o_ref, xb_ref, *, precise):
    if precise:
        o_ref[...] = jnp.dot(x_ref[...], w_ref[...], preferred_element_type=F32, precision=_HIGHEST)
        return

    @pl.when(pl.program_id(1) == 0)
    def _():
        xb_ref[...] = x_ref[...].astype(BF16)

    o_ref[...] = jnp.dot(xb_ref[...], w_ref[...].astype(BF16), preferred_element_type=F32)


def matmul_cols(x, w, n_cols, tm, tn, precise=False):
    m, k = x.shape
    return pl.pallas_call(
        functools.partial(_matmul_kernel, precise=precise),
        grid=(m // tm, n_cols // tn),
        in_specs=[pl.BlockSpec((tm, k), lambda i, j: (i, 0)),
                  pl.BlockSpec((k, tn), lambda i, j: (0, j))],
        out_specs=pl.BlockSpec((tm, tn), lambda i, j: (i, j)),
        out_shape=jax.ShapeDtypeStruct((m, n_cols), F32),
        scratch_shapes=[pltpu.VMEM((tm, k), BF16)],
        compiler_params=_cparams(2),
        name="matmul_cols",
    )(x, w)


def _matmul_narrow_kernel(x_ref, w_ref, o_ref):
    o_ref[...] = jnp.dot(x_ref[...], w_ref[...], preferred_element_type=F32, precision=_HIGHEST)


def matmul_narrow(x, w_pad, tm):
    m, k = x.shape
    return pl.pallas_call(
        _matmul_narrow_kernel,
        grid=(m // tm,),
        in_specs=[pl.BlockSpec((tm, k), lambda i: (i, 0)),
                  pl.BlockSpec((k, LANES), lambda i: (0, 0))],
        out_specs=pl.BlockSpec((tm, LANES), lambda i: (i, 0)),
        out_shape=jax.ShapeDtypeStruct((m, LANES), F32),
        compiler_params=_cparams(1),
        name="matmul_narrow",
    )(x, w_pad)


def _proj_ln_kernel(o_ref, w_ref, r_ref, g_ref, b_ref, y_ref, wb_ref, *, precise):
    if precise:
        m = jnp.dot(o_ref[...], w_ref[...], preferred_element_type=F32, precision=_HIGHEST)
    else:
        @pl.when(pl.program_id(0) == 0)
        def _():
            wb_ref[...] = w_ref[...].astype(BF16)

        m = jnp.dot(o_ref[...].astype(BF16), wb_ref[...], preferred_element_type=F32)
    y_ref[...] = _layer_norm_rows(DEEPNORM_ALPHA * r_ref[...] + m, g_ref[...], b_ref[...])


def proj_residual_ln(o, w, resid, g, b, tm, precise=False):
    m, k = o.shape
    d = w.shape[1]
    return pl.pallas_call(
        functools.partial(_proj_ln_kernel, precise=precise),
        grid=(m // tm,),
        in_specs=[pl.BlockSpec((tm, k), lambda i: (i, 0)),
                  pl.BlockSpec((k, d), lambda i: (0, 0)),
                  pl.BlockSpec((tm, d), lambda i: (i, 0)),
                  pl.BlockSpec((1, d), lambda i: (0, 0)),
                  pl.BlockSpec((1, d), lambda i: (0, 0))],
        out_specs=pl.BlockSpec((tm, d), lambda i: (i, 0)),
        out_shape=jax.ShapeDtypeStruct((m, d), F32),
        scratch_shapes=[pltpu.VMEM((k, d), BF16)],
        compiler_params=_cparams(1),
        name="proj_residual_ln",
    )(o, w, resid, g.reshape(1, d), b.reshape(1, d))


def _ffn_kernel(te_ref, na_ref, x_ref, wg_ref, wu_ref, wd_ref, g_ref, b_ref, o_ref,
                xb_ref, acc_ref, *, fuse_ln, precise):
    i = pl.program_id(0)
    j = pl.program_id(1)
    active = i < na_ref[0]

    @pl.when(jnp.logical_and(active, j == 0))
    def _():
        if not precise:
            xb_ref[...] = x_ref[...].astype(BF16)
        acc_ref[...] = jnp.zeros(acc_ref.shape, F32)

    @pl.when(active)
    def _():
        if precise:
            x = x_ref[...]
            h = jnp.dot(x, wg_ref[...], preferred_element_type=F32, precision=_HIGHEST)
            u = jnp.dot(x, wu_ref[...], preferred_element_type=F32, precision=_HIGHEST)
            acc_ref[...] += jnp.dot(_silu(h) * u, wd_ref[...], preferred_element_type=F32,
                                    precision=_HIGHEST)
        else:
            xb = xb_ref[...]
            h = jnp.dot(xb, wg_ref[...].astype(BF16), preferred_element_type=F32)
            u = jnp.dot(xb, wu_ref[...].astype(BF16), preferred_element_type=F32)
            a = (_silu(h) * u).astype(BF16)
            acc_ref[...] += jnp.dot(a, wd_ref[...].astype(BF16), preferred_element_type=F32)

    @pl.when(jnp.logical_and(active, j == N_FF_CHUNKS - 1))
    def _():
        if fuse_ln:
            o_ref[...] = _layer_norm_rows(DEEPNORM_ALPHA * x_ref[...] + acc_ref[...],
                                          g_ref[...], b_ref[...])
        else:
            o_ref[...] = acc_ref[...]

    @pl.when(jnp.logical_and(jnp.logical_not(active), j == N_FF_CHUNKS - 1))
    def _():
        o_ref[...] = jnp.zeros_like(o_ref)


def swiglu_tiles(x, w_gu, w_down, tile_expert, n_active, ln_g, ln_b, tm, fuse_ln, precise=False):
    m, d = x.shape
    n_tiles = m // tm

    def row_map(i, j, te, na):
        return (jnp.minimum(i, na[0] - 1), 0)

    def wg_map(i, j, te, na):
        return (te[jnp.minimum(i, na[0] - 1)], 0, j)

    def wu_map(i, j, te, na):
        return (te[jnp.minimum(i, na[0] - 1)], 0, j + N_FF_CHUNKS)

    def wd_map(i, j, te, na):
        return (te[jnp.minimum(i, na[0] - 1)], j, 0)

    grid_spec = pltpu.PrefetchScalarGridSpec(
        num_scalar_prefetch=2,
        grid=(n_tiles, N_FF_CHUNKS),
        in_specs=[pl.BlockSpec((tm, d), row_map),
                  pl.BlockSpec((None, d, FF_CHUNK), wg_map),
                  pl.BlockSpec((None, d, FF_CHUNK), wu_map),
                  pl.BlockSpec((None, FF_CHUNK, d), wd_map),
                  pl.BlockSpec((1, d), lambda i, j, te, na: (0, 0)),
                  pl.BlockSpec((1, d), lambda i, j, te, na: (0, 0))],
        out_specs=pl.BlockSpec((tm, d), lambda i, j, te, na: (i, 0)),
        scratch_shapes=[pltpu.VMEM((tm, d), BF16), pltpu.VMEM((tm, d), F32)],
    )
    return pl.pallas_call(
        functools.partial(_ffn_kernel, fuse_ln=fuse_ln, precise=precise),
        grid_spec=grid_spec,
        out_shape=jax.ShapeDtypeStruct((m, d), F32),
        compiler_params=_cparams(2),
        name="swiglu_ln" if fuse_ln else "swiglu_grouped",
    )(tile_expert, n_active, x, w_gu, w_gu, w_down, ln_g.reshape(1, d), ln_b.reshape(1, d))


def dense_ffn_ln(x, w_gu_all, w_down_all, layer, ln_g, ln_b, tm, precise=False):
    n_tiles = x.shape[0] // tm
    te = jnp.full((n_tiles,), layer, jnp.int32)
    na = jnp.full((1,), n_tiles, jnp.int32)
    return swiglu_tiles(x, w_gu_all, w_down_all, te, na, ln_g, ln_b, tm, True, precise)


def _router_kernel(x_ref, w_ref, meta_ref, cnt_ref, carry_ref, *, tm):
    i = pl.program_id(0)

    @pl.when(i == 0)
    def _():
        carry_ref[...] = jnp.zeros_like(carry_ref)

    logits = jnp.dot(x_ref[...], w_ref[...], preferred_element_type=F32, precision=_HIGHEST)
    lane = lax.broadcasted_iota(jnp.int32, (tm, LANES), 1)
    neg = jnp.float32(-jnp.inf)
    logits = jnp.where(lane < N_EXPERTS, logits, neg)
    m1 = jnp.max(logits, axis=-1, keepdims=True)
    i1 = jnp.min(jnp.where(logits == m1, lane, LANES), axis=-1, keepdims=True)
    rest = jnp.where(lane == i1, neg, logits)
    m2 = jnp.max(rest, axis=-1, keepdims=True)
    i2 = jnp.min(jnp.where(rest == m2, lane, LANES), axis=-1, keepdims=True)
    e = jnp.exp(m2 - m1)
    g1 = 1.0 / (1.0 + e)
    g2 = e / (1.0 + e)

    chosen = jnp.logical_or(lane == i1, lane == i2)
    onehot = jnp.where(chosen, 1.0, 0.0).astype(BF16)
    row = lax.broadcasted_iota(jnp.int32, (tm, tm), 0)
    col = lax.broadcasted_iota(jnp.int32, (tm, tm), 1)
    strict_lower = jnp.where(col < row, 1.0, 0.0).astype(BF16)
    before = jnp.dot(strict_lower, onehot, preferred_element_type=F32) + carry_ref[...]
    r1 = jnp.sum(jnp.where(lane == i1, before, 0.0), axis=-1, keepdims=True)
    r2 = jnp.sum(jnp.where(lane == i2, before, 0.0), axis=-1, keepdims=True)
    carry_ref[...] += jnp.sum(onehot.astype(F32), axis=0, keepdims=True)

    meta = jnp.where(lane == 0, i1.astype(F32),
           jnp.where(lane == 1, i2.astype(F32),
           jnp.where(lane == 2, g1,
           jnp.where(lane == 3, g2,
           jnp.where(lane == 4, r1,
           jnp.where(lane == 5, r2, 0.0))))))
    meta_ref[...] = meta
    cnt_ref[...] = carry_ref[...]


def route_top2(x, w_router, tm):
    t, d = x.shape
    w_pad = jnp.pad(w_router, ((0, 0), (0, LANES - N_EXPERTS)))
    return pl.pallas_call(
        functools.partial(_router_kernel, tm=tm),
        grid=(t // tm,),
        in_specs=[pl.BlockSpec((tm, d), lambda i: (i, 0)),
                  pl.BlockSpec((d, LANES), lambda i: (0, 0))],
        out_specs=[pl.BlockSpec((tm, LANES), lambda i: (i, 0)),
                   pl.BlockSpec((1, LANES), lambda i: (0, 0))],
        out_shape=[jax.ShapeDtypeStruct((t, LANES), F32),
                   jax.ShapeDtypeStruct((1, LANES), F32)],
        scratch_shapes=[pltpu.VMEM((1, LANES), F32)],
        compiler_params=_cparams(1),
        name="route_top2",
    )(x, w_pad)


def _dispatch_kernel(pos_ref, x_ref, xs_in_hbm, xs_hbm, sem, *, tokens_per_step):
    del xs_in_hbm

    def row_copy(t, k):
        return pltpu.make_async_copy(x_ref.at[pl.ds(t, 1)],
                                     xs_hbm.at[pl.ds(pos_ref[0, 2 * t + k], 1)], sem)

    def issue(t, carry):
        row_copy(t, 0).start()
        row_copy(t, 1).start()
        return carry

    lax.fori_loop(0, tokens_per_step, issue, 0, unroll=8)

    def drain(t, carry):
        row_copy(t, 0).wait()
        row_copy(t, 1).wait()
        return carry

    lax.fori_loop(0, tokens_per_step, drain, 0, unroll=8)


def dispatch_rows(x, pos, n_rows, tokens_per_step):
    t, d = x.shape
    n_steps = t // tokens_per_step
    pos2 = pos.reshape(n_steps, 1, 2 * tokens_per_step)
    zeros = jnp.zeros((n_rows, d), x.dtype)
    return pl.pallas_call(
        functools.partial(_dispatch_kernel, tokens_per_step=tokens_per_step),
        grid=(n_steps,),
        in_specs=[pl.BlockSpec((None, 1, 2 * tokens_per_step), lambda i: (i, 0, 0),
                               memory_space=pltpu.SMEM),
                  pl.BlockSpec((tokens_per_step, d), lambda i: (i, 0)),
                  pl.BlockSpec(memory_space=pl.ANY)],
        out_specs=pl.BlockSpec(memory_space=pl.ANY),
        out_shape=jax.ShapeDtypeStruct((n_rows, d), x.dtype),
        scratch_shapes=[pltpu.SemaphoreType.DMA(())],
        input_output_aliases={2: 0},
        compiler_params=_cparams(1),
        name="dispatch_rows",
    )(pos2, x, zeros)


def _combine_ln_kernel(pos_ref, x_ref, gate_ref, g_ref, b_ref, ys_hbm, o_ref, buf_ref, sem, *, tm):
    def row_copy(t, k):
        return pltpu.make_async_copy(ys_hbm.at[pl.ds(pos_ref[0, 2 * t + k], 1)],
                                     buf_ref.at[k, pl.ds(t, 1)], sem)

    def issue(t, carry):
        row_copy(t, 0).start()
        row_copy(t, 1).start()
        return carry

    lax.fori_loop(0, tm, issue, 0, unroll=8)

    def drain(t, carry):
        row_copy(t, 0).wait()
        row_copy(t, 1).wait()
        return carry

    lax.fori_loop(0, tm, drain, 0, unroll=8)

    gates = gate_ref[...]
    y = gates[:, 2:3] * buf_ref[0] + gates[:, 3:4] * buf_ref[1]
    o_ref[...] = _layer_norm_rows(DEEPNORM_ALPHA * x_ref[...] + y, g_ref[...], b_ref[...])


def combine_ln(x, ys, pos, meta, ln_g, ln_b, tm):
    t, d = x.shape
    n_steps = t // tm
    pos2 = pos.reshape(n_steps, 1, 2 * tm)
    return pl.pallas_call(
        functools.partial(_combine_ln_kernel, tm=tm),
        grid=(n_steps,),
        in_specs=[pl.BlockSpec((None, 1, 2 * tm), lambda i: (i, 0, 0), memory_space=pltpu.SMEM),
                  pl.BlockSpec((tm, d), lambda i: (i, 0)),
                  pl.BlockSpec((tm, LANES), lambda i: (i, 0)),
                  pl.BlockSpec((1, d), lambda i: (0, 0)),
                  pl.BlockSpec((1, d), lambda i: (0, 0)),
                  pl.BlockSpec(memory_space=pl.ANY)],
        out_specs=pl.BlockSpec((tm, d), lambda i: (i, 0)),
        out_shape=jax.ShapeDtypeStruct((t, d), F32),
        scratch_shapes=[pltpu.VMEM((2, tm, d), F32), pltpu.SemaphoreType.DMA(())],
        compiler_params=_cparams(1),
        name="combine_ln",
    )(pos2, x, meta, ln_g.reshape(1, d), ln_b.reshape(1, d), ys)


def moe_ffn_ln(x, w_router, w_gu_all, w_down_all, layer, ln_g, ln_b, tm_route, tm_group, tm_combine,
               precise=False):
    t, d = x.shape
    meta, counts = route_top2(x, w_router, tm_route)
    ids = meta[:, 0:2].astype(jnp.int32)
    ranks = meta[:, 4:6].astype(jnp.int32)
    cnt = counts[0, :N_EXPERTS].astype(jnp.int32)
    tiles_per = (cnt + tm_group - 1) // tm_group
    tile_end = jnp.cumsum(tiles_per)
    group_start = (tile_end - tiles_per) * tm_group
    pos = group_start[ids] + ranks
    n_tiles = min((TOP_K * t + N_EXPERTS * (tm_group - 1)) // tm_group,
                  N_EXPERTS * ((t + tm_group - 1) // tm_group))
    n_active = tile_end[N_EXPERTS - 1:N_EXPERTS]
    tile_ids = jnp.arange(n_tiles, dtype=jnp.int32)
    tile_expert = jnp.sum((tile_ids[:, None] >= tile_end[None, :]).astype(jnp.int32), axis=1)
    tile_expert = jnp.minimum(tile_expert, N_EXPERTS - 1) + layer * N_EXPERTS
    xs = dispatch_rows(x, pos, n_tiles * tm_group, min(t, 1024))
    ys = swiglu_tiles(xs, w_gu_all, w_down_all, tile_expert.astype(jnp.int32),
                      n_active.astype(jnp.int32), ln_g, ln_b, tm_group, False, precise)
    return combine_ln(x, ys, pos, meta, ln_g, ln_b, tm_combine)


def _t5_bucket_runs():
    max_exact = N_BUCKETS // 2
    n = np.arange(MAX_DISTANCE)
    scaled = (np.log(np.maximum(n, max_exact).astype(np.float64) / max_exact)
              / math.log(MAX_DISTANCE / max_exact) * (N_BUCKETS - max_exact))
    frac = scaled - np.floor(scaled)
    assert np.all((np.minimum(frac, 1 - frac) > 1e-3) | (n <= max_exact))
    bucket = np.where(n < max_exact, n, np.minimum(max_exact + scaled.astype(np.int64), N_BUCKETS - 1))
    runs = []
    for dist in range(MAX_DISTANCE):
        if runs and runs[-1][1] == int(bucket[dist]):
            runs[-1] = (dist, int(bucket[dist]))
        else:
            runs.append((dist, int(bucket[dist])))
    return runs, bucket


T5_RUNS, T5_BUCKETS = _t5_bucket_runs()
ATT_TQ = 512
ATT_ROW_BLOCK = 256


def _bias_tile_kernel(rel_ref, o_ref):
    h = pl.program_id(0)
    tq, tk2 = o_ref.shape
    row = lax.broadcasted_iota(jnp.int32, (tq, tk2), 0)
    col = lax.broadcasted_iota(jnp.int32, (tq, tk2), 1)
    dist = row - col + tq
    far = rel_ref[N_BUCKETS - 1, h]
    acc = jnp.zeros((tq, tk2), F32)
    for last, bucket in reversed(T5_RUNS):
        acc = jnp.where(dist <= last, rel_ref[bucket, h] - far, acc)
    o_ref[...] = jnp.where(dist < 0, -jnp.inf, acc)


def attn_bias_tiles(rel_bias):
    return pl.pallas_call(
        _bias_tile_kernel,
        grid=(ATT_HEADS,),
        in_specs=[pl.BlockSpec(memory_space=pltpu.SMEM)],
        out_specs=pl.BlockSpec((None, ATT_TQ, 2 * ATT_TQ), lambda h: (h, 0, 0)),
        out_shape=jax.ShapeDtypeStruct((ATT_HEADS, ATT_TQ, 2 * ATT_TQ), F32),
        compiler_params=_cparams(1),
        name="attn_bias_tiles",
    )(rel_bias)


def _softmax_step(s, v, m_ref, l_ref, acc_ref):
    m_prev = m_ref[...]
    m_new = jnp.maximum(m_prev, jnp.max(s, axis=-1, keepdims=True))
    alpha = jnp.exp(m_prev - m_new)
    p = jnp.exp(s - pltpu.repeat(m_new, s.shape[1] // LANES, axis=1))
    l_ref[...] = alpha * l_ref[...] + jnp.sum(p, axis=-1, keepdims=True)
    acc_ref[...] = alpha * acc_ref[...] + jnp.dot(p.astype(BF16), v, preferred_element_type=F32)
    m_ref[...] = m_new


def _attn_prompt_kernel(lam_ref, q_ref, k_ref, v_ref, bt_ref, g_ref, o_ref,
                        kb_ref, vb_ref, m1_ref, l1_ref, a1_ref, m2_ref, l2_ref, a2_ref, *, out_scale):
    qi = pl.program_id(2)
    tq = q_ref.shape[0]
    half = ATT_HEAD_DIM

    @pl.when(qi == 0)
    def _():
        kb_ref[...] = k_ref[...].astype(BF16)
        vb_ref[...] = v_ref[...].astype(BF16)

    q = q_ref[...] * (ATT_HEAD_DIM ** -0.5)
    lane = lax.broadcasted_iota(jnp.int32, q.shape, 1)
    q1 = jnp.where(lane < half, q, 0.0).astype(BF16)
    q2 = jnp.where(lane >= half, q, 0.0).astype(BF16)

    for m_ref, l_ref, a_ref in ((m1_ref, l1_ref, a1_ref), (m2_ref, l2_ref, a2_ref)):
        m_ref[...] = jnp.full(m_ref.shape, -jnp.inf, F32)
        l_ref[...] = jnp.zeros(l_ref.shape, F32)
        a_ref[...] = jnp.zeros(a_ref.shape, F32)

    nt = (((1,), (1,)), ((), ()))
    rb = ATT_ROW_BLOCK
    stats = ((q1, m1_ref, l1_ref, a1_ref), (q2, m2_ref, l2_ref, a2_ref))
    units = [(r, mp) for r in range(tq // rb) for mp in range(2)]

    def chunk(start, bias):
        k = kb_ref[pl.ds(start, tq), :]
        v = vb_ref[pl.ds(start, tq), :]

        def scores(u):
            r, mp = u
            s = lax.dot_general(stats[mp][0][r * rb:(r + 1) * rb], k, nt, preferred_element_type=F32)
            return s if bias is None else s + bias[r * rb:(r + 1) * rb, :]

        s_cur = scores(units[0])
        for i, (r, mp) in enumerate(units):
            s_next = scores(units[i + 1]) if i + 1 < len(units) else None
            _, m_ref, l_ref, a_ref = stats[mp]
            rows = slice(r * rb, (r + 1) * rb)
            _softmax_step(s_cur, v, m_ref.at[rows], l_ref.at[rows], a_ref.at[rows])
            s_cur = s_next

    def far_chunk(c, carry):
        chunk(pl.multiple_of(c * tq, tq), None)
        return carry

    lax.fori_loop(0, qi - 1, far_chunk, 0)

    @pl.when(qi > 0)
    def _():
        chunk(pl.multiple_of((qi - 1) * tq, tq), bt_ref[:, 0:tq])

    chunk(pl.multiple_of(qi * tq, tq), bt_ref[:, tq:2 * tq])

    o = a1_ref[...] / l1_ref[...] - lam_ref[0] * (a2_ref[...] / l2_ref[...])
    o = o * lax.rsqrt(jnp.mean(o * o, axis=-1, keepdims=True) + RMS_EPS) * g_ref[...]
    o_ref[...] = o * out_scale


def attn_prompt(qkv, lam, lam_init, bias_tiles, subln_g):
    bn, l, _ = qkv.shape
    hd = 2 * ATT_HEAD_DIM
    tq = ATT_TQ
    stat = pltpu.VMEM((tq, hd), F32)
    return pl.pallas_call(
        functools.partial(_attn_prompt_kernel, out_scale=1.0 - lam_init),
        grid=(bn, ATT_HEADS, l // tq),
        in_specs=[pl.BlockSpec(memory_space=pltpu.SMEM),
                  pl.BlockSpec((None, tq, hd), lambda b, h, i: (b, i, h)),
                  pl.BlockSpec((None, l, hd), lambda b, h, i: (b, 0, ATT_HEADS + h)),
                  pl.BlockSpec((None, l, hd), lambda b, h, i: (b, 0, 2 * ATT_HEADS + h)),
                  pl.BlockSpec((None, tq, 2 * tq), lambda b, h, i: (h, 0, 0)),
                  pl.BlockSpec((1, hd), lambda b, h, i: (0, 0))],
        out_specs=pl.BlockSpec((None, tq, hd), lambda b, h, i: (b, i, h)),
        out_shape=jax.ShapeDtypeStruct((bn, l, ATT_HEADS * hd), F32),
        scratch_shapes=[pltpu.VMEM((l, hd), BF16), pltpu.VMEM((l, hd), BF16),
                        stat, stat, stat, stat, stat, stat],
        compiler_params=_cparams(3),
        name="attn_prompt",
    )(lam.reshape(1), qkv, qkv, qkv, bias_tiles, subln_g.reshape(1, hd))


ATT_PAGES_PER_STEP = 4
ATT_ROWS = 2 * ATT_HEADS


def _row_select(vec_row, h, lane_lo, lane_hi):
    rid = lax.broadcasted_iota(jnp.int32, (ATT_ROWS, LANES), 0)
    lane = lax.broadcasted_iota(jnp.int32, (ATT_ROWS, LANES), 1)
    full = jnp.broadcast_to(vec_row, (ATT_ROWS, LANES))
    keep = jnp.logical_or(jnp.logical_and(rid == 2 * h, lane < lane_hi),
                          jnp.logical_and(rid == 2 * h + 1, lane >= lane_lo))
    return jnp.where(keep, full, 0.0)


def _attn_decode_kernel(pt_ref, lam_ref, qkv_ref, bl_ref, bc_ref, g_ref, *refs, out_scale):
    npg = ATT_PAGES_PER_STEP
    k_refs, v_refs = refs[:npg], refs[npg:2 * npg]
    o_ref, qs_ref, m_ref, l_ref, acc_ref = refs[2 * npg:]
    del pt_ref
    step = pl.program_id(1)
    last = pl.num_programs(1) - 1
    rid = lax.broadcasted_iota(jnp.int32, (ATT_ROWS, LANES), 0)
    half = ATT_HEAD_DIM
    hd = 2 * ATT_HEAD_DIM

    @pl.when(step == 0)
    def _():
        m_ref[...] = jnp.full(m_ref.shape, -jnp.inf, F32)
        l_ref[...] = jnp.zeros(l_ref.shape, F32)
        acc_ref[...] = jnp.zeros(acc_ref.shape, F32)
        for h in range(ATT_HEADS):
            qh = qkv_ref[:, h * hd:(h + 1) * hd] * (ATT_HEAD_DIM ** -0.5)
            qs_ref[h] = _row_select(qh, h, half, half).astype(BF16)

    def head_rows(page_refs, h):
        return jnp.concatenate(
            [r[pl.ds(h, PAGE_SIZE, stride=ATT_HEADS), :].astype(BF16) for r in page_refs], axis=0)

    s = None
    for h in range(ATT_HEADS):
        t = lax.dot_general(qs_ref[h], head_rows(k_refs, h), _NT, preferred_element_type=F32)
        s = t if s is None else s + t
    tail_bias = jnp.where(step == last, 1.0, 0.0) * bl_ref[...]
    s = jnp.concatenate([s[:, :(npg - 1) * PAGE_SIZE], s[:, (npg - 1) * PAGE_SIZE:] + tail_bias], axis=1)

    m_prev = m_ref[...]
    m_new = jnp.maximum(m_prev, jnp.max(s, axis=-1, keepdims=True))
    alpha = jnp.exp(m_prev - m_new)
    p = jnp.exp(s - m_new[:, 0:1])
    l_ref[...] = alpha * l_ref[...] + jnp.sum(p, axis=-1, keepdims=True)
    pb = p.astype(BF16)
    pv = None
    for h in range(ATT_HEADS):
        t = jnp.dot(pb, head_rows(v_refs, h), preferred_element_type=F32)
        t = jnp.where(jnp.logical_or(rid == 2 * h, rid == 2 * h + 1), t, 0.0)
        pv = t if pv is None else pv + t
    acc_ref[...] = alpha * acc_ref[...] + pv
    m_ref[...] = m_new

    @pl.when(step == last)
    def _():
        s_cur = None
        v_rows = None
        for h in range(ATT_HEADS):
            kn = qkv_ref[:, D_MODEL + h * hd:D_MODEL + (h + 1) * hd].astype(BF16).astype(F32)
            vn = qkv_ref[:, 2 * D_MODEL + h * hd:2 * D_MODEL + (h + 1) * hd].astype(BF16).astype(F32)
            t = jnp.sum(qs_ref[h].astype(F32) * kn, axis=-1, keepdims=True)
            s_cur = t if s_cur is None else s_cur + t
            vr = _row_select(vn, h, 0, LANES)
            v_rows = vr if v_rows is None else v_rows + vr
        s_cur = s_cur + bc_ref[...][:, 0:1]
        m_prev = m_ref[...]
        m_new = jnp.maximum(m_prev, s_cur)
        alpha = jnp.exp(m_prev - m_new)
        p = jnp.exp(s_cur - m_new[:, 0:1])
        l_fin = alpha * l_ref[...] + p
        acc = alpha * acc_ref[...] + p.astype(BF16).astype(F32) * v_rows
        o = acc / l_fin
        er = lax.broadcasted_iota(jnp.int32, (ATT_HEADS, ATT_ROWS), 0)
        ec = lax.broadcasted_iota(jnp.int32, (ATT_HEADS, ATT_ROWS), 1)
        pick1 = jnp.where(ec == 2 * er, 1.0, 0.0).astype(BF16)
        pick2 = jnp.where(ec == 2 * er + 1, 1.0, 0.0).astype(BF16)
        hi, mid, lo = _split3(o)
        o1 = (jnp.dot(pick1, hi, preferred_element_type=F32) + jnp.dot(pick1, mid, preferred_element_type=F32)
              + jnp.dot(pick1, lo, preferred_element_type=F32))
        o2 = (jnp.dot(pick2, hi, preferred_element_type=F32) + jnp.dot(pick2, mid, preferred_element_type=F32)
              + jnp.dot(pick2, lo, preferred_element_type=F32))
        d = o1 - lam_ref[0] * o2
        d = d * lax.rsqrt(jnp.mean(d * d, axis=-1, keepdims=True) + RMS_EPS) * g_ref[...]
        o_ref[...] = d * out_scale


def attn_decode(qkv, k_pages, v_pages, layer, page_table, lam, lam_init, rel_bias, subln_g):
    bn = qkv.shape[0]
    n_pages = page_table.shape[1]
    npg = ATT_PAGES_PER_STEP
    hd = 2 * ATT_HEAD_DIM
    dist = PAGE_SIZE - np.arange(PAGE_SIZE)
    bucket = np.where(dist < MAX_DISTANCE, T5_BUCKETS[np.minimum(dist, MAX_DISTANCE - 1)], N_BUCKETS - 1)
    far = rel_bias[N_BUCKETS - 1]
    bias_last = jnp.repeat((rel_bias[bucket] - far).T, 2, axis=0)
    bias_cur = jnp.broadcast_to(jnp.repeat(rel_bias[0] - far, 2)[:, None], (ATT_ROWS, LANES))

    def rows_of(pages):
        return pages.reshape(pages.shape[0], pages.shape[1], PAGE_SIZE * ATT_HEADS, hd)

    def page_spec(pg):
        return pl.BlockSpec((None, None, PAGE_SIZE * ATT_HEADS, hd),
                            lambda b, s, pt: (layer, pt[b, s * npg + pg], 0, 0))

    full = lambda shape: pl.BlockSpec(shape, lambda b, s, pt: (0,) * len(shape))
    stat = pltpu.VMEM((ATT_ROWS, LANES), F32)
    grid_spec = pltpu.PrefetchScalarGridSpec(
        num_scalar_prefetch=1,
        grid=(bn, n_pages // npg),
        in_specs=[pl.BlockSpec(memory_space=pltpu.SMEM),
                  pl.BlockSpec((None, 1, 3 * D_MODEL), lambda b, s, pt: (b, 0, 0)),
                  full((ATT_ROWS, PAGE_SIZE)), full((ATT_ROWS, LANES)), full((1, hd))]
                 + [page_spec(pg) for pg in range(npg)] * 2,
        out_specs=pl.BlockSpec((None, ATT_HEADS, hd), lambda b, s, pt: (b, 0, 0)),
        scratch_shapes=[pltpu.VMEM((ATT_HEADS, ATT_ROWS, LANES), BF16), stat, stat, stat],
    )
    out = pl.pallas_call(
        functools.partial(_attn_decode_kernel, out_scale=1.0 - lam_init),
        grid_spec=grid_spec,
        out_shape=jax.ShapeDtypeStruct((bn, ATT_HEADS, hd), F32),
        compiler_params=_cparams(2),
        name="attn_decode",
    )(page_table, lam.reshape(1), qkv.reshape(bn, 1, 3 * D_MODEL), bias_last, bias_cur,
      subln_g.reshape(1, hd), *([rows_of(k_pages)] * npg), *([rows_of(v_pages)] * npg))
    return out.reshape(bn, D_MODEL)


def _split3(v):
    hi = v.astype(BF16)
    r1 = v - hi.astype(F32)
    mid = r1.astype(BF16)
    lo = (r1 - mid.astype(F32)).astype(BF16)
    return hi, mid, lo


def _dot3(v, w_bf16, dims=None):
    out = None
    for piece in _split3(v):
        if dims is None:
            t = jnp.dot(piece, w_bf16, preferred_element_type=F32)
        else:
            t = lax.dot_general(piece, w_bf16, dims, preferred_element_type=F32)
        out = t if out is None else out + t
    return out


def _softplus(x):
    return jnp.maximum(x, 0.0) + jnp.log1p(jnp.exp(-jnp.abs(x)))


_NT = (((1,), (1,)), ((), ()))
_TN = (((0,), (0,)), ((), ()))
SSM_GROUP_W = SSM_HPG * SSM_HEAD_DIM
SSM_GN = SSM_GROUPS * SSM_STATE


def _ssd_prompt_kernel(proj_ref, dt_ref, cw_ref, cb_ref, dtb_ref, aneg_ref, dskip_ref, ng_ref, e_ref,
                       y_ref, h_ref, xp_ref, ht_ref):
    ci = pl.program_id(1)
    c = SSM_CHUNK
    di = SSM_D_INNER

    @pl.when(ci == 0)
    def _():
        xp_ref[0:8, :] = jnp.zeros((8, SSM_CONV_DIM), F32)
        ht_ref[...] = jnp.zeros(ht_ref.shape, F32)

    xbc = proj_ref[:, di:di + SSM_CONV_DIM]
    xp_ref[8:8 + c, :] = xbc
    conv = cb_ref[...] + xp_ref[5:5 + c, :] * cw_ref[0:1, :]
    for i in range(1, CONV_W):
        conv = conv + xp_ref[5 + i:5 + i + c, :] * cw_ref[i:i + 1, :]
    xp_ref[0:8, :] = xbc[c - 8:c, :]
    act = _silu(conv)
    xs = act[:, :di]

    dt = _softplus(dt_ref[...] + dtb_ref[...])
    a = dt * aneg_ref[...]
    row = lax.broadcasted_iota(jnp.int32, (c, c), 0)
    col = lax.broadcasted_iota(jnp.int32, (c, c), 1)
    incl = col <= row
    tri = jnp.where(incl, 1.0, 0.0).astype(BF16)
    hi, mid, lo = _split3(a)
    acum = (jnp.dot(tri, hi, preferred_element_type=F32) + jnp.dot(tri, mid, preferred_element_type=F32)
            + jnp.dot(tri, lo, preferred_element_type=F32))
    upper = jnp.where(row <= col, 1.0, 0.0).astype(BF16)
    acum_t = _dot3(a, upper, _TN)
    a_end = acum[c - 1:c, :]
    eac = jnp.exp(acum)
    wend = jnp.exp(a_end - acum) * dt
    ex = _dot3(jnp.concatenate([dt, eac, wend], axis=0), e_ref[...])
    dt_x, eac_x, wend_x = ex[0:c], ex[c:2 * c], ex[2 * c:3 * c]
    xdt = (xs * dt_x).astype(BF16)
    xw = (xs * wend_x).astype(BF16)
    lane = lax.broadcasted_iota(jnp.int32, (c, LANES), 1)
    lo_half = lane < SSM_HEAD_DIM
    neg_inf = jnp.float32(-jnp.inf)

    for g in range(SSM_GROUPS):
        gs = slice(g * SSM_GROUP_W, (g + 1) * SSM_GROUP_W)
        bg = act[:, di + g * SSM_STATE:di + (g + 1) * SSM_STATE].astype(BF16)
        cg = act[:, di + SSM_GN + g * SSM_STATE:di + SSM_GN + (g + 1) * SSM_STATE].astype(BF16)
        cb = lax.dot_general(cg, bg, _NT, preferred_element_type=F32)
        pieces = []
        for pr in range(SSM_GROUP_W // LANES):
            pair = g * (SSM_GROUP_W // LANES) + pr
            xd = xdt[:, pair * LANES:(pair + 1) * LANES]
            acc = None
            for half in range(2):
                h = 2 * pair + half
                diff = acum[:, h:h + 1] - acum_t[h:h + 1, :]
                s = (cb * jnp.exp(jnp.where(incl, diff, neg_inf))).astype(BF16)
                xm = jnp.where(lo_half if half == 0 else jnp.logical_not(lo_half), xd, jnp.zeros_like(xd))
                t = jnp.dot(s, xm, preferred_element_type=F32)
                acc = t if acc is None else acc + t
            pieces.append(acc)
        y_intra = jnp.concatenate(pieces, axis=1)
        h_old = ht_ref[:, gs]
        y_inter = jnp.dot(cg, h_old.astype(BF16), preferred_element_type=F32) * eac_x[:, gs]
        dstate = lax.dot_general(bg, xw[:, gs], _TN, preferred_element_type=F32)
        ht_ref[:, gs] = h_old * eac_x[c - 1:c, gs] + dstate
        y = y_intra + y_inter + dskip_ref[:, gs] * xs[:, gs]
        y = y * _silu(proj_ref[:, gs])
        y = y * lax.rsqrt(jnp.mean(y * y, axis=-1, keepdims=True) + RMS_EPS) * ng_ref[:, gs]
        y_ref[:, gs] = y

    @pl.when(ci == pl.num_programs(1) - 1)
    def _():
        h_ref[...] = ht_ref[...].T


def _head_expander():
    e = np.zeros((LANES, SSM_D_INNER), np.float32)
    for h in range(SSM_HEADS):
        e[h, h * SSM_HEAD_DIM:(h + 1) * SSM_HEAD_DIM] = 1.0
    return jnp.asarray(e, BF16)


def _pad_lanes(v):
    return jnp.pad(v.astype(F32), (0, LANES - v.shape[0])).reshape(1, LANES)


def ssd_prompt(proj, dtp, conv_w, conv_b, a_log, dt_bias, d_skip, norm_g):
    bn, l, pw = proj.shape
    c = SSM_CHUNK
    di = SSM_D_INNER
    full = lambda shape: pl.BlockSpec(shape, lambda b, i: (0,) * len(shape))
    y, h = pl.pallas_call(
        _ssd_prompt_kernel,
        grid=(bn, l // c),
        in_specs=[pl.BlockSpec((None, c, pw), lambda b, i: (b, i, 0)),
                  pl.BlockSpec((None, c, LANES), lambda b, i: (b, i, 0)),
                  full((CONV_W, SSM_CONV_DIM)), full((1, SSM_CONV_DIM)),
                  full((1, LANES)), full((1, LANES)), full((1, di)), full((1, di)),
                  full((LANES, di))],
        out_specs=[pl.BlockSpec((None, c, di), lambda b, i: (b, i, 0)),
                   pl.BlockSpec((None, di, SSM_STATE), lambda b, i: (b, 0, 0))],
        out_shape=[jax.ShapeDtypeStruct((bn, l, di), F32),
                   jax.ShapeDtypeStruct((bn, di, SSM_STATE), F32)],
        scratch_shapes=[pltpu.VMEM((8 + c, SSM_CONV_DIM), F32), pltpu.VMEM((SSM_STATE, di), F32)],
        compiler_params=_cparams(2),
        name="ssd_prompt",
    )(proj, dtp, conv_w, conv_b.reshape(1, -1), _pad_lanes(dt_bias), _pad_lanes(-jnp.exp(a_log)),
      jnp.repeat(d_skip, SSM_HEAD_DIM).reshape(1, di), norm_g.reshape(1, di), _head_expander())
    return y, h.reshape(bn, SSM_HEADS, SSM_HEAD_DIM, SSM_STATE)


GDN_BLK = 512
GDN_HEADS_PER_STEP = 2


def _split2(v):
    hi = v.astype(BF16)
    return hi, (v - hi.astype(F32)).astype(BF16)


def _mm_split(a, b):
    (a_hi, a_lo), (b_hi, b_lo) = a, b
    return (jnp.dot(a_hi, b_hi, preferred_element_type=F32) + jnp.dot(a_hi, b_lo, preferred_element_type=F32)
            + jnp.dot(a_lo, b_hi, preferred_element_type=F32))


def _conv_silu_block(x_ref, xp_ref, cw_ref):
    n = x_ref.shape[0]
    x = x_ref[...]
    xp_ref[8:8 + n, :] = x
    y = xp_ref[5:5 + n, :] * cw_ref[0:1, :]
    for i in range(1, CONV_W):
        y = y + xp_ref[5 + i:5 + i + n, :] * cw_ref[i:i + 1, :]
    xp_ref[0:8, :] = x[n - 8:n, :]
    return _silu(y)


def _gdn_prompt_kernel(alog_ref, dtb_ref, q_ref, k_ref, v_ref, z_ref, ba_ref, cwq_ref, cwk_ref, cwv_ref,
                       ng_ref, tri_ref, upper_ref, o_ref, s_out_ref, xq_ref, xk_ref, xv_ref, s_ref):
    hp = pl.program_id(1)
    bi = pl.program_id(2)
    blk = q_ref.shape[0]
    c = GDN_CHUNK
    hd = GDN_DK
    nhs = GDN_HEADS_PER_STEP

    @pl.when(bi == 0)
    def _():
        for r in (xq_ref, xk_ref, xv_ref):
            r[0:8, :] = jnp.zeros((8, r.shape[1]), F32)
        s_ref[...] = jnp.zeros(s_ref.shape, F32)

    q_all = _conv_silu_block(q_ref, xq_ref, cwq_ref)
    k_all = _conv_silu_block(k_ref, xk_ref, cwk_ref)
    v_all = _conv_silu_block(v_ref, xv_ref, cwv_ref)
    lane = lax.broadcasted_iota(jnp.int32, (blk, LANES), 1)
    ba = ba_ref[...]
    tri_bd = tri_ref[...]
    upper_bd = upper_ref[...]

    row = lax.broadcasted_iota(jnp.int32, (c, c), 0)
    col = lax.broadcasted_iota(jnp.int32, (c, c), 1)
    incl = col <= row
    strict = col < row
    eye = jnp.where(row == col, 1.0, 0.0)
    neg_inf = jnp.float32(-jnp.inf)
    nch = blk // c

    qh, kh, vh, betah, gcolh, growh, egh = [], [], [], [], [], [], []
    for e in range(nhs):
        h = hp * nhs + e
        hs = slice(e * hd, (e + 1) * hd)
        q = q_all[:, hs]
        k = k_all[:, hs]
        qh.append(q * lax.rsqrt(jnp.sum(q * q, axis=-1, keepdims=True) + 1e-6) * (GDN_DK ** -0.5))
        kh.append(k * lax.rsqrt(jnp.sum(k * k, axis=-1, keepdims=True) + 1e-6))
        vh.append(v_all[:, hs])
        b_col = jnp.sum(jnp.where(lane == h, ba, 0.0), axis=-1, keepdims=True)
        a_col = jnp.sum(jnp.where(lane == GDN_HEADS + h, ba, 0.0), axis=-1, keepdims=True)
        betah.append(1.0 / (1.0 + jnp.exp(-b_col)))
        g = -jnp.exp(alog_ref[h]) * _softplus(a_col + dtb_ref[h])
        g_rep = jnp.broadcast_to(g, (blk, LANES))
        hi, mid, lo = _split3(g_rep)
        gcolh.append(jnp.dot(tri_bd, hi, preferred_element_type=F32)
                     + jnp.dot(tri_bd, mid, preferred_element_type=F32)
                     + jnp.dot(tri_bd, lo, preferred_element_type=F32))
        growh.append(_dot3(g_rep[:, :c], upper_bd, _TN))
        egh.append(jnp.exp(gcolh[e][:, 0:1]))

    units = [(e, n) for n in range(nch) for e in range(nhs)]
    rws = lambda n: slice(n * c, (n + 1) * c)
    decays = [jnp.exp(jnp.where(incl, gcolh[e][rws(n), 0:c] - growh[e][:, n * LANES:n * LANES + c], neg_inf))
              for e, n in units]
    kbs = [kh[e][rws(n)].astype(BF16) for e, n in units]
    kks = [lax.dot_general(kb, kb, _NT, preferred_element_type=F32) for kb in kbs]
    qks = [lax.dot_general(qh[e][rws(n)].astype(BF16), kbs[u], _NT, preferred_element_type=F32) * decays[u]
           for u, (e, n) in enumerate(units)]
    mps = [jnp.where(strict, betah[e][rws(n)] * kks[u] * decays[u], 0.0) for u, (e, n) in enumerate(units)]
    ps = [eye - m for m in mps]
    for _ in range(5):
        mps = [_mm_split(_split2(mp), _split2(mp)) for mp in mps]
        mp_parts = [_split2(mp) for mp in mps]
        ps = [p + _mm_split(_split2(p), mpp) for p, mpp in zip(ps, mp_parts)]
    rhs = [jnp.concatenate([betah[e][rws(n)] * vh[e][rws(n)],
                            (betah[e][rws(n)] * egh[e][rws(n)]) * kh[e][rws(n)]], axis=1) for e, n in units]
    ws = [_mm_split(_split2(p), _split2(x)) for p, x in zip(ps, rhs)]
    uvs = [w[:, :GDN_DV].astype(BF16) for w in ws]
    uks = [w[:, GDN_DV:].astype(BF16) for w in ws]
    g_last = [gcolh[e][n * c + c - 1:n * c + c, 0:1] for e, n in units]
    kes = [(kh[e][rws(n)] * jnp.exp(g_last[u] - gcolh[e][rws(n), 0:1])).astype(BF16)
           for u, (e, n) in enumerate(units)]
    qkb = [x.astype(BF16) for x in qks]
    kmats = [lax.dot_general(kes[u], uks[u], _TN, preferred_element_type=F32).astype(BF16)
             for u in range(len(units))]
    hmats = [lax.dot_general(kes[u], uvs[u], _TN, preferred_element_type=F32) for u in range(len(units))]
    qps = [(qh[e][rws(n)] * egh[e][rws(n)] - jnp.dot(qkb[u], uks[u], preferred_element_type=F32)).astype(BF16)
           for u, (e, n) in enumerate(units)]
    ops = [jnp.dot(qkb[u], uvs[u], preferred_element_type=F32) for u in range(len(units))]

    states = [s_ref[e] for e in range(nhs)]
    for u, (e, n) in enumerate(units):
        hs = slice(e * hd, (e + 1) * hd)
        sb = states[e].astype(BF16)
        o = jnp.dot(qps[u], sb, preferred_element_type=F32) + ops[u]
        states[e] = (jnp.exp(g_last[u]) * states[e] - jnp.dot(kmats[u], sb, preferred_element_type=F32)
                     + hmats[u])
        o = o * lax.rsqrt(jnp.mean(o * o, axis=-1, keepdims=True) + RMS_EPS) * ng_ref[...]
        o_ref[rws(n), hs] = o * _silu(z_ref[rws(n), hs])
    for e in range(nhs):
        s_ref[e] = states[e]

    @pl.when(bi == pl.num_programs(2) - 1)
    def _():
        for e in range(nhs):
            s_out_ref[e] = states[e]


def gdn_prompt(proj, ba, conv_w, a_log, dt_bias, norm_g):
    bn, l, _ = proj.shape
    blk = GDN_BLK
    nh = GDN_HEADS
    hd = GDN_DK
    nhs = GDN_HEADS_PER_STEP
    hw = nhs * hd
    ngrp = nh // nhs
    col = lambda part: pl.BlockSpec((None, blk, hw), lambda b, h, i: (b, i, part * ngrp + h))
    cw = lambda part: pl.BlockSpec((CONV_W, hw), lambda b, h, i: (0, part * ngrp + h))
    tail = pltpu.VMEM((8 + blk, hw), F32)
    nch = blk // GDN_CHUNK
    t = np.arange(blk)
    same = (t[:, None] // GDN_CHUNK) == (t[None, :] // GDN_CHUNK)
    tri_bd = jnp.asarray(same & (t[None, :] <= t[:, None]), BF16)
    jj = np.arange(nch * LANES)
    upper = ((t[:, None] // GDN_CHUNK) == (jj[None, :] // LANES)) & \
            ((t[:, None] % GDN_CHUNK) <= (jj[None, :] % LANES)) & ((jj[None, :] % LANES) < GDN_CHUNK)
    upper_bd = jnp.asarray(upper, BF16)
    o, s = pl.pallas_call(
        _gdn_prompt_kernel,
        grid=(bn, ngrp, l // blk),
        in_specs=[pl.BlockSpec(memory_space=pltpu.SMEM), pl.BlockSpec(memory_space=pltpu.SMEM),
                  col(0), col(1), col(2), col(3),
                  pl.BlockSpec((None, blk, LANES), lambda b, h, i: (b, i, 0)),
                  cw(0), cw(1), cw(2),
                  pl.BlockSpec((1, hd), lambda b, h, i: (0, 0)),
                  pl.BlockSpec((blk, blk), lambda b, h, i: (0, 0)),
                  pl.BlockSpec((blk, nch * LANES), lambda b, h, i: (0, 0))],
        out_specs=[pl.BlockSpec((None, blk, hw), lambda b, h, i: (b, i, h)),
                   pl.BlockSpec((None, nhs, hd, hd), lambda b, h, i: (b, h, 0, 0))],
        out_shape=[jax.ShapeDtypeStruct((bn, l, nh * hd), F32),
                   jax.ShapeDtypeStruct((bn, nh, hd, hd), F32)],
        scratch_shapes=[tail, tail, tail, pltpu.VMEM((nhs, hd, hd), F32)],
        compiler_params=_cparams(3),
        name="gdn_prompt",
    )(a_log.astype(F32), dt_bias.astype(F32), proj, proj, proj, proj, ba, conv_w, conv_w, conv_w,
      norm_g.reshape(1, hd), tri_bd, upper_bd)
    return o, s


SUB = 16


def _rows8(row, zero_fill, dtype=F32):
    full = jnp.broadcast_to(row.astype(F32), (SUB, row.shape[1]))
    if zero_fill:
        rid = lax.broadcasted_iota(jnp.int32, full.shape, 0)
        full = jnp.where(rid == 0, full, 0.0)
    return full.astype(dtype)


def _conv_step(x_row, c_ref, cw_ref, cb_row):
    y = x_row * cw_ref[CONV_W - 1:CONV_W, :]
    for i in range(CONV_W - 1):
        y = y + c_ref[i:i + 1, :] * cw_ref[i:i + 1, :]
    if cb_row is not None:
        y = y + cb_row
    return _silu(y)


def _gdn_step_kernel(alog_ref, dtb_ref, proj_ref, ba_ref, c_ref, cw_ref, ng_ref, s_ref, o_ref, s_out_ref):
    hd = GDN_DK
    act = _conv_step(proj_ref[:, :GDN_CONV_DIM], c_ref, cw_ref, None)
    ba = ba_ref[...]
    qs, kbs, us, egs, qks = [], [], [], [], []
    for h in range(GDN_HEADS):
        q = act[:, h * hd:(h + 1) * hd]
        k = act[:, GDN_KD + h * hd:GDN_KD + (h + 1) * hd]
        q = q * lax.rsqrt(jnp.sum(q * q, axis=-1, keepdims=True) + 1e-6) * (GDN_DK ** -0.5)
        k = k * lax.rsqrt(jnp.sum(k * k, axis=-1, keepdims=True) + 1e-6)
        qs.append(q)
        kbs.append(k)
        beta = 1.0 / (1.0 + jnp.exp(-ba[:, h:h + 1]))
        g = -jnp.exp(alog_ref[h]) * _softplus(ba[:, GDN_HEADS + h:GDN_HEADS + h + 1] + dtb_ref[h])
        egs.append(jnp.exp(g))
        us.append(beta)
        qks.append(jnp.sum(q * k, axis=-1, keepdims=True))
    ks_ = [jnp.dot(_rows8(kbs[h], False), s_ref[h], preferred_element_type=F32, precision=_HIGHEST)[0:1]
           for h in range(GDN_HEADS)]
    qs_ = [jnp.dot(_rows8(qs[h], False), s_ref[h], preferred_element_type=F32, precision=_HIGHEST)[0:1]
           for h in range(GDN_HEADS)]
    for h in range(GDN_HEADS):
        v = act[:, 2 * GDN_KD + h * hd:2 * GDN_KD + (h + 1) * hd]
        u = us[h] * (v - egs[h] * ks_[h])
        o = egs[h] * qs_[h] + qks[h] * u
        outer = lax.dot_general(_rows8(kbs[h], True), _rows8(u, False), _TN,
                                preferred_element_type=F32, precision=_HIGHEST)
        s_out_ref[h] = egs[h] * s_ref[h] + outer
        o = o * lax.rsqrt(jnp.mean(o * o, axis=-1, keepdims=True) + RMS_EPS) * ng_ref[...]
        z = proj_ref[:, GDN_CONV_DIM + h * hd:GDN_CONV_DIM + (h + 1) * hd]
        o_ref[:, h * hd:(h + 1) * hd] = o * _silu(z)


def gdn_step(proj, ba, s0, conv0, conv_w, a_log, dt_bias, norm_g):
    bn, pw = proj.shape
    hd = GDN_DK
    o, s = pl.pallas_call(
        _gdn_step_kernel,
        grid=(bn,),
        in_specs=[pl.BlockSpec(memory_space=pltpu.SMEM), pl.BlockSpec(memory_space=pltpu.SMEM),
                  pl.BlockSpec((None, 1, pw), lambda b: (b, 0, 0)),
                  pl.BlockSpec((None, 1, LANES), lambda b: (b, 0, 0)),
                  pl.BlockSpec((None, CONV_W - 1, GDN_CONV_DIM), lambda b: (b, 0, 0)),
                  pl.BlockSpec((CONV_W, GDN_CONV_DIM), lambda b: (0, 0)),
                  pl.BlockSpec((1, hd), lambda b: (0, 0)),
                  pl.BlockSpec((None, GDN_HEADS, hd, hd), lambda b: (b, 0, 0, 0))],
        out_specs=[pl.BlockSpec((None, 1, GDN_VD), lambda b: (b, 0, 0)),
                   pl.BlockSpec((None, GDN_HEADS, hd, hd), lambda b: (b, 0, 0, 0))],
        out_shape=[jax.ShapeDtypeStruct((bn, 1, GDN_VD), F32),
                   jax.ShapeDtypeStruct(s0.shape, F32)],
        compiler_params=_cparams(1),
        name="gdn_step",
    )(a_log.astype(F32), dt_bias.astype(F32), proj.reshape(bn, 1, pw), ba.reshape(bn, 1, LANES),
      conv0, conv_w, norm_g.reshape(1, hd), s0)
    return o.reshape(bn, GDN_VD), s


def _ssd_step_kernel(proj_ref, dt_ref, c_ref, cw_ref, cb_ref, dtb_ref, aneg_ref, dskip_ref, ng_ref, e_ref,
                     h_ref, y_ref, h_out_ref):
    di = SSM_D_INNER
    act = _conv_step(proj_ref[:, di:di + SSM_CONV_DIM], c_ref, cw_ref, cb_ref[...])
    xs = act[:, :di]
    dt = _softplus(dt_ref[...] + dtb_ref[...])
    ea = jnp.exp(dt * aneg_ref[...])
    ex = _dot3(jnp.concatenate([_rows8(dt, True), _rows8(ea, True)], axis=0), e_ref[...])
    dt_x, ea_x = ex[0:1], ex[SUB:SUB + 1]
    xdt = xs * dt_x
    ones = jnp.ones((SUB, SSM_STATE), BF16)
    for g in range(SSM_GROUPS):
        gs = slice(g * SSM_GROUP_W, (g + 1) * SSM_GROUP_W)
        bg = act[:, di + g * SSM_STATE:di + (g + 1) * SSM_STATE]
        cg = act[:, di + SSM_GN + g * SSM_STATE:di + SSM_GN + (g + 1) * SSM_STATE]
        dstate = lax.dot_general(_rows8(xdt[:, gs], True), _rows8(bg, False), _TN,
                                 preferred_element_type=F32, precision=_HIGHEST)
        decay = _dot3(_rows8(ea_x[:, gs], True), ones, _TN)
        h_old = h_ref[g * SSM_HPG:(g + 1) * SSM_HPG].reshape(SSM_GROUP_W, SSM_STATE)
        h_new = decay * h_old + dstate
        h_out_ref[g * SSM_HPG:(g + 1) * SSM_HPG] = h_new.reshape(SSM_HPG, SSM_HEAD_DIM, SSM_STATE)
        y = lax.dot_general(_rows8(cg, False), h_new, _NT, preferred_element_type=F32,
                            precision=_HIGHEST)[0:1]
        y = y + dskip_ref[:, gs] * xs[:, gs]
        y = y * _silu(proj_ref[:, gs])
        y = y * lax.rsqrt(jnp.mean(y * y, axis=-1, keepdims=True) + RMS_EPS) * ng_ref[:, gs]
        y_ref[:, gs] = y


def ssd_step(proj, dtp, h0, conv0, conv_w, conv_b, a_log, dt_bias, d_skip, norm_g):
    bn, pw = proj.shape
    di = SSM_D_INNER
    full = lambda shape: pl.BlockSpec(shape, lambda b: (0,) * len(shape))
    hblk = pl.BlockSpec((None, SSM_HEADS, SSM_HEAD_DIM, SSM_STATE), lambda b: (b, 0, 0, 0))
    y, h = pl.pallas_call(
        _ssd_step_kernel,
        grid=(bn,),
        in_specs=[pl.BlockSpec((None, 1, pw), lambda b: (b, 0, 0)),
                  pl.BlockSpec((None, 1, LANES), lambda b: (b, 0, 0)),
                  pl.BlockSpec((None, CONV_W - 1, SSM_CONV_DIM), lambda b: (b, 0, 0)),
                  full((CONV_W, SSM_CONV_DIM)), full((1, SSM_CONV_DIM)),
                  full((1, LANES)), full((1, LANES)), full((1, di)), full((1, di)), full((LANES, di)),
                  hblk],
        out_specs=[pl.BlockSpec((None, 1, di), lambda b: (b, 0, 0)), hblk],
        out_shape=[jax.ShapeDtypeStruct((bn, 1, di), F32), jax.ShapeDtypeStruct(h0.shape, F32)],
        compiler_params=_cparams(1),
        name="ssd_step",
    )(proj.reshape(bn, 1, pw), dtp.reshape(bn, 1, LANES), conv0, conv_w, conv_b.reshape(1, -1),
      _pad_lanes(dt_bias), _pad_lanes(-jnp.exp(a_log)), jnp.repeat(d_skip, SSM_HEAD_DIM).reshape(1, di),
      norm_g.reshape(1, di), _head_expander(), h0)
    return y.reshape(bn, di), h


def _pad_cols(w):
    return jnp.pad(w, ((0, 0), (0, LANES - w.shape[1])))


def kernel(x_prompt, x_sample, state_gdn_S, state_gdn_conv, state_ssm_h, state_ssm_conv,
           cache_attn_k, cache_attn_v, page_table,
           gdn_w_in, gdn_conv_w, gdn_a_log, gdn_dt_bias, gdn_norm_g, gdn_w_out,
           ssm_w_in, ssm_conv_w, ssm_conv_b, ssm_a_log, ssm_dt_bias, ssm_d_skip, ssm_norm_g, ssm_w_out,
           att_w_qkv, att_lam_q1, att_lam_k1, att_lam_q2, att_lam_k2, att_subln_g, att_w_out, rel_bias,
           ffn_w_gu, ffn_w_down, moe_w_router, moe_w_gu, moe_w_down, ln_g, ln_b):
    bp, lp, d = x_prompt.shape
    bs, ls, _ = x_sample.shape
    assert ls == 1, "the sample group advances one token per sequence"
    tp = bp * lp
    ts = bs * ls
    xp = x_prompt.reshape(tp, d)
    xs = x_sample.reshape(ts, d)
    moe_gu = moe_w_gu.reshape((-1,) + moe_w_gu.shape[2:])
    moe_down = moe_w_down.reshape((-1,) + moe_w_down.shape[2:])

    tm_p = 1024
    tm_s = ts

    outs = {name: [] for name in ("gS_p", "gS_s", "gc_p", "gc_s", "sh_p", "sh_s", "sc_p", "sc_s",
                                  "ak_p", "ak_s", "av_p", "av_s")}
    for i in range(DEPTH):
        j = i // N_MIXERS
        if i % N_MIXERS == 0:
            n_main = GDN_CONV_DIM + GDN_VD
            w_ba = _pad_cols(gdn_w_in[j][:, n_main:])
            prm = (gdn_conv_w[j], gdn_a_log[j], gdn_dt_bias[j], gdn_norm_g[j])
            proj_p = matmul_cols(xp, gdn_w_in[j], n_main, tm_p, 512).reshape(bp, lp, n_main)
            ba_p = matmul_narrow(xp, w_ba, tm_p).reshape(bp, lp, LANES)
            op, s_p = gdn_prompt(proj_p, ba_p, *prm)
            c_p = proj_p[:, lp - (CONV_W - 1):, :GDN_CONV_DIM]
            proj_s = matmul_cols(xs, gdn_w_in[j], n_main, tm_s, 512, precise=True)
            ba_s = matmul_narrow(xs, w_ba, tm_s)
            os_, s_s = gdn_step(proj_s, ba_s, state_gdn_S[j], state_gdn_conv[j], *prm)
            c_s = jnp.concatenate([state_gdn_conv[j][:, 1:], proj_s[:, None, :GDN_CONV_DIM]], axis=1)
            outs["gS_p"].append(s_p); outs["gS_s"].append(s_s)
            outs["gc_p"].append(c_p); outs["gc_s"].append(c_s)
            w_out = gdn_w_out[j]
        elif i % N_MIXERS == 1:
            n_main = SSM_D_INNER + SSM_CONV_DIM
            w_dt = _pad_cols(ssm_w_in[j][:, n_main:])
            prm = (ssm_conv_w[j], ssm_conv_b[j], ssm_a_log[j], ssm_dt_bias[j], ssm_d_skip[j], ssm_norm_g[j])
            proj_p = matmul_cols(xp, ssm_w_in[j], n_main, tm_p, 512).reshape(bp, lp, n_main)
            dt_p = matmul_narrow(xp, w_dt, tm_p).reshape(bp, lp, LANES)
            op, h_p = ssd_prompt(proj_p, dt_p, *prm)
            c_p = proj_p[:, lp - (CONV_W - 1):, SSM_D_INNER:]
            proj_s = matmul_cols(xs, ssm_w_in[j], n_main, tm_s, 512, precise=True)
            dt_s = matmul_narrow(xs, w_dt, tm_s)
            os_, h_s = ssd_step(proj_s, dt_s, state_ssm_h[j], state_ssm_conv[j], *prm)
            c_s = jnp.concatenate([state_ssm_conv[j][:, 1:], proj_s[:, None, SSM_D_INNER:]], axis=1)
            outs["sh_p"].append(h_p); outs["sh_s"].append(h_s)
            outs["sc_p"].append(c_p); outs["sc_s"].append(c_s)
            w_out = ssm_w_out[j]
        else:
            lam_init = 0.8 - 0.6 * math.exp(-0.3 * i)
            lam = (jnp.exp(jnp.sum(att_lam_q1[j] * att_lam_k1[j]))
                   - jnp.exp(jnp.sum(att_lam_q2[j] * att_lam_k2[j])) + lam_init)
            qkv_p = matmul_cols(xp, att_w_qkv[j], 3 * d, tm_p, 512).reshape(bp, lp, 3 * d)
            op = attn_prompt(qkv_p, lam, lam_init, attn_bias_tiles(rel_bias), att_subln_g[j])
            k_p = qkv_p[..., d:2 * d].reshape(bp, lp, ATT_HEADS, 2 * ATT_HEAD_DIM)
            v_p = qkv_p[..., 2 * d:].reshape(bp, lp, ATT_HEADS, 2 * ATT_HEAD_DIM)
            qkv_s = matmul_cols(xs, att_w_qkv[j], 3 * d, tm_s, 512, precise=True).reshape(bs, ls, 3 * d)
            os_ = attn_decode(qkv_s.reshape(ts, 3 * d), cache_attn_k, cache_attn_v, j, page_table,
                              lam, lam_init, rel_bias, att_subln_g[j])
            k_s = qkv_s[..., d:2 * d].reshape(bs, ls, ATT_HEADS, 2 * ATT_HEAD_DIM)
            v_s = qkv_s[..., 2 * d:].reshape(bs, ls, ATT_HEADS, 2 * ATT_HEAD_DIM)
            outs["ak_p"].append(k_p); outs["ak_s"].append(k_s)
            outs["av_p"].append(v_p); outs["av_s"].append(v_s)
            w_out = att_w_out[j]
        xp = proj_residual_ln(op.reshape(tp, -1), w_out, xp, ln_g[i, 0], ln_b[i, 0], 512)
        xs = proj_residual_ln(os_.reshape(ts, -1), w_out, xs, ln_g[i, 0], ln_b[i, 0], tm_s, precise=True)
        f = i // 2
        if i % 2 == 0:
            xp = dense_ffn_ln(xp, ffn_w_gu, ffn_w_down, f, ln_g[i, 1], ln_b[i, 1], tm_p)
            xs = dense_ffn_ln(xs, ffn_w_gu, ffn_w_down, f, ln_g[i, 1], ln_b[i, 1], tm_s, precise=True)
        else:
            xp = moe_ffn_ln(xp, moe_w_router[f], moe_gu, moe_down, f, ln_g[i, 1], ln_b[i, 1],
                            512, 1024, 256)
            xs = moe_ffn_ln(xs, moe_w_router[f], moe_gu, moe_down, f, ln_g[i, 1], ln_b[i, 1],
                            ts, ts, ts, precise=True)
    st = lambda name: jnp.stack(outs[name])
    return (xp.reshape(bp, lp, d), xs.reshape(bs, ls, d),
            st("gS_p"), st("gS_s"), st("gc_p"), st("gc_s"),
            st("sh_p"), st("sh_s"), st("sc_p"), st("sc_s"),
            st("ak_p"), st("ak_s"), st("av_p"), st("av_s"))
```

```python
import functools
import math

import numpy as np
import jax
import jax.numpy as jnp
from jax import lax
from jax.experimental import pallas as pl
from jax.experimental.pallas import tpu as pltpu

F32 = jnp.float32
BF16 = jnp.bfloat16

D_MODEL = 1024
DEPTH = 4
PAGE_SIZE = 128
N_MIXERS = 3
CONV_W = 4

GDN_HEADS = 8
GDN_DK = 128
GDN_DV = 128
GDN_KD = GDN_HEADS * GDN_DK
GDN_VD = GDN_HEADS * GDN_DV
GDN_CONV_DIM = 2 * GDN_KD + GDN_VD
GDN_CHUNK = 64

SSM_D_INNER = 2 * D_MODEL
SSM_HEAD_DIM = 64
SSM_HEADS = SSM_D_INNER // SSM_HEAD_DIM
SSM_GROUPS = 4
SSM_HPG = SSM_HEADS // SSM_GROUPS
SSM_STATE = 128
SSM_CONV_DIM = SSM_D_INNER + 2 * SSM_GROUPS * SSM_STATE
SSM_CHUNK = 128

ATT_HEADS = 8
ATT_HEAD_DIM = D_MODEL // (2 * ATT_HEADS)
N_BUCKETS = 32
MAX_DISTANCE = 128
Q_BLOCK = 128

D_FF = 3584
N_EXPERTS = 8
TOP_K = 2

DEEPNORM_ALPHA = (2.0 * DEPTH) ** 0.25
LN_EPS = 1e-5
RMS_EPS = 1e-6

V7X_VMEM_BYTES = 64 * 1024 * 1024
VMEM_LIMIT_BYTES = V7X_VMEM_BYTES - 8 * 1024 * 1024
LANES = 128

FF_CHUNK = 512
N_FF_CHUNKS = D_FF // FF_CHUNK


def _cparams(n_axes, flags=None):
    return pltpu.CompilerParams(dimension_semantics=("arbitrary",) * n_axes,
                                vmem_limit_bytes=VMEM_LIMIT_BYTES, flags=flags)


def _layer_norm_rows(x, g, b):
    mu = jnp.mean(x, axis=-1, keepdims=True)
    xc = x - mu
    var = jnp.mean(xc * xc, axis=-1, keepdims=True)
    return xc * lax.rsqrt(var + LN_EPS) * g + b


def _silu(x):
    return x * (1.0 / (1.0 + jnp.exp(-x)))


_HIGHEST = lax.Precision.HIGHEST


def _matmul_kernel(x_ref, w_ref, o_ref, xb_ref, *, precise):
    if precise:
        o_ref[...] = jnp.dot(x_ref[...], w_ref[...], preferred_element_type=F32, precision=_HIGHEST)
        return

    @pl.when(pl.program_id(1) == 0)
    def _():
        xb_ref[...] = x_ref[...].astype(BF16)

    o_ref[...] = jnp.dot(xb_ref[...], w_ref[...].astype(BF16), preferred_element_type=F32)


def matmul_cols(x, w, n_cols, tm, tn, precise=False):
    m, k = x.shape
    return pl.pallas_call(
        functools.partial(_matmul_kernel, precise=precise),
        grid=(m // tm, n_cols // tn),
        in_specs=[pl.BlockSpec((tm, k), lambda i, j: (i, 0)),
                  pl.BlockSpec((k, tn), lambda i, j: (0, j))],
        out_specs=pl.BlockSpec((tm, tn), lambda i, j: (i, j)),
        out_shape=jax.ShapeDtypeStruct((m, n_cols), F32),
        scratch_shapes=[pltpu.VMEM((tm, k), BF16)],
        compiler_params=_cparams(2),
        name="matmul_cols",
    )(x, w)


def _matmul_narrow_kernel(x_ref, w_ref, o_ref):
    o_ref[...] = jnp.dot(x_ref[...], w_ref[...], preferred_element_type=F32, precision=_HIGHEST)


def matmul_narrow(x, w_pad, tm):
    m, k = x.shape
    return pl.pallas_call(
        _matmul_narrow_kernel,
        grid=(m // tm,),
        in_specs=[pl.BlockSpec((tm, k), lambda i: (i, 0)),
                  pl.BlockSpec((k, LANES), lambda i: (0, 0))],
        out_specs=pl.BlockSpec((tm, LANES), lambda i: (i, 0)),
        out_shape=jax.ShapeDtypeStruct((m, LANES), F32),
        compiler_params=_cparams(1),
        name="matmul_narrow",
    )(x, w_pad)


def _proj_ln_kernel(o_ref, w_ref, r_ref, g_ref, b_ref, y_ref, wb_ref, *, precise):
    if precise:
        m = jnp.dot(o_ref[...], w_ref[...], preferred_element_type=F32, precision=_HIGHEST)
    else:
        @pl.when(pl.program_id(0) == 0)
        def _():
            wb_ref[...] = w_ref[...].astype(BF16)

        m = jnp.dot(o_ref[...].astype(BF16), wb_ref[...], preferred_element_type=F32)
    y_ref[...] = _layer_norm_rows(DEEPNORM_ALPHA * r_ref[...] + m, g_ref[...], b_ref[...])


def proj_residual_ln(o, w, resid, g, b, tm, precise=False):
    m, k = o.shape
    d = w.shape[1]
    return pl.pallas_call(
        functools.partial(_proj_ln_kernel, precise=precise),
        grid=(m // tm,),
        in_specs=[pl.BlockSpec((tm, k), lambda i: (i, 0)),
                  pl.BlockSpec((k, d), lambda i: (0, 0)),
                  pl.BlockSpec((tm, d), lambda i: (i, 0)),
                  pl.BlockSpec((1, d), lambda i: (0, 0)),
                  pl.BlockSpec((1, d), lambda i: (0, 0))],
        out_specs=pl.BlockSpec((tm, d), lambda i: (i, 0)),
        out_shape=jax.ShapeDtypeStruct((m, d), F32),
        scratch_shapes=[pltpu.VMEM((k, d), BF16)],
        compiler_params=_cparams(1),
        name="proj_residual_ln",
    )(o, w, resid, g.reshape(1, d), b.reshape(1, d))


def _ffn_kernel(te_ref, na_ref, x_ref, wg_ref, wu_ref, wd_ref, g_ref, b_ref, o_ref,
                xb_ref, acc_ref, *, fuse_ln, precise):
    i = pl.program_id(0)
    j = pl.program_id(1)
    active = i < na_ref[0]

    @pl.when(jnp.logical_and(active, j == 0))
    def _():
        if not precise:
            xb_ref[...] = x_ref[...].astype(BF16)
        acc_ref[...] = jnp.zeros(acc_ref.shape, F32)

    @pl.when(active)
    def _():
        if precise:
            x = x_ref[...]
            h = jnp.dot(x, wg_ref[...], preferred_element_type=F32, precision=_HIGHEST)
            u = jnp.dot(x, wu_ref[...], preferred_element_type=F32, precision=_HIGHEST)
            acc_ref[...] += jnp.dot(_silu(h) * u, wd_ref[...], preferred_element_type=F32,
                                    precision=_HIGHEST)
        else:
            xb = xb_ref[...]
            h = jnp.dot(xb, wg_ref[...].astype(BF16), preferred_element_type=F32)
            u = jnp.dot(xb, wu_ref[...].astype(BF16), preferred_element_type=F32)
            a = (_silu(h) * u).astype(BF16)
            acc_ref[...] += jnp.dot(a, wd_ref[...].astype(BF16), preferred_element_type=F32)

    @pl.when(jnp.logical_and(active, j == N_FF_CHUNKS - 1))
    def _():
        if fuse_ln:
            o_ref[...] = _layer_norm_rows(DEEPNORM_ALPHA * x_ref[...] + acc_ref[...],
                                          g_ref[...], b_ref[...])
        else:
            o_ref[...] = acc_ref[...]

    @pl.when(jnp.logical_and(jnp.logical_not(active), j == N_FF_CHUNKS - 1))
    def _():
        o_ref[...] = jnp.zeros_like(o_ref)


def chunk_major_bf16(w_gu, w_down):
    lead = w_gu.shape[:-2]
    d = w_gu.shape[-2]
    n = len(lead)
    wg = w_gu.reshape(lead + (d, 2, N_FF_CHUNKS, FF_CHUNK))
    wg = jnp.transpose(wg, tuple(range(n)) + (n + 2, n + 1, n, n + 3)).astype(BF16)
    wd = w_down.reshape(lead + (N_FF_CHUNKS, FF_CHUNK, d)).astype(BF16)
    return wg, wd


def swiglu_tiles(x, w_gu, w_down, tile_expert, n_active, ln_g, ln_b, tm, fuse_ln, precise=False):
    m, d = x.shape
    n_tiles = m // tm

    def row_map(i, j, te, na):
        return (jnp.minimum(i, na[0] - 1), 0)

    def expert(i, te, na):
        return te[jnp.minimum(i, na[0] - 1)]

    if precise:
        w_specs = [pl.BlockSpec((None, d, FF_CHUNK), lambda i, j, te, na: (expert(i, te, na), 0, j)),
                   pl.BlockSpec((None, d, FF_CHUNK),
                                lambda i, j, te, na: (expert(i, te, na), 0, j + N_FF_CHUNKS)),
                   pl.BlockSpec((None, FF_CHUNK, d), lambda i, j, te, na: (expert(i, te, na), j, 0))]
    else:
        w_specs = [pl.BlockSpec((None, None, None, d, FF_CHUNK),
                                lambda i, j, te, na: (expert(i, te, na), j, 0, 0, 0)),
                   pl.BlockSpec((None, None, None, d, FF_CHUNK),
                                lambda i, j, te, na: (expert(i, te, na), j, 1, 0, 0)),
                   pl.BlockSpec((None, None, FF_CHUNK, d),
                                lambda i, j, te, na: (expert(i, te, na), j, 0, 0))]

    grid_spec = pltpu.PrefetchScalarGridSpec(
        num_scalar_prefetch=2,
        grid=(n_tiles, N_FF_CHUNKS),
        in_specs=[pl.BlockSpec((tm, d), row_map)] + w_specs + [
                  pl.BlockSpec((1, d), lambda i, j, te, na: (0, 0)),
                  pl.BlockSpec((1, d), lambda i, j, te, na: (0, 0))],
        out_specs=pl.BlockSpec((tm, d), lambda i, j, te, na: (i, 0)),
        scratch_shapes=[pltpu.VMEM((tm, d), BF16), pltpu.VMEM((tm, d), F32)],
    )
    return pl.pallas_call(
        functools.partial(_ffn_kernel, fuse_ln=fuse_ln, precise=precise),
        grid_spec=grid_spec,
        out_shape=jax.ShapeDtypeStruct((m, d), F32),
        compiler_params=_cparams(2),
        name="swiglu_ln" if fuse_ln else "swiglu_grouped",
    )(tile_expert, n_active, x, w_gu, w_gu, w_down, ln_g.reshape(1, d), ln_b.reshape(1, d))


def dense_ffn_ln(x, w_gu_all, w_down_all, layer, ln_g, ln_b, tm, precise=False):
    n_tiles = x.shape[0] // tm
    te = jnp.full((n_tiles,), layer, jnp.int32)
    na = jnp.full((1,), n_tiles, jnp.int32)
    return swiglu_tiles(x, w_gu_all, w_down_all, te, na, ln_g, ln_b, tm, True, precise)


def _router_kernel(x_ref, w_ref, meta_ref, cnt_ref, carry_ref, *, tm):
    i = pl.program_id(0)

    @pl.when(i == 0)
    def _():
        carry_ref[...] = jnp.zeros_like(carry_ref)

    logits = jnp.dot(x_ref[...], w_ref[...], preferred_element_type=F32, precision=_HIGHEST)
    lane = lax.broadcasted_iota(jnp.int32, (tm, LANES), 1)
    neg = jnp.float32(-jnp.inf)
    logits = jnp.where(lane < N_EXPERTS, logits, neg)
    m1 = jnp.max(logits, axis=-1, keepdims=True)
    i1 = jnp.min(jnp.where(logits == m1, lane, LANES), axis=-1, keepdims=True)
    rest = jnp.where(lane == i1, neg, logits)
    m2 = jnp.max(rest, axis=-1, keepdims=True)
    i2 = jnp.min(jnp.where(rest == m2, lane, LANES), axis=-1, keepdims=True)
    e = jnp.exp(m2 - m1)
    g1 = 1.0 / (1.0 + e)
    g2 = e / (1.0 + e)

    chosen = jnp.logical_or(lane == i1, lane == i2)
    onehot = jnp.where(chosen, 1.0, 0.0).astype(BF16)
    row = lax.broadcasted_iota(jnp.int32, (tm, tm), 0)
    col = lax.broadcasted_iota(jnp.int32, (tm, tm), 1)
    strict_lower = jnp.where(col < row, 1.0, 0.0).astype(BF16)
    before = jnp.dot(strict_lower, onehot, preferred_element_type=F32) + carry_ref[...]
    r1 = jnp.sum(jnp.where(lane == i1, before, 0.0), axis=-1, keepdims=True)
    r2 = jnp.sum(jnp.where(lane == i2, before, 0.0), axis=-1, keepdims=True)
    carry_ref[...] += jnp.sum(onehot.astype(F32), axis=0, keepdims=True)

    meta = jnp.where(lane == 0, i1.astype(F32),
           jnp.where(lane == 1, i2.astype(F32),
           jnp.where(lane == 2, g1,
           jnp.where(lane == 3, g2,
           jnp.where(lane == 4, r1,
           jnp.where(lane == 5, r2, 0.0))))))
    meta_ref[...] = meta
    cnt_ref[...] = carry_ref[...]


def route_top2(x, w_router, tm):
    t, d = x.shape
    w_pad = jnp.pad(w_router, ((0, 0), (0, LANES - N_EXPERTS)))
    return pl.pallas_call(
        functools.partial(_router_kernel, tm=tm),
        grid=(t // tm,),
        in_specs=[pl.BlockSpec((tm, d), lambda i: (i, 0)),
                  pl.BlockSpec((d, LANES), lambda i: (0, 0))],
        out_specs=[pl.BlockSpec((tm, LANES), lambda i: (i, 0)),
                   pl.BlockSpec((1, LANES), lambda i: (0, 0))],
        out_shape=[jax.ShapeDtypeStruct((t, LANES), F32),
                   jax.ShapeDtypeStruct((1, LANES), F32)],
        scratch_shapes=[pltpu.VMEM((1, LANES), F32)],
        compiler_params=_cparams(1),
        name="route_top2",
    )(x, w_pad)


def _dispatch_kernel(pos_ref, x_ref, xs_in_hbm, xs_hbm, sem, *, tokens_per_step):
    del xs_in_hbm

    def row_copy(t, k):
        return pltpu.make_async_copy(x_ref.at[pl.ds(t, 1)],
                                     xs_hbm.at[pl.ds(pos_ref[0, 2 * t + k], 1)], sem)

    def issue(t, carry):
        row_copy(t, 0).start()
        row_copy(t, 1).start()
        return carry

    lax.fori_loop(0, tokens_per_step, issue, 0, unroll=8)

    def drain(t, carry):
        row_copy(t, 0).wait()
        row_copy(t, 1).wait()
        return carry

    lax.fori_loop(0, tokens_per_step, drain, 0, unroll=8)


def dispatch_rows(x, pos, n_rows, tokens_per_step):
    t, d = x.shape
    n_steps = t // tokens_per_step
    pos2 = pos.reshape(n_steps, 1, 2 * tokens_per_step)
    zeros = jnp.zeros((n_rows, d), x.dtype)
    return pl.pallas_call(
        functools.partial(_dispatch_kernel, tokens_per_step=tokens_per_step),
        grid=(n_steps,),
        in_specs=[pl.BlockSpec((None, 1, 2 * tokens_per_step), lambda i: (i, 0, 0),
                               memory_space=pltpu.SMEM),
                  pl.BlockSpec((tokens_per_step, d), lambda i: (i, 0)),
                  pl.BlockSpec(memory_space=pl.ANY)],
        out_specs=pl.BlockSpec(memory_space=pl.ANY),
        out_shape=jax.ShapeDtypeStruct((n_rows, d), x.dtype),
        scratch_shapes=[pltpu.SemaphoreType.DMA(())],
        input_output_aliases={2: 0},
        compiler_params=_cparams(1),
        name="dispatch_rows",
    )(pos2, x, zeros)


def _combine_ln_kernel(pos_ref, x_ref, gate_ref, g_ref, b_ref, ys_hbm, o_ref, buf_ref, sem, *, tm):
    def row_copy(t, k):
        return pltpu.make_async_copy(ys_hbm.at[pl.ds(pos_ref[0, 2 * t + k], 1)],
                                     buf_ref.at[k, pl.ds(t, 1)], sem)

    def issue(t, carry):
        row_copy(t, 0).start()
        row_copy(t, 1).start()
        return carry

    lax.fori_loop(0, tm, issue, 0, unroll=8)

    def drain(t, carry):
        row_copy(t, 0).wait()
        row_copy(t, 1).wait()
        return carry

    lax.fori_loop(0, tm, drain, 0, unroll=8)

    gates = gate_ref[...]
    y = gates[:, 2:3] * buf_ref[0] + gates[:, 3:4] * buf_ref[1]
    o_ref[...] = _layer_norm_rows(DEEPNORM_ALPHA * x_ref[...] + y, g_ref[...], b_ref[...])


def combine_ln(x, ys, pos, meta, ln_g, ln_b, tm):
    t, d = x.shape
    n_steps = t // tm
    pos2 = pos.reshape(n_steps, 1, 2 * tm)
    return pl.pallas_call(
        functools.partial(_combine_ln_kernel, tm=tm),
        grid=(n_steps,),
        in_specs=[pl.BlockSpec((None, 1, 2 * tm), lambda i: (i, 0, 0), memory_space=pltpu.SMEM),
                  pl.BlockSpec((tm, d), lambda i: (i, 0)),
                  pl.BlockSpec((tm, LANES), lambda i: (i, 0)),
                  pl.BlockSpec((1, d), lambda i: (0, 0)),
                  pl.BlockSpec((1, d), lambda i: (0, 0)),
                  pl.BlockSpec(memory_space=pl.ANY)],
        out_specs=pl.BlockSpec((tm, d), lambda i: (i, 0)),
        out_shape=jax.ShapeDtypeStruct((t, d), F32),
        scratch_shapes=[pltpu.VMEM((2, tm, d), F32), pltpu.SemaphoreType.DMA(())],
        compiler_params=_cparams(1),
        name="combine_ln",
    )(pos2, x, meta, ln_g.reshape(1, d), ln_b.reshape(1, d), ys)


def moe_ffn_ln(x, w_router, w_gu_all, w_down_all, layer, ln_g, ln_b, tm_route, tm_group, tm_combine,
               precise=False):
    t, d = x.shape
    meta, counts = route_top2(x, w_router, tm_route)
    ids = meta[:, 0:2].astype(jnp.int32)
    ranks = meta[:, 4:6].astype(jnp.int32)
    cnt = counts[0, :N_EXPERTS].astype(jnp.int32)
    tiles_per = (cnt + tm_group - 1) // tm_group
    tile_end = jnp.cumsum(tiles_per)
    group_start = (tile_end - tiles_per) * tm_group
    pos = group_start[ids] + ranks
    n_tiles = min((TOP_K * t + N_EXPERTS * (tm_group - 1)) // tm_group,
                  N_EXPERTS * ((t + tm_group - 1) // tm_group))
    n_active = tile_end[N_EXPERTS - 1:N_EXPERTS]
    tile_ids = jnp.arange(n_tiles, dtype=jnp.int32)
    tile_expert = jnp.sum((tile_ids[:, None] >= tile_end[None, :]).astype(jnp.int32), axis=1)
    tile_expert = jnp.minimum(tile_expert, N_EXPERTS - 1) + layer * N_EXPERTS
    xs = dispatch_rows(x, pos, n_tiles * tm_group, min(t, 1024))
    ys = swiglu_tiles(xs, w_gu_all, w_down_all, tile_expert.astype(jnp.int32),
                      n_active.astype(jnp.int32), ln_g, ln_b, tm_group, False, precise)
    return combine_ln(x, ys, pos, meta, ln_g, ln_b, tm_combine)


def _t5_bucket_runs():
    max_exact = N_BUCKETS // 2
    n = np.arange(MAX_DISTANCE)
    scaled = (np.log(np.maximum(n, max_exact).astype(np.float64) / max_exact)
              / math.log(MAX_DISTANCE / max_exact) * (N_BUCKETS - max_exact))
    frac = scaled - np.floor(scaled)
    assert np.all((np.minimum(frac, 1 - frac) > 1e-3) | (n <= max_exact))
    bucket = np.where(n < max_exact, n, np.minimum(max_exact + scaled.astype(np.int64), N_BUCKETS - 1))
    runs = []
    for dist in range(MAX_DISTANCE):
        if runs and runs[-1][1] == int(bucket[dist]):
            runs[-1] = (dist, int(bucket[dist]))
        else:
            runs.append((dist, int(bucket[dist])))
    return runs, bucket


T5_RUNS, T5_BUCKETS = _t5_bucket_runs()
ATT_TQ = 512
ATT_ROW_BLOCK = 512


def _bias_tile_kernel(rel_ref, o_ref):
    h = pl.program_id(0)
    tq, tk2 = o_ref.shape
    row = lax.broadcasted_iota(jnp.int32, (tq, tk2), 0)
    col = lax.broadcasted_iota(jnp.int32, (tq, tk2), 1)
    dist = row - col + tq
    far = rel_ref[N_BUCKETS - 1, h]
    acc = jnp.zeros((tq, tk2), F32)
    for last, bucket in reversed(T5_RUNS):
        acc = jnp.where(dist <= last, rel_ref[bucket, h] - far, acc)
    o_ref[...] = jnp.where(dist < 0, -jnp.inf, acc)


def attn_bias_tiles(rel_bias):
    return pl.pallas_call(
        _bias_tile_kernel,
        grid=(ATT_HEADS,),
        in_specs=[pl.BlockSpec(memory_space=pltpu.SMEM)],
        out_specs=pl.BlockSpec((None, ATT_TQ, 2 * ATT_TQ), lambda h: (h, 0, 0)),
        out_shape=jax.ShapeDtypeStruct((ATT_HEADS, ATT_TQ, 2 * ATT_TQ), F32),
        compiler_params=_cparams(1),
        name="attn_bias_tiles",
    )(rel_bias)


def _softmax_step(s, v, m_ref, l_ref, acc_ref):
    m_prev = m_ref[...]
    m_new = jnp.maximum(m_prev, jnp.max(s, axis=-1, keepdims=True))
    alpha = jnp.exp(m_prev - m_new)
    p = jnp.exp(s - jnp.concatenate([m_new] * (s.shape[1] // LANES), axis=1))
    l_ref[...] = alpha * l_ref[...] + jnp.sum(p, axis=-1, keepdims=True)
    acc_ref[...] = alpha * acc_ref[...] + jnp.dot(p.astype(BF16), v, preferred_element_type=F32)
    m_ref[...] = m_new


def _attn_prompt_kernel(lam_ref, q_ref, k_ref, v_ref, bt_ref, g_ref, o_ref,
                        kb_ref, vb_ref, m1_ref, l1_ref, a1_ref, m2_ref, l2_ref, a2_ref, *, out_scale):
    qi = pl.program_id(2)
    tq = q_ref.shape[0]
    half = ATT_HEAD_DIM

    @pl.when(qi == 0)
    def _():
        kb_ref[...] = k_ref[...].astype(BF16)
        vb_ref[...] = v_ref[...].astype(BF16)

    q = q_ref[...] * (ATT_HEAD_DIM ** -0.5)
    lane = lax.broadcasted_iota(jnp.int32, q.shape, 1)
    q1 = jnp.where(lane < half, q, 0.0).astype(BF16)
    q2 = jnp.where(lane >= half, q, 0.0).astype(BF16)

    for m_ref, l_ref, a_ref in ((m1_ref, l1_ref, a1_ref), (m2_ref, l2_ref, a2_ref)):
        m_ref[...] = jnp.full(m_ref.shape, -jnp.inf, F32)
        l_ref[...] = jnp.zeros(l_ref.shape, F32)
        a_ref[...] = jnp.zeros(a_ref.shape, F32)

    nt = (((1,), (1,)), ((), ()))
    rb = ATT_ROW_BLOCK
    stats = ((q1, m1_ref, l1_ref, a1_ref), (q2, m2_ref, l2_ref, a2_ref))

    def chunks(specs):
        ks = [kb_ref[pl.ds(start, tq), :] for start, _ in specs]
        vs = [vb_ref[pl.ds(start, tq), :] for start, _ in specs]
        units = [(ci, r, mp) for ci in range(len(specs)) for r in range(tq // rb) for mp in range(2)]

        def scores(u):
            ci, r, mp = u
            s = lax.dot_general(stats[mp][0][r * rb:(r + 1) * rb], ks[ci], nt, preferred_element_type=F32)
            bias = specs[ci][1]
            return s if bias is None else s + bias[r * rb:(r + 1) * rb, :]

        s_cur = scores(units[0])
        for i, (ci, r, mp) in enumerate(units):
            s_next = scores(units[i + 1]) if i + 1 < len(units) else None
            _, m_ref, l_ref, a_ref = stats[mp]
            rows = slice(r * rb, (r + 1) * rb)
            _softmax_step(s_cur, vs[ci], m_ref.at[rows], l_ref.at[rows], a_ref.at[rows])
            s_cur = s_next

    n_far = qi - 1

    def far_pair(c, carry):
        chunks([(pl.multiple_of(2 * c * tq, tq), None), (pl.multiple_of((2 * c + 1) * tq, tq), None)])
        return carry

    lax.fori_loop(0, n_far // 2, far_pair, 0)

    @pl.when(jnp.logical_and(n_far > 0, n_far % 2 == 1))
    def _():
        chunks([(pl.multiple_of((n_far - 1) * tq, tq), None)])

    @pl.when(qi > 0)
    def _():
        chunks([(pl.multiple_of((qi - 1) * tq, tq), bt_ref[:, 0:tq]),
                (pl.multiple_of(qi * tq, tq), bt_ref[:, tq:2 * tq])])

    @pl.when(qi == 0)
    def _():
        chunks([(0, bt_ref[:, tq:2 * tq])])

    o = a1_ref[...] / l1_ref[...] - lam_ref[0] * (a2_ref[...] / l2_ref[...])
    o = o * lax.rsqrt(jnp.mean(o * o, axis=-1, keepdims=True) + RMS_EPS) * g_ref[...]
    o_ref[...] = o * out_scale


def attn_prompt(qkv, lam, lam_init, bias_tiles, subln_g):
    bn, l, _ = qkv.shape
    hd = 2 * ATT_HEAD_DIM
    tq = ATT_TQ
    stat = pltpu.VMEM((tq, hd), F32)
    return pl.pallas_call(
        functools.partial(_attn_prompt_kernel, out_scale=1.0 - lam_init),
        grid=(bn, ATT_HEADS, l // tq),
        in_specs=[pl.BlockSpec(memory_space=pltpu.SMEM),
                  pl.BlockSpec((None, tq, hd), lambda b, h, i: (b, i, h)),
                  pl.BlockSpec((None, l, hd), lambda b, h, i: (b, 0, ATT_HEADS + h)),
                  pl.BlockSpec((None, l, hd), lambda b, h, i: (b, 0, 2 * ATT_HEADS + h)),
                  pl.BlockSpec((None, tq, 2 * tq), lambda b, h, i: (h, 0, 0)),
                  pl.BlockSpec((1, hd), lambda b, h, i: (0, 0))],
        out_specs=pl.BlockSpec((None, tq, hd), lambda b, h, i: (b, i, h)),
        out_shape=jax.ShapeDtypeStruct((bn, l, ATT_HEADS * hd), F32),
        scratch_shapes=[pltpu.VMEM((l, hd), BF16), pltpu.VMEM((l, hd), BF16),
                        stat, stat, stat, stat, stat, stat],
        compiler_params=_cparams(3),
        name="attn_prompt",
    )(lam.reshape(1), qkv, qkv, qkv, bias_tiles, subln_g.reshape(1, hd))


ATT_PAGES_PER_STEP = 8
ATT_ROWS = 2 * ATT_HEADS


def _row_select(vec_row, h, lane_lo, lane_hi):
    rid = lax.broadcasted_iota(jnp.int32, (ATT_ROWS, LANES), 0)
    lane = lax.broadcasted_iota(jnp.int32, (ATT_ROWS, LANES), 1)
    full = jnp.broadcast_to(vec_row, (ATT_ROWS, LANES))
    keep = jnp.logical_or(jnp.logical_and(rid == 2 * h, lane < lane_hi),
                          jnp.logical_and(rid == 2 * h + 1, lane >= lane_lo))
    return jnp.where(keep, full, 0.0)


def _attn_decode_kernel(pt_ref, lam_ref, qkv_ref, bl_ref, bc_ref, g_ref, *refs, out_scale):
    npg = ATT_PAGES_PER_STEP
    k_refs, v_refs = refs[:npg], refs[npg:2 * npg]
    o_ref, qs_ref, m_ref, l_ref, acc_ref = refs[2 * npg:]
    del pt_ref
    step = pl.program_id(1)
    last = pl.num_programs(1) - 1
    rid = lax.broadcasted_iota(jnp.int32, (ATT_ROWS, LANES), 0)
    half = ATT_HEAD_DIM
    hd = 2 * ATT_HEAD_DIM

    @pl.when(step == 0)
    def _():
        m_ref[...] = jnp.full(m_ref.shape, -jnp.inf, F32)
        l_ref[...] = jnp.zeros(l_ref.shape, F32)
        acc_ref[...] = jnp.zeros(acc_ref.shape, F32)
        for h in range(ATT_HEADS):
            qh = qkv_ref[:, h * hd:(h + 1) * hd] * (ATT_HEAD_DIM ** -0.5)
            qs_ref[h] = _row_select(qh, h, half, half).astype(BF16)

    def head_rows(page_refs, h):
        return jnp.concatenate(
            [r[pl.ds(h, PAGE_SIZE, stride=ATT_HEADS), :].astype(BF16) for r in page_refs], axis=0)

    s = None
    for h in range(ATT_HEADS):
        t = lax.dot_general(qs_ref[h], head_rows(k_refs, h), _NT, preferred_element_type=F32)
        s = t if s is None else s + t
    tail_bias = jnp.where(step == last, 1.0, 0.0) * bl_ref[...]
    s = jnp.concatenate([s[:, :(npg - 1) * PAGE_SIZE], s[:, (npg - 1) * PAGE_SIZE:] + tail_bias], axis=1)

    m_prev = m_ref[...]
    m_new = jnp.maximum(m_prev, jnp.max(s, axis=-1, keepdims=True))
    alpha = jnp.exp(m_prev - m_new)
    p = jnp.exp(s - m_new[:, 0:1])
    l_ref[...] = alpha * l_ref[...] + jnp.sum(p, axis=-1, keepdims=True)
    pb = p.astype(BF16)
    pv = None
    for h in range(ATT_HEADS):
        t = jnp.dot(pb, head_rows(v_refs, h), preferred_element_type=F32)
        t = jnp.where(jnp.logical_or(rid == 2 * h, rid == 2 * h + 1), t, 0.0)
        pv = t if pv is None else pv + t
    acc_ref[...] = alpha * acc_ref[...] + pv
    m_ref[...] = m_new

    @pl.when(step == last)
    def _():
        s_cur = None
        v_rows = None
        for h in range(ATT_HEADS):
            kn = qkv_ref[:, D_MODEL + h * hd:D_MODEL + (h + 1) * hd].astype(BF16).astype(F32)
            vn = qkv_ref[:, 2 * D_MODEL + h * hd:2 * D_MODEL + (h + 1) * hd].astype(BF16).astype(F32)
            t = jnp.sum(qs_ref[h].astype(F32) * kn, axis=-1, keepdims=True)
            s_cur = t if s_cur is None else s_cur + t
            vr = _row_select(vn, h, 0, LANES)
            v_rows = vr if v_rows is None else v_rows + vr
        s_cur = s_cur + bc_ref[...][:, 0:1]
        m_prev = m_ref[...]
        m_new = jnp.maximum(m_prev, s_cur)
        alpha = jnp.exp(m_prev - m_new)
        p = jnp.exp(s_cur - m_new[:, 0:1])
        l_fin = alpha * l_ref[...] + p
        acc = alpha * acc_ref[...] + p.astype(BF16).astype(F32) * v_rows
        o = acc / l_fin
        er = lax.broadcasted_iota(jnp.int32, (ATT_HEADS, ATT_ROWS), 0)
        ec = lax.broadcasted_iota(jnp.int32, (ATT_HEADS, ATT_ROWS), 1)
        pick1 = jnp.where(ec == 2 * er, 1.0, 0.0).astype(BF16)
        pick2 = jnp.where(ec == 2 * er + 1, 1.0, 0.0).astype(BF16)
        hi, mid, lo = _split3(o)
        o1 = (jnp.dot(pick1, hi, preferred_element_type=F32) + jnp.dot(pick1, mid, preferred_element_type=F32)
              + jnp.dot(pick1, lo, preferred_element_type=F32))
        o2 = (jnp.dot(pick2, hi, preferred_element_type=F32) + jnp.dot(pick2, mid, preferred_element_type=F32)
              + jnp.dot(pick2, lo, preferred_element_type=F32))
        d = o1 - lam_ref[0] * o2
        d = d * lax.rsqrt(jnp.mean(d * d, axis=-1, keepdims=True) + RMS_EPS) * g_ref[...]
        o_ref[...] = d * out_scale


def attn_decode(qkv, k_pages, v_pages, layer, page_table, lam, lam_init, rel_bias, subln_g):
    bn = qkv.shape[0]
    n_pages = page_table.shape[1]
    npg = ATT_PAGES_PER_STEP
    hd = 2 * ATT_HEAD_DIM
    dist = PAGE_SIZE - np.arange(PAGE_SIZE)
    bucket = np.where(dist < MAX_DISTANCE, T5_BUCKETS[np.minimum(dist, MAX_DISTANCE - 1)], N_BUCKETS - 1)
    far = rel_bias[N_BUCKETS - 1]
    bias_last = jnp.repeat((rel_bias[bucket] - far).T, 2, axis=0)
    bias_cur = jnp.broadcast_to(jnp.repeat(rel_bias[0] - far, 2)[:, None], (ATT_ROWS, LANES))

    def rows_of(pages):
        return pages.reshape(pages.shape[0], pages.shape[1], PAGE_SIZE * ATT_HEADS, hd)

    def page_spec(pg):
        return pl.BlockSpec((None, None, PAGE_SIZE * ATT_HEADS, hd),
                            lambda b, s, pt: (layer, pt[b, s * npg + pg], 0, 0))

    full = lambda shape: pl.BlockSpec(shape, lambda b, s, pt: (0,) * len(shape))
    stat = pltpu.VMEM((ATT_ROWS, LANES), F32)
    grid_spec = pltpu.PrefetchScalarGridSpec(
        num_scalar_prefetch=1,
        grid=(bn, n_pages // npg),
        in_specs=[pl.BlockSpec(memory_space=pltpu.SMEM),
                  pl.BlockSpec((None, 1, 3 * D_MODEL), lambda b, s, pt: (b, 0, 0)),
                  full((ATT_ROWS, PAGE_SIZE)), full((ATT_ROWS, LANES)), full((1, hd))]
                 + [page_spec(pg) for pg in range(npg)] * 2,
        out_specs=pl.BlockSpec((None, ATT_HEADS, hd), lambda b, s, pt: (b, 0, 0)),
        scratch_shapes=[pltpu.VMEM((ATT_HEADS, ATT_ROWS, LANES), BF16), stat, stat, stat],
    )
    out = pl.pallas_call(
        functools.partial(_attn_decode_kernel, out_scale=1.0 - lam_init),
        grid_spec=grid_spec,
        out_shape=jax.ShapeDtypeStruct((bn, ATT_HEADS, hd), F32),
        compiler_params=_cparams(2),
        name="attn_decode",
    )(page_table, lam.reshape(1), qkv.reshape(bn, 1, 3 * D_MODEL), bias_last, bias_cur,
      subln_g.reshape(1, hd), *([rows_of(k_pages)] * npg), *([rows_of(v_pages)] * npg))
    return out.reshape(bn, D_MODEL)


def _split3(v):
    hi = v.astype(BF16)
    r1 = v - hi.astype(F32)
    mid = r1.astype(BF16)
    lo = (r1 - mid.astype(F32)).astype(BF16)
    return hi, mid, lo


def _dot3(v, w_bf16, dims=None):
    out = None
    for piece in _split3(v):
        if dims is None:
            t = jnp.dot(piece, w_bf16, preferred_element_type=F32)
        else:
            t = lax.dot_general(piece, w_bf16, dims, preferred_element_type=F32)
        out = t if out is None else out + t
    return out


def _softplus(x):
    return jnp.maximum(x, 0.0) + jnp.log1p(jnp.exp(-jnp.abs(x)))


_NT = (((1,), (1,)), ((), ()))
_TN = (((0,), (0,)), ((), ()))
SSM_GROUP_W = SSM_HPG * SSM_HEAD_DIM
SSM_GN = SSM_GROUPS * SSM_STATE


def _ssd_prompt_kernel(proj_ref, dt_ref, cw_ref, cb_ref, dtb_ref, aneg_ref, dskip_ref, ng_ref, e_ref,
                       y_ref, h_ref, xp_ref, ht_ref):
    ci = pl.program_id(1)
    c = SSM_CHUNK
    di = SSM_D_INNER

    @pl.when(ci == 0)
    def _():
        xp_ref[0:8, :] = jnp.zeros((8, SSM_CONV_DIM), F32)
        ht_ref[...] = jnp.zeros(ht_ref.shape, F32)

    xbc = proj_ref[:, di:di + SSM_CONV_DIM]
    xp_ref[8:8 + c, :] = xbc
    conv = cb_ref[...] + xp_ref[5:5 + c, :] * cw_ref[0:1, :]
    for i in range(1, CONV_W):
        conv = conv + xp_ref[5 + i:5 + i + c, :] * cw_ref[i:i + 1, :]
    xp_ref[0:8, :] = xbc[c - 8:c, :]
    act = _silu(conv)
    xs = act[:, :di]

    dt = _softplus(dt_ref[...] + dtb_ref[...])
    a = dt * aneg_ref[...]
    row = lax.broadcasted_iota(jnp.int32, (c, c), 0)
    col = lax.broadcasted_iota(jnp.int32, (c, c), 1)
    incl = col <= row
    tri = jnp.where(incl, 1.0, 0.0).astype(BF16)
    hi, mid, lo = _split3(a)
    acum = (jnp.dot(tri, hi, preferred_element_type=F32) + jnp.dot(tri, mid, preferred_element_type=F32)
            + jnp.dot(tri, lo, preferred_element_type=F32))
    upper = jnp.where(row <= col, 1.0, 0.0).astype(BF16)
    acum_t = _dot3(a, upper, _TN)
    a_end = acum[c - 1:c, :]
    eac = jnp.exp(acum)
    wend = jnp.exp(a_end - acum) * dt
    ex = _dot3(jnp.concatenate([dt, eac, wend], axis=0), e_ref[...])
    dt_x, eac_x, wend_x = ex[0:c], ex[c:2 * c], ex[2 * c:3 * c]
    xdt = (xs * dt_x).astype(BF16)
    xw = (xs * wend_x).astype(BF16)
    lane = lax.broadcasted_iota(jnp.int32, (c, LANES), 1)
    lo_half = lane < SSM_HEAD_DIM
    neg_inf = jnp.float32(-jnp.inf)

    for g in range(SSM_GROUPS):
        gs = slice(g * SSM_GROUP_W, (g + 1) * SSM_GROUP_W)
        bg = act[:, di + g * SSM_STATE:di + (g + 1) * SSM_STATE].astype(BF16)
        cg = act[:, di + SSM_GN + g * SSM_STATE:di + SSM_GN + (g + 1) * SSM_STATE].astype(BF16)
        cb = lax.dot_general(cg, bg, _NT, preferred_element_type=F32)
        pieces = []
        for pr in range(SSM_GROUP_W // LANES):
            pair = g * (SSM_GROUP_W // LANES) + pr
            xd = xdt[:, pair * LANES:(pair + 1) * LANES]
            acc = None
            for half in range(2):
                h = 2 * pair + half
                diff = acum[:, h:h + 1] - acum_t[h:h + 1, :]
                s = (cb * jnp.exp(jnp.where(incl, diff, neg_inf))).astype(BF16)
                xm = jnp.where(lo_half if half == 0 else jnp.logical_not(lo_half), xd, jnp.zeros_like(xd))
                t = jnp.dot(s, xm, preferred_element_type=F32)
                acc = t if acc is None else acc + t
            pieces.append(acc)
        y_intra = jnp.concatenate(pieces, axis=1)
        h_old = ht_ref[:, gs]
        y_inter = jnp.dot(cg, h_old.astype(BF16), preferred_element_type=F32) * eac_x[:, gs]
        dstate = lax.dot_general(bg, xw[:, gs], _TN, preferred_element_type=F32)
        ht_ref[:, gs] = h_old * eac_x[c - 1:c, gs] + dstate
        y = y_intra + y_inter + dskip_ref[:, gs] * xs[:, gs]
        y = y * _silu(proj_ref[:, gs])
        y = y * lax.rsqrt(jnp.mean(y * y, axis=-1, keepdims=True) + RMS_EPS) * ng_ref[:, gs]
        y_ref[:, gs] = y

    @pl.when(ci == pl.num_programs(1) - 1)
    def _():
        h_ref[...] = ht_ref[...].T


def _head_expander():
    e = np.zeros((LANES, SSM_D_INNER), np.float32)
    for h in range(SSM_HEADS):
        e[h, h * SSM_HEAD_DIM:(h + 1) * SSM_HEAD_DIM] = 1.0
    return jnp.asarray(e, BF16)


def _pad_lanes(v):
    return jnp.pad(v.astype(F32), (0, LANES - v.shape[0])).reshape(1, LANES)


def ssd_prompt(proj, dtp, conv_w, conv_b, a_log, dt_bias, d_skip, norm_g):
    bn, l, pw = proj.shape
    c = SSM_CHUNK
    di = SSM_D_INNER
    full = lambda shape: pl.BlockSpec(shape, lambda b, i: (0,) * len(shape))
    y, h = pl.pallas_call(
        _ssd_prompt_kernel,
        grid=(bn, l // c),
        in_specs=[pl.BlockSpec((None, c, pw), lambda b, i: (b, i, 0)),
                  pl.BlockSpec((None, c, LANES), lambda b, i: (b, i, 0)),
                  full((CONV_W, SSM_CONV_DIM)), full((1, SSM_CONV_DIM)),
                  full((1, LANES)), full((1, LANES)), full((1, di)), full((1, di)),
                  full((LANES, di))],
        out_specs=[pl.BlockSpec((None, c, di), lambda b, i: (b, i, 0)),
                   pl.BlockSpec((None, di, SSM_STATE), lambda b, i: (b, 0, 0))],
        out_shape=[jax.ShapeDtypeStruct((bn, l, di), F32),
                   jax.ShapeDtypeStruct((bn, di, SSM_STATE), F32)],
        scratch_shapes=[pltpu.VMEM((8 + c, SSM_CONV_DIM), F32), pltpu.VMEM((SSM_STATE, di), F32)],
        compiler_params=_cparams(2),
        name="ssd_prompt",
    )(proj, dtp, conv_w, conv_b.reshape(1, -1), _pad_lanes(dt_bias), _pad_lanes(-jnp.exp(a_log)),
      jnp.repeat(d_skip, SSM_HEAD_DIM).reshape(1, di), norm_g.reshape(1, di), _head_expander())
    return y, h.reshape(bn, SSM_HEADS, SSM_HEAD_DIM, SSM_STATE)


GDN_BLK = 512
GDN_HEADS_PER_STEP = 4


def _split2(v):
    hi = v.astype(BF16)
    return hi, (v - hi.astype(F32)).astype(BF16)


def _mm_split(a, b):
    (a_hi, a_lo), (b_hi, b_lo) = a, b
    return (jnp.dot(a_hi, b_hi, preferred_element_type=F32) + jnp.dot(a_hi, b_lo, preferred_element_type=F32)
            + jnp.dot(a_lo, b_hi, preferred_element_type=F32))


def _conv_silu_block(x_ref, xp_ref, cw_ref):
    n = x_ref.shape[0]
    x = x_ref[...]
    xp_ref[8:8 + n, :] = x
    y = xp_ref[5:5 + n, :] * cw_ref[0:1, :]
    for i in range(1, CONV_W):
        y = y + xp_ref[5 + i:5 + i + n, :] * cw_ref[i:i + 1, :]
    xp_ref[0:8, :] = x[n - 8:n, :]
    return _silu(y)


def _gdn_prompt_kernel(alog_ref, dtb_ref, q_ref, k_ref, v_ref, z_ref, ba_ref, cwq_ref, cwk_ref, cwv_ref,
                       ng_ref, tri_ref, upper_ref, o_ref, s_out_ref, xq_ref, xk_ref, xv_ref, s_ref):
    hp = pl.program_id(1)
    bi = pl.program_id(2)
    blk = q_ref.shape[0]
    c = GDN_CHUNK
    hd = GDN_DK
    nhs = GDN_HEADS_PER_STEP

    @pl.when(bi == 0)
    def _():
        for r in (xq_ref, xk_ref, xv_ref):
            r[0:8, :] = jnp.zeros((8, r.shape[1]), F32)
        s_ref[...] = jnp.zeros(s_ref.shape, F32)

    q_all = _conv_silu_block(q_ref, xq_ref, cwq_ref)
    k_all = _conv_silu_block(k_ref, xk_ref, cwk_ref)
    v_all = _conv_silu_block(v_ref, xv_ref, cwv_ref)
    lane = lax.broadcasted_iota(jnp.int32, (blk, LANES), 1)
    ba = ba_ref[...]
    tri_bd = tri_ref[...]
    upper_bd = upper_ref[...]

    row = lax.broadcasted_iota(jnp.int32, (c, c), 0)
    col = lax.broadcasted_iota(jnp.int32, (c, c), 1)
    incl = col <= row
    strict = col < row
    eye = jnp.where(row == col, 1.0, 0.0)
    neg_inf = jnp.float32(-jnp.inf)
    nch = blk // c

    qh, kh, vh, betah, gcolh, growh, egh = [], [], [], [], [], [], []
    for e in range(nhs):
        h = hp * nhs + e
        hs = slice(e * hd, (e + 1) * hd)
        q = q_all[:, hs]
        k = k_all[:, hs]
        qh.append(q * lax.rsqrt(jnp.sum(q * q, axis=-1, keepdims=True) + 1e-6) * (GDN_DK ** -0.5))
        kh.append(k * lax.rsqrt(jnp.sum(k * k, axis=-1, keepdims=True) + 1e-6))
        vh.append(v_all[:, hs])
        b_col = jnp.sum(jnp.where(lane == h, ba, 0.0), axis=-1, keepdims=True)
        a_col = jnp.sum(jnp.where(lane == GDN_HEADS + h, ba, 0.0), axis=-1, keepdims=True)
        betah.append(1.0 / (1.0 + jnp.exp(-b_col)))
        g = -jnp.exp(alog_ref[h]) * _softplus(a_col + dtb_ref[h])
        g_rep = jnp.broadcast_to(g, (blk, LANES))
        hi, mid, lo = _split3(g_rep)
        gcolh.append(jnp.dot(tri_bd, hi, preferred_element_type=F32)
                     + jnp.dot(tri_bd, mid, preferred_element_type=F32)
                     + jnp.dot(tri_bd, lo, preferred_element_type=F32))
        growh.append(_dot3(g_rep[:, :c], upper_bd, _TN))
        egh.append(jnp.exp(gcolh[e][:, 0:1]))

    units = [(e, n) for n in range(nch) for e in range(nhs)]
    rws = lambda n: slice(n * c, (n + 1) * c)
    decays = [jnp.exp(jnp.where(incl, gcolh[e][rws(n), 0:c] - growh[e][:, n * LANES:n * LANES + c], neg_inf))
              for e, n in units]
    kbs = [kh[e][rws(n)].astype(BF16) for e, n in units]
    kks = [lax.dot_general(kb, kb, _NT, preferred_element_type=F32) for kb in kbs]
    qks = [lax.dot_general(qh[e][rws(n)].astype(BF16), kbs[u], _NT, preferred_element_type=F32) * decays[u]
           for u, (e, n) in enumerate(units)]
    mps = [jnp.where(strict, betah[e][rws(n)] * kks[u] * decays[u], 0.0) for u, (e, n) in enumerate(units)]
    ps = [eye - m for m in mps]
    for _ in range(5):
        mps = [_mm_split(_split2(mp), _split2(mp)) for mp in mps]
        mp_parts = [_split2(mp) for mp in mps]
        ps = [p + _mm_split(_split2(p), mpp) for p, mpp in zip(ps, mp_parts)]
    rhs = [jnp.concatenate([betah[e][rws(n)] * vh[e][rws(n)],
                            (betah[e][rws(n)] * egh[e][rws(n)]) * kh[e][rws(n)]], axis=1) for e, n in units]
    ws = [_mm_split(_split2(p), _split2(x)) for p, x in zip(ps, rhs)]
    uvs = [w[:, :GDN_DV].astype(BF16) for w in ws]
    uks = [w[:, GDN_DV:].astype(BF16) for w in ws]
    g_last = [gcolh[e][n * c + c - 1:n * c + c, 0:1] for e, n in units]
    kes = [(kh[e][rws(n)] * jnp.exp(g_last[u] - gcolh[e][rws(n), 0:1])).astype(BF16)
           for u, (e, n) in enumerate(units)]
    qkb = [x.astype(BF16) for x in qks]
    kmats = [lax.dot_general(kes[u], uks[u], _TN, preferred_element_type=F32).astype(BF16)
             for u in range(len(units))]
    hmats = [lax.dot_general(kes[u], uvs[u], _TN, preferred_element_type=F32) for u in range(len(units))]
    qps = [(qh[e][rws(n)] * egh[e][rws(n)] - jnp.dot(qkb[u], uks[u], preferred_element_type=F32)).astype(BF16)
           for u, (e, n) in enumerate(units)]
    ops = [jnp.dot(qkb[u], uvs[u], preferred_element_type=F32) for u in range(len(units))]

    states = [s_ref[e] for e in range(nhs)]
    for u, (e, n) in enumerate(units):
        hs = slice(e * hd, (e + 1) * hd)
        sb = states[e].astype(BF16)
        o = jnp.dot(qps[u], sb, preferred_element_type=F32) + ops[u]
        states[e] = (jnp.exp(g_last[u]) * states[e] - jnp.dot(kmats[u], sb, preferred_element_type=F32)
                     + hmats[u])
        o = o * lax.rsqrt(jnp.mean(o * o, axis=-1, keepdims=True) + RMS_EPS) * ng_ref[...]
        o_ref[rws(n), hs] = o * _silu(z_ref[rws(n), hs])
    for e in range(nhs):
        s_ref[e] = states[e]

    @pl.when(bi == pl.num_programs(2) - 1)
    def _():
        for e in range(nhs):
            s_out_ref[e] = states[e]


def gdn_prompt(proj, ba, conv_w, a_log, dt_bias, norm_g):
    bn, l, _ = proj.shape
    blk = GDN_BLK
    nh = GDN_HEADS
    hd = GDN_DK
    nhs = GDN_HEADS_PER_STEP
    hw = nhs * hd
    ngrp = nh // nhs
    col = lambda part: pl.BlockSpec((None, blk, hw), lambda b, h, i: (b, i, part * ngrp + h))
    cw = lambda part: pl.BlockSpec((CONV_W, hw), lambda b, h, i: (0, part * ngrp + h))
    tail = pltpu.VMEM((8 + blk, hw), F32)
    nch = blk // GDN_CHUNK
    t = np.arange(blk)
    same = (t[:, None] // GDN_CHUNK) == (t[None, :] // GDN_CHUNK)
    tri_bd = jnp.asarray(same & (t[None, :] <= t[:, None]), BF16)
    jj = np.arange(nch * LANES)
    upper = ((t[:, None] // GDN_CHUNK) == (jj[None, :] // LANES)) & \
            ((t[:, None] % GDN_CHUNK) <= (jj[None, :] % LANES)) & ((jj[None, :] % LANES) < GDN_CHUNK)
    upper_bd = jnp.asarray(upper, BF16)
    o, s = pl.pallas_call(
        _gdn_prompt_kernel,
        grid=(bn, ngrp, l // blk),
        in_specs=[pl.BlockSpec(memory_space=pltpu.SMEM), pl.BlockSpec(memory_space=pltpu.SMEM),
                  col(0), col(1), col(2), col(3),
                  pl.BlockSpec((None, blk, LANES), lambda b, h, i: (b, i, 0)),
                  cw(0), cw(1), cw(2),
                  pl.BlockSpec((1, hd), lambda b, h, i: (0, 0)),
                  pl.BlockSpec((blk, blk), lambda b, h, i: (0, 0)),
                  pl.BlockSpec((blk, nch * LANES), lambda b, h, i: (0, 0))],
        out_specs=[pl.BlockSpec((None, blk, hw), lambda b, h, i: (b, i, h)),
                   pl.BlockSpec((None, nhs, hd, hd), lambda b, h, i: (b, h, 0, 0))],
        out_shape=[jax.ShapeDtypeStruct((bn, l, nh * hd), F32),
                   jax.ShapeDtypeStruct((bn, nh, hd, hd), F32)],
        scratch_shapes=[tail, tail, tail, pltpu.VMEM((nhs, hd, hd), F32)],
        compiler_params=_cparams(3),
        name="gdn_prompt",
    )(a_log.astype(F32), dt_bias.astype(F32), proj, proj, proj, proj, ba, conv_w, conv_w, conv_w,
      norm_g.reshape(1, hd), tri_bd, upper_bd)
    return o, s


SUB = 16


def _rows8(row, zero_fill, dtype=F32):
    full = jnp.broadcast_to(row.astype(F32), (SUB, row.shape[1]))
    if zero_fill:
        rid = lax.broadcasted_iota(jnp.int32, full.shape, 0)
        full = jnp.where(rid == 0, full, 0.0)
    return full.astype(dtype)


def _conv_step(x_row, c_ref, cw_ref, cb_row):
    y = x_row * cw_ref[CONV_W - 1:CONV_W, :]
    for i in range(CONV_W - 1):
        y = y + c_ref[i:i + 1, :] * cw_ref[i:i + 1, :]
    if cb_row is not None:
        y = y + cb_row
    return _silu(y)


def _gdn_step_kernel(alog_ref, dtb_ref, proj_ref, ba_ref, c_ref, cw_ref, ng_ref, s_ref, o_ref, s_out_ref):
    hd = GDN_DK
    act = _conv_step(proj_ref[:, :GDN_CONV_DIM], c_ref, cw_ref, None)
    ba = ba_ref[...]
    qs, kbs, us, egs, qks = [], [], [], [], []
    for h in range(GDN_HEADS):
        q = act[:, h * hd:(h + 1) * hd]
        k = act[:, GDN_KD + h * hd:GDN_KD + (h + 1) * hd]
        q = q * lax.rsqrt(jnp.sum(q * q, axis=-1, keepdims=True) + 1e-6) * (GDN_DK ** -0.5)
        k = k * lax.rsqrt(jnp.sum(k * k, axis=-1, keepdims=True) + 1e-6)
        qs.append(q)
        kbs.append(k)
        beta = 1.0 / (1.0 + jnp.exp(-ba[:, h:h + 1]))
        g = -jnp.exp(alog_ref[h]) * _softplus(ba[:, GDN_HEADS + h:GDN_HEADS + h + 1] + dtb_ref[h])
        egs.append(jnp.exp(g))
        us.append(beta)
        qks.append(jnp.sum(q * k, axis=-1, keepdims=True))
    ks_ = [jnp.dot(_rows8(kbs[h], False), s_ref[h], preferred_element_type=F32, precision=_HIGHEST)[0:1]
           for h in range(GDN_HEADS)]
    qs_ = [jnp.dot(_rows8(qs[h], False), s_ref[h], preferred_element_type=F32, precision=_HIGHEST)[0:1]
           for h in range(GDN_HEADS)]
    for h in range(GDN_HEADS):
        v = act[:, 2 * GDN_KD + h * hd:2 * GDN_KD + (h + 1) * hd]
        u = us[h] * (v - egs[h] * ks_[h])
        o = egs[h] * qs_[h] + qks[h] * u
        outer = lax.dot_general(_rows8(kbs[h], True), _rows8(u, False), _TN,
                                preferred_element_type=F32, precision=_HIGHEST)
        s_out_ref[h] = egs[h] * s_ref[h] + outer
        o = o * lax.rsqrt(jnp.mean(o * o, axis=-1, keepdims=True) + RMS_EPS) * ng_ref[...]
        z = proj_ref[:, GDN_CONV_DIM + h * hd:GDN_CONV_DIM + (h + 1) * hd]
        o_ref[:, h * hd:(h + 1) * hd] = o * _silu(z)


def gdn_step(proj, ba, s0, conv0, conv_w, a_log, dt_bias, norm_g):
    bn, pw = proj.shape
    hd = GDN_DK
    o, s = pl.pallas_call(
        _gdn_step_kernel,
        grid=(bn,),
        in_specs=[pl.BlockSpec(memory_space=pltpu.SMEM), pl.BlockSpec(memory_space=pltpu.SMEM),
                  pl.BlockSpec((None, 1, pw), lambda b: (b, 0, 0)),
                  pl.BlockSpec((None, 1, LANES), lambda b: (b, 0, 0)),
                  pl.BlockSpec((None, CONV_W - 1, GDN_CONV_DIM), lambda b: (b, 0, 0)),
                  pl.BlockSpec((CONV_W, GDN_CONV_DIM), lambda b: (0, 0)),
                  pl.BlockSpec((1, hd), lambda b: (0, 0)),
                  pl.BlockSpec((None, GDN_HEADS, hd, hd), lambda b: (b, 0, 0, 0))],
        out_specs=[pl.BlockSpec((None, 1, GDN_VD), lambda b: (b, 0, 0)),
                   pl.BlockSpec((None, GDN_HEADS, hd, hd), lambda b: (b, 0, 0, 0))],
        out_shape=[jax.ShapeDtypeStruct((bn, 1, GDN_VD), F32),
                   jax.ShapeDtypeStruct(s0.shape, F32)],
        compiler_params=_cparams(1),
        name="gdn_step",
    )(a_log.astype(F32), dt_bias.astype(F32), proj.reshape(bn, 1, pw), ba.reshape(bn, 1, LANES),
      conv0, conv_w, norm_g.reshape(1, hd), s0)
    return o.reshape(bn, GDN_VD), s


def _ssd_step_kernel(proj_ref, dt_ref, c_ref, cw_ref, cb_ref, dtb_ref, aneg_ref, dskip_ref, ng_ref, e_ref,
                     h_ref, y_ref, h_out_ref):
    di = SSM_D_INNER
    act = _conv_step(proj_ref[:, di:di + SSM_CONV_DIM], c_ref, cw_ref, cb_ref[...])
    xs = act[:, :di]
    dt = _softplus(dt_ref[...] + dtb_ref[...])
    ea = jnp.exp(dt * aneg_ref[...])
    ex = _dot3(jnp.concatenate([_rows8(dt, True), _rows8(ea, True)], axis=0), e_ref[...])
    dt_x, ea_x = ex[0:1], ex[SUB:SUB + 1]
    xdt = xs * dt_x
    ones = jnp.ones((SUB, SSM_STATE), BF16)
    for g in range(SSM_GROUPS):
        gs = slice(g * SSM_GROUP_W, (g + 1) * SSM_GROUP_W)
        bg = act[:, di + g * SSM_STATE:di + (g + 1) * SSM_STATE]
        cg = act[:, di + SSM_GN + g * SSM_STATE:di + SSM_GN + (g + 1) * SSM_STATE]
        dstate = lax.dot_general(_rows8(xdt[:, gs], True), _rows8(bg, False), _TN,
                                 preferred_element_type=F32, precision=_HIGHEST)
        decay = _dot3(_rows8(ea_x[:, gs], True), ones, _TN)
        h_old = h_ref[g * SSM_HPG:(g + 1) * SSM_HPG].reshape(SSM_GROUP_W, SSM_STATE)
        h_new = decay * h_old + dstate
        h_out_ref[g * SSM_HPG:(g + 1) * SSM_HPG] = h_new.reshape(SSM_HPG, SSM_HEAD_DIM, SSM_STATE)
        y = lax.dot_general(_rows8(cg, False), h_new, _NT, preferred_element_type=F32,
                            precision=_HIGHEST)[0:1]
        y = y + dskip_ref[:, gs] * xs[:, gs]
        y = y * _silu(proj_ref[:, gs])
        y = y * lax.rsqrt(jnp.mean(y * y, axis=-1, keepdims=True) + RMS_EPS) * ng_ref[:, gs]
        y_ref[:, gs] = y


def ssd_step(proj, dtp, h0, conv0, conv_w, conv_b, a_log, dt_bias, d_skip, norm_g):
    bn, pw = proj.shape
    di = SSM_D_INNER
    full = lambda shape: pl.BlockSpec(shape, lambda b: (0,) * len(shape))
    hblk = pl.BlockSpec((None, SSM_HEADS, SSM_HEAD_DIM, SSM_STATE), lambda b: (b, 0, 0, 0))
    y, h = pl.pallas_call(
        _ssd_step_kernel,
        grid=(bn,),
        in_specs=[pl.BlockSpec((None, 1, pw), lambda b: (b, 0, 0)),
                  pl.BlockSpec((None, 1, LANES), lambda b: (b, 0, 0)),
                  pl.BlockSpec((None, CONV_W - 1, SSM_CONV_DIM), lambda b: (b, 0, 0)),
                  full((CONV_W, SSM_CONV_DIM)), full((1, SSM_CONV_DIM)),
                  full((1, LANES)), full((1, LANES)), full((1, di)), full((1, di)), full((LANES, di)),
                  hblk],
        out_specs=[pl.BlockSpec((None, 1, di), lambda b: (b, 0, 0)), hblk],
        out_shape=[jax.ShapeDtypeStruct((bn, 1, di), F32), jax.ShapeDtypeStruct(h0.shape, F32)],
        compiler_params=_cparams(1),
        name="ssd_step",
    )(proj.reshape(bn, 1, pw), dtp.reshape(bn, 1, LANES), conv0, conv_w, conv_b.reshape(1, -1),
      _pad_lanes(dt_bias), _pad_lanes(-jnp.exp(a_log)), jnp.repeat(d_skip, SSM_HEAD_DIM).reshape(1, di),
      norm_g.reshape(1, di), _head_expander(), h0)
    return y.reshape(bn, di), h


def _pad_cols(w):
    return jnp.pad(w, ((0, 0), (0, LANES - w.shape[1])))


def kernel(x_prompt, x_sample, state_gdn_S, state_gdn_conv, state_ssm_h, state_ssm_conv,
           cache_attn_k, cache_attn_v, page_table,
           gdn_w_in, gdn_conv_w, gdn_a_log, gdn_dt_bias, gdn_norm_g, gdn_w_out,
           ssm_w_in, ssm_conv_w, ssm_conv_b, ssm_a_log, ssm_dt_bias, ssm_d_skip, ssm_norm_g, ssm_w_out,
           att_w_qkv, att_lam_q1, att_lam_k1, att_lam_q2, att_lam_k2, att_subln_g, att_w_out, rel_bias,
           ffn_w_gu, ffn_w_down, moe_w_router, moe_w_gu, moe_w_down, ln_g, ln_b):
    bp, lp, d = x_prompt.shape
    bs, ls, _ = x_sample.shape
    assert ls == 1, "the sample group advances one token per sequence"
    tp = bp * lp
    ts = bs * ls
    xp = x_prompt.reshape(tp, d)
    xs = x_sample.reshape(ts, d)
    moe_gu = moe_w_gu.reshape((-1,) + moe_w_gu.shape[2:])
    moe_down = moe_w_down.reshape((-1,) + moe_w_down.shape[2:])

    tm_p = 1024
    tm_s = ts

    outs = {name: [] for name in ("gS_p", "gS_s", "gc_p", "gc_s", "sh_p", "sh_s", "sc_p", "sc_s",
                                  "ak_p", "ak_s", "av_p", "av_s")}
    for i in range(DEPTH):
        j = i // N_MIXERS
        if i % N_MIXERS == 0:
            n_main = GDN_CONV_DIM + GDN_VD
            w_ba = _pad_cols(gdn_w_in[j][:, n_main:])
            prm = (gdn_conv_w[j], gdn_a_log[j], gdn_dt_bias[j], gdn_norm_g[j])
            proj_p = matmul_cols(xp, gdn_w_in[j], n_main, tm_p, 512).reshape(bp, lp, n_main)
            ba_p = matmul_narrow(xp, w_ba, tm_p).reshape(bp, lp, LANES)
            op, s_p = gdn_prompt(proj_p, ba_p, *prm)
            c_p = proj_p[:, lp - (CONV_W - 1):, :GDN_CONV_DIM]
            proj_s = matmul_cols(xs, gdn_w_in[j], n_main, tm_s, 512, precise=True)
            ba_s = matmul_narrow(xs, w_ba, tm_s)
            os_, s_s = gdn_step(proj_s, ba_s, state_gdn_S[j], state_gdn_conv[j], *prm)
            c_s = jnp.concatenate([state_gdn_conv[j][:, 1:], proj_s[:, None, :GDN_CONV_DIM]], axis=1)
            outs["gS_p"].append(s_p); outs["gS_s"].append(s_s)
            outs["gc_p"].append(c_p); outs["gc_s"].append(c_s)
            w_out = gdn_w_out[j]
        elif i % N_MIXERS == 1:
            n_main = SSM_D_INNER + SSM_CONV_DIM
            w_dt = _pad_cols(ssm_w_in[j][:, n_main:])
            prm = (ssm_conv_w[j], ssm_conv_b[j], ssm_a_log[j], ssm_dt_bias[j], ssm_d_skip[j], ssm_norm_g[j])
            proj_p = matmul_cols(xp, ssm_w_in[j], n_main, tm_p, 512).reshape(bp, lp, n_main)
            dt_p = matmul_narrow(xp, w_dt, tm_p).reshape(bp, lp, LANES)
            op, h_p = ssd_prompt(proj_p, dt_p, *prm)
            c_p = proj_p[:, lp - (CONV_W - 1):, SSM_D_INNER:]
            proj_s = matmul_cols(xs, ssm_w_in[j], n_main, tm_s, 512, precise=True)
            dt_s = matmul_narrow(xs, w_dt, tm_s)
            os_, h_s = ssd_step(proj_s, dt_s, state_ssm_h[j], state_ssm_conv[j], *prm)
            c_s = jnp.concatenate([state_ssm_conv[j][:, 1:], proj_s[:, None, SSM_D_INNER:]], axis=1)
            outs["sh_p"].append(h_p); outs["sh_s"].append(h_s)
            outs["sc_p"].append(c_p); outs["sc_s"].append(c_s)
            w_out = ssm_w_out[j]
        else:
            lam_init = 0.8 - 0.6 * math.exp(-0.3 * i)
            lam = (jnp.exp(jnp.sum(att_lam_q1[j] * att_lam_k1[j]))
                   - jnp.exp(jnp.sum(att_lam_q2[j] * att_lam_k2[j])) + lam_init)
            qkv_p = matmul_cols(xp, att_w_qkv[j], 3 * d, tm_p, 512).reshape(bp, lp, 3 * d)
            op = attn_prompt(qkv_p, lam, lam_init, attn_bias_tiles(rel_bias), att_subln_g[j])
            k_p = qkv_p[..., d:2 * d].reshape(bp, lp, ATT_HEADS, 2 * ATT_HEAD_DIM)
            v_p = qkv_p[..., 2 * d:].reshape(bp, lp, ATT_HEADS, 2 * ATT_HEAD_DIM)
            qkv_s = matmul_cols(xs, att_w_qkv[j], 3 * d, tm_s, 512, precise=True).reshape(bs, ls, 3 * d)
            os_ = attn_decode(qkv_s.reshape(ts, 3 * d), cache_attn_k, cache_attn_v, j, page_table,
                              lam, lam_init, rel_bias, att_subln_g[j])
            k_s = qkv_s[..., d:2 * d].reshape(bs, ls, ATT_HEADS, 2 * ATT_HEAD_DIM)
            v_s = qkv_s[..., 2 * d:].reshape(bs, ls, ATT_HEADS, 2 * ATT_HEAD_DIM)
            outs["ak_p"].append(k_p); outs["ak_s"].append(k_s)
            outs["av_p"].append(v_p); outs["av_s"].append(v_s)
            w_out = att_w_out[j]
        xp = proj_residual_ln(op.reshape(tp, -1), w_out, xp, ln_g[i, 0], ln_b[i, 0], 512)
        xs = proj_residual_ln(os_.reshape(ts, -1), w_out, xs, ln_g[i, 0], ln_b[i, 0], tm_s, precise=True)
        f = i // 2
        if i % 2 == 0:
            wg, wd = chunk_major_bf16(ffn_w_gu[f:f + 1], ffn_w_down[f:f + 1])
            xp = dense_ffn_ln(xp, wg, wd, 0, ln_g[i, 1], ln_b[i, 1], tm_p)
            xs = dense_ffn_ln(xs, ffn_w_gu, ffn_w_down, f, ln_g[i, 1], ln_b[i, 1], tm_s, precise=True)
        else:
            wg, wd = chunk_major_bf16(moe_w_gu[f], moe_w_down[f])
            xp = moe_ffn_ln(xp, moe_w_router[f], wg, wd, 0, ln_g[i, 1], ln_b[i, 1], 512, 1024, 256)
            xs = moe_ffn_ln(xs, moe_w_router[f], moe_gu, moe_down, f, ln_g[i, 1], ln_b[i, 1],
                            ts, ts, ts, precise=True)
    st = lambda name: jnp.stack(outs[name])
    return (xp.reshape(bp, lp, d), xs.reshape(bs, ls, d),
            st("gS_p"), st("gS_s"), st("gc_p"), st("gc_s"),
            st("sh_p"), st("sh_s"), st("sc_p"), st("sc_s"),
            st("ak_p"), st("ak_s"), st("av_p"), st("av_s"))
```

```python
import functools
import math

import numpy as np
import jax
import jax.numpy as jnp
from jax import lax
from jax.experimental import pallas as pl
from jax.experimental.pallas import tpu as pltpu

F32 = jnp.float32
BF16 = jnp.bfloat16

D_MODEL = 1024
DEPTH = 4
PAGE_SIZE = 128
N_MIXERS = 3
CONV_W = 4

GDN_HEADS = 8
GDN_DK = 128
GDN_DV = 128
GDN_KD = GDN_HEADS * GDN_DK
GDN_VD = GDN_HEADS * GDN_DV
GDN_CONV_DIM = 2 * GDN_KD + GDN_VD
GDN_CHUNK = 64

SSM_D_INNER = 2 * D_MODEL
SSM_HEAD_DIM = 64
SSM_HEADS = SSM_D_INNER // SSM_HEAD_DIM
SSM_GROUPS = 4
SSM_HPG = SSM_HEADS // SSM_GROUPS
SSM_STATE = 128
SSM_CONV_DIM = SSM_D_INNER + 2 * SSM_GROUPS * SSM_STATE
SSM_CHUNK = 128

ATT_HEADS = 8
ATT_HEAD_DIM = D_MODEL // (2 * ATT_HEADS)
N_BUCKETS = 32
MAX_DISTANCE = 128
Q_BLOCK = 128

D_FF = 3584
N_EXPERTS = 8
TOP_K = 2

DEEPNORM_ALPHA = (2.0 * DEPTH) ** 0.25
LN_EPS = 1e-5
RMS_EPS = 1e-6

V7X_VMEM_BYTES = 64 * 1024 * 1024
VMEM_LIMIT_BYTES = V7X_VMEM_BYTES - 8 * 1024 * 1024
LANES = 128

FF_CHUNK = 512
N_FF_CHUNKS = D_FF // FF_CHUNK


def _cparams(n_axes, flags=None):
    return pltpu.CompilerParams(dimension_semantics=("arbitrary",) * n_axes,
                                vmem_limit_bytes=VMEM_LIMIT_BYTES, flags=flags)


def _layer_norm_rows(x, g, b):
    mu = jnp.mean(x, axis=-1, keepdims=True)
    xc = x - mu
    var = jnp.mean(xc * xc, axis=-1, keepdims=True)
    return xc * lax.rsqrt(var + LN_EPS) * g + b


def _silu(x):
    return x * (1.0 / (1.0 + jnp.exp(-x)))


_HIGHEST = lax.Precision.HIGHEST


def _matmul_kernel(x_ref, w_ref, o_ref, xb_ref, *, precise):
    if precise:
        o_ref[...] = jnp.dot(x_ref[...], w_ref[...], preferred_element_type=F32, precision=_HIGHEST)
        return

    @pl.when(pl.program_id(1) == 0)
    def _():
        xb_ref[...] = x_ref[...].astype(BF16)

    o_ref[...] = jnp.dot(xb_ref[...], w_ref[...].astype(BF16), preferred_element_type=F32)


def matmul_cols(x, w, n_cols, tm, tn, precise=False):
    m, k = x.shape
    return pl.pallas_call(
        functools.partial(_matmul_kernel, precise=precise),
        grid=(m // tm, n_cols // tn),
        in_specs=[pl.BlockSpec((tm, k), lambda i, j: (i, 0)),
                  pl.BlockSpec((k, tn), lambda i, j: (0, j))],
        out_specs=pl.BlockSpec((tm, tn), lambda i, j: (i, j)),
        out_shape=jax.ShapeDtypeStruct((m, n_cols), F32),
        scratch_shapes=[pltpu.VMEM((tm, k), BF16)],
        compiler_params=_cparams(2),
        name="matmul_cols",
    )(x, w)


def _matmul_narrow_kernel(x_ref, w_ref, o_ref):
    o_ref[...] = jnp.dot(x_ref[...], w_ref[...], preferred_element_type=F32, precision=_HIGHEST)


def matmul_narrow(x, w_pad, tm):
    m, k = x.shape
    return pl.pallas_call(
        _matmul_narrow_kernel,
        grid=(m // tm,),
        in_specs=[pl.BlockSpec((tm, k), lambda i: (i, 0)),
                  pl.BlockSpec((k, LANES), lambda i: (0, 0))],
        out_specs=pl.BlockSpec((tm, LANES), lambda i: (i, 0)),
        out_shape=jax.ShapeDtypeStruct((m, LANES), F32),
        compiler_params=_cparams(1),
        name="matmul_narrow",
    )(x, w_pad)


def _proj_ln_kernel(o_ref, w_ref, r_ref, g_ref, b_ref, y_ref, wb_ref, *, precise):
    if precise:
        m = jnp.dot(o_ref[...], w_ref[...], preferred_element_type=F32, precision=_HIGHEST)
    else:
        @pl.when(pl.program_id(0) == 0)
        def _():
            wb_ref[...] = w_ref[...].astype(BF16)

        m = jnp.dot(o_ref[...].astype(BF16), wb_ref[...], preferred_element_type=F32)
    y_ref[...] = _layer_norm_rows(DEEPNORM_ALPHA * r_ref[...] + m, g_ref[...], b_ref[...])


def proj_residual_ln(o, w, resid, g, b, tm, precise=False):
    m, k = o.shape
    d = w.shape[1]
    return pl.pallas_call(
        functools.partial(_proj_ln_kernel, precise=precise),
        grid=(m // tm,),
        in_specs=[pl.BlockSpec((tm, k), lambda i: (i, 0)),
                  pl.BlockSpec((k, d), lambda i: (0, 0)),
                  pl.BlockSpec((tm, d), lambda i: (i, 0)),
                  pl.BlockSpec((1, d), lambda i: (0, 0)),
                  pl.BlockSpec((1, d), lambda i: (0, 0))],
        out_specs=pl.BlockSpec((tm, d), lambda i: (i, 0)),
        out_shape=jax.ShapeDtypeStruct((m, d), F32),
        scratch_shapes=[pltpu.VMEM((k, d), BF16)],
        compiler_params=_cparams(1),
        name="proj_residual_ln",
    )(o, w, resid, g.reshape(1, d), b.reshape(1, d))


def _ffn_kernel(te_ref, na_ref, x_ref, wg_ref, wu_ref, wd_ref, g_ref, b_ref, o_ref,
                xb_ref, acc_ref, *, fuse_ln, precise):
    i = pl.program_id(0)
    j = pl.program_id(1)
    active = i < na_ref[0]

    @pl.when(jnp.logical_and(active, j == 0))
    def _():
        if not precise:
            xb_ref[...] = x_ref[...].astype(BF16)
        acc_ref[...] = jnp.zeros(acc_ref.shape, F32)

    @pl.when(active)
    def _():
        if precise:
            x = x_ref[...]
            h = jnp.dot(x, wg_ref[...], preferred_element_type=F32, precision=_HIGHEST)
            u = jnp.dot(x, wu_ref[...], preferred_element_type=F32, precision=_HIGHEST)
            acc_ref[...] += jnp.dot(_silu(h) * u, wd_ref[...], preferred_element_type=F32,
                                    precision=_HIGHEST)
        else:
            xb = xb_ref[...]
            h = jnp.dot(xb, wg_ref[...].astype(BF16), preferred_element_type=F32)
            u = jnp.dot(xb, wu_ref[...].astype(BF16), preferred_element_type=F32)
            a = (_silu(h) * u).astype(BF16)
            acc_ref[...] += jnp.dot(a, wd_ref[...].astype(BF16), preferred_element_type=F32)

    @pl.when(jnp.logical_and(active, j == N_FF_CHUNKS - 1))
    def _():
        if fuse_ln:
            o_ref[...] = _layer_norm_rows(DEEPNORM_ALPHA * x_ref[...] + acc_ref[...],
                                          g_ref[...], b_ref[...])
        else:
            o_ref[...] = acc_ref[...]

    @pl.when(jnp.logical_and(jnp.logical_not(active), j == N_FF_CHUNKS - 1))
    def _():
        o_ref[...] = jnp.zeros_like(o_ref)


def swiglu_tiles(x, w_gu, w_down, tile_expert, n_active, ln_g, ln_b, tm, fuse_ln, precise=False):
    m, d = x.shape
    n_tiles = m // tm

    def row_map(i, j, te, na):
        return (jnp.minimum(i, na[0] - 1), 0)

    def expert(i, te, na):
        return te[jnp.minimum(i, na[0] - 1)]

    w_specs = [pl.BlockSpec((None, d, FF_CHUNK), lambda i, j, te, na: (expert(i, te, na), 0, j)),
               pl.BlockSpec((None, d, FF_CHUNK),
                            lambda i, j, te, na: (expert(i, te, na), 0, j + N_FF_CHUNKS)),
               pl.BlockSpec((None, FF_CHUNK, d), lambda i, j, te, na: (expert(i, te, na), j, 0))]

    grid_spec = pltpu.PrefetchScalarGridSpec(
        num_scalar_prefetch=2,
        grid=(n_tiles, N_FF_CHUNKS),
        in_specs=[pl.BlockSpec((tm, d), row_map)] + w_specs + [
                  pl.BlockSpec((1, d), lambda i, j, te, na: (0, 0)),
                  pl.BlockSpec((1, d), lambda i, j, te, na: (0, 0))],
        out_specs=pl.BlockSpec((tm, d), lambda i, j, te, na: (i, 0)),
        scratch_shapes=[pltpu.VMEM((tm, d), BF16), pltpu.VMEM((tm, d), F32)],
    )
    return pl.pallas_call(
        functools.partial(_ffn_kernel, fuse_ln=fuse_ln, precise=precise),
        grid_spec=grid_spec,
        out_shape=jax.ShapeDtypeStruct((m, d), F32),
        compiler_params=_cparams(2),
        name="swiglu_ln" if fuse_ln else "swiglu_grouped",
    )(tile_expert, n_active, x, w_gu, w_gu, w_down, ln_g.reshape(1, d), ln_b.reshape(1, d))


def dense_ffn_ln(x, w_gu_all, w_down_all, layer, ln_g, ln_b, tm, precise=False):
    n_tiles = x.shape[0] // tm
    te = jnp.full((n_tiles,), layer, jnp.int32)
    na = jnp.full((1,), n_tiles, jnp.int32)
    return swiglu_tiles(x, w_gu_all, w_down_all, te, na, ln_g, ln_b, tm, True, precise)


def _router_kernel(x_ref, w_ref, meta_ref, cnt_ref, carry_ref, *, tm):
    i = pl.program_id(0)

    @pl.when(i == 0)
    def _():
        carry_ref[...] = jnp.zeros_like(carry_ref)

    logits = jnp.dot(x_ref[...], w_ref[...], preferred_element_type=F32, precision=_HIGHEST)
    lane = lax.broadcasted_iota(jnp.int32, (tm, LANES), 1)
    neg = jnp.float32(-jnp.inf)
    logits = jnp.where(lane < N_EXPERTS, logits, neg)
    m1 = jnp.max(logits, axis=-1, keepdims=True)
    i1 = jnp.min(jnp.where(logits == m1, lane, LANES), axis=-1, keepdims=True)
    rest = jnp.where(lane == i1, neg, logits)
    m2 = jnp.max(rest, axis=-1, keepdims=True)
    i2 = jnp.min(jnp.where(rest == m2, lane, LANES), axis=-1, keepdims=True)
    e = jnp.exp(m2 - m1)
    g1 = 1.0 / (1.0 + e)
    g2 = e / (1.0 + e)

    chosen = jnp.logical_or(lane == i1, lane == i2)
    onehot = jnp.where(chosen, 1.0, 0.0).astype(BF16)
    row = lax.broadcasted_iota(jnp.int32, (tm, tm), 0)
    col = lax.broadcasted_iota(jnp.int32, (tm, tm), 1)
    strict_lower = jnp.where(col < row, 1.0, 0.0).astype(BF16)
    before = jnp.dot(strict_lower, onehot, preferred_element_type=F32) + carry_ref[...]
    r1 = jnp.sum(jnp.where(lane == i1, before, 0.0), axis=-1, keepdims=True)
    r2 = jnp.sum(jnp.where(lane == i2, before, 0.0), axis=-1, keepdims=True)
    carry_ref[...] += jnp.sum(onehot.astype(F32), axis=0, keepdims=True)

    meta = jnp.where(lane == 0, i1.astype(F32),
           jnp.where(lane == 1, i2.astype(F32),
           jnp.where(lane == 2, g1,
           jnp.where(lane == 3, g2,
           jnp.where(lane == 4, r1,
           jnp.where(lane == 5, r2, 0.0))))))
    meta_ref[...] = meta
    cnt_ref[...] = carry_ref[...]


def route_top2(x, w_router, tm):
    t, d = x.shape
    w_pad = jnp.pad(w_router, ((0, 0), (0, LANES - N_EXPERTS)))
    return pl.pallas_call(
        functools.partial(_router_kernel, tm=tm),
        grid=(t // tm,),
        in_specs=[pl.BlockSpec((tm, d), lambda i: (i, 0)),
                  pl.BlockSpec((d, LANES), lambda i: (0, 0))],
        out_specs=[pl.BlockSpec((tm, LANES), lambda i: (i, 0)),
                   pl.BlockSpec((1, LANES), lambda i: (0, 0))],
        out_shape=[jax.ShapeDtypeStruct((t, LANES), F32),
                   jax.ShapeDtypeStruct((1, LANES), F32)],
        scratch_shapes=[pltpu.VMEM((1, LANES), F32)],
        compiler_params=_cparams(1),
        name="route_top2",
    )(x, w_pad)


def _dispatch_kernel(pos_ref, x_ref, xs_in_hbm, xs_hbm, sem, *, tokens_per_step):
    del xs_in_hbm

    def row_copy(t, k):
        return pltpu.make_async_copy(x_ref.at[pl.ds(t, 1)],
                                     xs_hbm.at[pl.ds(pos_ref[0, 2 * t + k], 1)], sem)

    def issue(t, carry):
        row_copy(t, 0).start(priority=0)
        row_copy(t, 1).start(priority=1)
        return carry

    lax.fori_loop(0, tokens_per_step, issue, 0, unroll=8)

    def drain(t, carry):
        row_copy(t, 0).wait()
        row_copy(t, 1).wait()
        return carry

    lax.fori_loop(0, tokens_per_step, drain, 0, unroll=8)


def dispatch_rows(x, pos, n_rows, tokens_per_step):
    t, d = x.shape
    n_steps = t // tokens_per_step
    pos2 = pos.reshape(n_steps, 1, 2 * tokens_per_step)
    zeros = jnp.zeros((n_rows, d), x.dtype)
    return pl.pallas_call(
        functools.partial(_dispatch_kernel, tokens_per_step=tokens_per_step),
        grid=(n_steps,),
        in_specs=[pl.BlockSpec((None, 1, 2 * tokens_per_step), lambda i: (i, 0, 0),
                               memory_space=pltpu.SMEM),
                  pl.BlockSpec((tokens_per_step, d), lambda i: (i, 0)),
                  pl.BlockSpec(memory_space=pl.ANY)],
        out_specs=pl.BlockSpec(memory_space=pl.ANY),
        out_shape=jax.ShapeDtypeStruct((n_rows, d), x.dtype),
        scratch_shapes=[pltpu.SemaphoreType.DMA(())],
        input_output_aliases={2: 0},
        compiler_params=_cparams(1),
        name="dispatch_rows",
    )(pos2, x, zeros)


def _combine_ln_kernel(pos_ref, x_ref, gate_ref, g_ref, b_ref, ys_hbm, o_ref, buf_ref, sem, *, tm):
    def row_copy(t, k):
        return pltpu.make_async_copy(ys_hbm.at[pl.ds(pos_ref[0, 2 * t + k], 1)],
                                     buf_ref.at[k, pl.ds(t, 1)], sem)

    def issue(t, carry):
        row_copy(t, 0).start(priority=0)
        row_copy(t, 1).start(priority=1)
        return carry

    lax.fori_loop(0, tm, issue, 0, unroll=8)

    def drain(t, carry):
        row_copy(t, 0).wait()
        row_copy(t, 1).wait()
        return carry

    lax.fori_loop(0, tm, drain, 0, unroll=8)

    gates = gate_ref[...]
    y = gates[:, 2:3] * buf_ref[0] + gates[:, 3:4] * buf_ref[1]
    o_ref[...] = _layer_norm_rows(DEEPNORM_ALPHA * x_ref[...] + y, g_ref[...], b_ref[...])


def combine_ln(x, ys, pos, meta, ln_g, ln_b, tm):
    t, d = x.shape
    n_steps = t // tm
    pos2 = pos.reshape(n_steps, 1, 2 * tm)
    return pl.pallas_call(
        functools.partial(_combine_ln_kernel, tm=tm),
        grid=(n_steps,),
        in_specs=[pl.BlockSpec((None, 1, 2 * tm), lambda i: (i, 0, 0), memory_space=pltpu.SMEM),
                  pl.BlockSpec((tm, d), lambda i: (i, 0)),
                  pl.BlockSpec((tm, LANES), lambda i: (i, 0)),
                  pl.BlockSpec((1, d), lambda i: (0, 0)),
                  pl.BlockSpec((1, d), lambda i: (0, 0)),
                  pl.BlockSpec(memory_space=pl.ANY)],
        out_specs=pl.BlockSpec((tm, d), lambda i: (i, 0)),
        out_shape=jax.ShapeDtypeStruct((t, d), F32),
        scratch_shapes=[pltpu.VMEM((2, tm, d), F32), pltpu.SemaphoreType.DMA(())],
        compiler_params=_cparams(1),
        name="combine_ln",
    )(pos2, x, meta, ln_g.reshape(1, d), ln_b.reshape(1, d), ys)


def moe_ffn_ln(x, w_router, w_gu_all, w_down_all, layer, ln_g, ln_b, tm_route, tm_group, tm_combine,
               precise=False):
    t, d = x.shape
    meta, counts = route_top2(x, w_router, tm_route)
    ids = meta[:, 0:2].astype(jnp.int32)
    ranks = meta[:, 4:6].astype(jnp.int32)
    cnt = counts[0, :N_EXPERTS].astype(jnp.int32)
    tiles_per = (cnt + tm_group - 1) // tm_group
    tile_end = jnp.cumsum(tiles_per)
    group_start = (tile_end - tiles_per) * tm_group
    pos = group_start[ids] + ranks
    n_tiles = min((TOP_K * t + N_EXPERTS * (tm_group - 1)) // tm_group,
                  N_EXPERTS * ((t + tm_group - 1) // tm_group))
    n_active = tile_end[N_EXPERTS - 1:N_EXPERTS]
    tile_ids = jnp.arange(n_tiles, dtype=jnp.int32)
    tile_expert = jnp.sum((tile_ids[:, None] >= tile_end[None, :]).astype(jnp.int32), axis=1)
    tile_expert = jnp.minimum(tile_expert, N_EXPERTS - 1) + layer * N_EXPERTS
    xs = dispatch_rows(x, pos, n_tiles * tm_group, min(t, 1024))
    ys = swiglu_tiles(xs, w_gu_all, w_down_all, tile_expert.astype(jnp.int32),
                      n_active.astype(jnp.int32), ln_g, ln_b, tm_group, False, precise)
    return combine_ln(x, ys, pos, meta, ln_g, ln_b, tm_combine)


def _t5_bucket_runs():
    max_exact = N_BUCKETS // 2
    n = np.arange(MAX_DISTANCE)
    scaled = (np.log(np.maximum(n, max_exact).astype(np.float64) / max_exact)
              / math.log(MAX_DISTANCE / max_exact) * (N_BUCKETS - max_exact))
    frac = scaled - np.floor(scaled)
    assert np.all((np.minimum(frac, 1 - frac) > 1e-3) | (n <= max_exact))
    bucket = np.where(n < max_exact, n, np.minimum(max_exact + scaled.astype(np.int64), N_BUCKETS - 1))
    runs = []
    for dist in range(MAX_DISTANCE):
        if runs and runs[-1][1] == int(bucket[dist]):
            runs[-1] = (dist, int(bucket[dist]))
        else:
            runs.append((dist, int(bucket[dist])))
    return runs, bucket


T5_RUNS, T5_BUCKETS = _t5_bucket_runs()
ATT_TQ = 512
ATT_ROW_BLOCK = 512


def _bias_tile_kernel(rel_ref, o_ref):
    h = pl.program_id(0)
    tq, tk2 = o_ref.shape
    row = lax.broadcasted_iota(jnp.int32, (tq, tk2), 0)
    col = lax.broadcasted_iota(jnp.int32, (tq, tk2), 1)
    dist = row - col + tq
    far = rel_ref[N_BUCKETS - 1, h]
    acc = jnp.zeros((tq, tk2), F32)
    for last, bucket in reversed(T5_RUNS):
        acc = jnp.where(dist <= last, rel_ref[bucket, h] - far, acc)
    o_ref[...] = jnp.where(dist < 0, -jnp.inf, acc)


def attn_bias_tiles(rel_bias):
    return pl.pallas_call(
        _bias_tile_kernel,
        grid=(ATT_HEADS,),
        in_specs=[pl.BlockSpec(memory_space=pltpu.SMEM)],
        out_specs=pl.BlockSpec((None, ATT_TQ, 2 * ATT_TQ), lambda h: (h, 0, 0)),
        out_shape=jax.ShapeDtypeStruct((ATT_HEADS, ATT_TQ, 2 * ATT_TQ), F32),
        compiler_params=_cparams(1),
        name="attn_bias_tiles",
    )(rel_bias)


def _softmax_step(s, v, m_ref, l_ref, acc_ref):
    m_prev = m_ref[...]
    m_new = jnp.maximum(m_prev, jnp.max(s, axis=-1, keepdims=True))
    alpha = jnp.exp(m_prev - m_new)
    p = jnp.exp(s - jnp.concatenate([m_new] * (s.shape[1] // LANES), axis=1))
    l_ref[...] = alpha * l_ref[...] + jnp.sum(p, axis=-1, keepdims=True)
    acc_ref[...] = alpha * acc_ref[...] + jnp.dot(p.astype(BF16), v, preferred_element_type=F32)
    m_ref[...] = m_new


def _attn_prompt_kernel(lam_ref, q_ref, k_ref, v_ref, bt_ref, g_ref, o_ref,
                        kb_ref, vb_ref, m1_ref, l1_ref, a1_ref, m2_ref, l2_ref, a2_ref, *, out_scale):
    qi = pl.program_id(2)
    tq = q_ref.shape[0]
    half = ATT_HEAD_DIM

    @pl.when(qi == 0)
    def _():
        kb_ref[...] = k_ref[...].astype(BF16)
        vb_ref[...] = v_ref[...].astype(BF16)

    q = q_ref[...] * (ATT_HEAD_DIM ** -0.5)
    lane = lax.broadcasted_iota(jnp.int32, q.shape, 1)
    q1 = jnp.where(lane < half, q, 0.0).astype(BF16)
    q2 = jnp.where(lane >= half, q, 0.0).astype(BF16)

    for m_ref, l_ref, a_ref in ((m1_ref, l1_ref, a1_ref), (m2_ref, l2_ref, a2_ref)):
        m_ref[...] = jnp.full(m_ref.shape, -jnp.inf, F32)
        l_ref[...] = jnp.zeros(l_ref.shape, F32)
        a_ref[...] = jnp.zeros(a_ref.shape, F32)

    nt = (((1,), (1,)), ((), ()))
    rb = ATT_ROW_BLOCK
    stats = ((q1, m1_ref, l1_ref, a1_ref), (q2, m2_ref, l2_ref, a2_ref))

    def chunks(specs):
        ks = [kb_ref[pl.ds(start, tq), :] for start, _ in specs]
        vs = [vb_ref[pl.ds(start, tq), :] for start, _ in specs]
        units = [(ci, r, mp) for ci in range(len(specs)) for r in range(tq // rb) for mp in range(2)]

        def scores(u):
            ci, r, mp = u
            s = lax.dot_general(stats[mp][0][r * rb:(r + 1) * rb], ks[ci], nt, preferred_element_type=F32)
            bias = specs[ci][1]
            return s if bias is None else s + bias[r * rb:(r + 1) * rb, :]

        s_cur = scores(units[0])
        for i, (ci, r, mp) in enumerate(units):
            s_next = scores(units[i + 1]) if i + 1 < len(units) else None
            _, m_ref, l_ref, a_ref = stats[mp]
            rows = slice(r * rb, (r + 1) * rb)
            _softmax_step(s_cur, vs[ci], m_ref.at[rows], l_ref.at[rows], a_ref.at[rows])
            s_cur = s_next

    n_far = qi - 1

    def far_pair(c, carry):
        chunks([(pl.multiple_of(2 * c * tq, tq), None), (pl.multiple_of((2 * c + 1) * tq, tq), None)])
        return carry

    lax.fori_loop(0, n_far // 2, far_pair, 0)

    @pl.when(jnp.logical_and(n_far > 0, n_far % 2 == 1))
    def _():
        chunks([(pl.multiple_of((n_far - 1) * tq, tq), None)])

    @pl.when(qi > 0)
    def _():
        chunks([(pl.multiple_of((qi - 1) * tq, tq), bt_ref[:, 0:tq]),
                (pl.multiple_of(qi * tq, tq), bt_ref[:, tq:2 * tq])])

    @pl.when(qi == 0)
    def _():
        chunks([(0, bt_ref[:, tq:2 * tq])])

    o = a1_ref[...] / l1_ref[...] - lam_ref[0] * (a2_ref[...] / l2_ref[...])
    o = o * lax.rsqrt(jnp.mean(o * o, axis=-1, keepdims=True) + RMS_EPS) * g_ref[...]
    o_ref[...] = o * out_scale


def attn_prompt(qkv, lam, lam_init, bias_tiles, subln_g):
    bn, l, _ = qkv.shape
    hd = 2 * ATT_HEAD_DIM
    tq = ATT_TQ
    stat = pltpu.VMEM((tq, hd), F32)
    return pl.pallas_call(
        functools.partial(_attn_prompt_kernel, out_scale=1.0 - lam_init),
        grid=(bn, ATT_HEADS, l // tq),
        in_specs=[pl.BlockSpec(memory_space=pltpu.SMEM),
                  pl.BlockSpec((None, tq, hd), lambda b, h, i: (b, i, h)),
                  pl.BlockSpec((None, l, hd), lambda b, h, i: (b, 0, ATT_HEADS + h)),
                  pl.BlockSpec((None, l, hd), lambda b, h, i: (b, 0, 2 * ATT_HEADS + h)),
                  pl.BlockSpec((None, tq, 2 * tq), lambda b, h, i: (h, 0, 0)),
                  pl.BlockSpec((1, hd), lambda b, h, i: (0, 0))],
        out_specs=pl.BlockSpec((None, tq, hd), lambda b, h, i: (b, i, h)),
        out_shape=jax.ShapeDtypeStruct((bn, l, ATT_HEADS * hd), F32),
        scratch_shapes=[pltpu.VMEM((l, hd), BF16), pltpu.VMEM((l, hd), BF16),
                        stat, stat, stat, stat, stat, stat],
        compiler_params=_cparams(3),
        name="attn_prompt",
    )(lam.reshape(1), qkv, qkv, qkv, bias_tiles, subln_g.reshape(1, hd))


ATT_PAGES_PER_STEP = 8
ATT_ROWS = 2 * ATT_HEADS


def _row_select(vec_row, h, lane_lo, lane_hi):
    rid = lax.broadcasted_iota(jnp.int32, (ATT_ROWS, LANES), 0)
    lane = lax.broadcasted_iota(jnp.int32, (ATT_ROWS, LANES), 1)
    full = jnp.broadcast_to(vec_row, (ATT_ROWS, LANES))
    keep = jnp.logical_or(jnp.logical_and(rid == 2 * h, lane < lane_hi),
                          jnp.logical_and(rid == 2 * h + 1, lane >= lane_lo))
    return jnp.where(keep, full, 0.0)


def _attn_decode_kernel(pt_ref, lam_ref, qkv_ref, bl_ref, bc_ref, g_ref, *refs, out_scale):
    npg = ATT_PAGES_PER_STEP
    k_refs, v_refs = refs[:npg], refs[npg:2 * npg]
    o_ref, qs_ref, m_ref, l_ref, acc_ref = refs[2 * npg:]
    del pt_ref
    step = pl.program_id(1)
    last = pl.num_programs(1) - 1
    rid = lax.broadcasted_iota(jnp.int32, (ATT_ROWS, LANES), 0)
    half = ATT_HEAD_DIM
    hd = 2 * ATT_HEAD_DIM

    @pl.when(step == 0)
    def _():
        m_ref[...] = jnp.full(m_ref.shape, -jnp.inf, F32)
        l_ref[...] = jnp.zeros(l_ref.shape, F32)
        acc_ref[...] = jnp.zeros(acc_ref.shape, F32)
        for h in range(ATT_HEADS):
            qh = qkv_ref[:, h * hd:(h + 1) * hd] * (ATT_HEAD_DIM ** -0.5)
            qs_ref[h] = _row_select(qh, h, half, half).astype(BF16)

    def head_rows(page_refs, h):
        return jnp.concatenate(
            [r[pl.ds(h, PAGE_SIZE, stride=ATT_HEADS), :].astype(BF16) for r in page_refs], axis=0)

    s = None
    for h in range(ATT_HEADS):
        t = lax.dot_general(qs_ref[h], head_rows(k_refs, h), _NT, preferred_element_type=F32)
        s = t if s is None else s + t
    tail_bias = jnp.where(step == last, 1.0, 0.0) * bl_ref[...]
    s = jnp.concatenate([s[:, :(npg - 1) * PAGE_SIZE], s[:, (npg - 1) * PAGE_SIZE:] + tail_bias], axis=1)

    m_prev = m_ref[...]
    m_new = jnp.maximum(m_prev, jnp.max(s, axis=-1, keepdims=True))
    alpha = jnp.exp(m_prev - m_new)
    p = jnp.exp(s - m_new[:, 0:1])
    l_ref[...] = alpha * l_ref[...] + jnp.sum(p, axis=-1, keepdims=True)
    pb = p.astype(BF16)
    pv = None
    for h in range(ATT_HEADS):
        t = jnp.dot(pb, head_rows(v_refs, h), preferred_element_type=F32)
        t = jnp.where(jnp.logical_or(rid == 2 * h, rid == 2 * h + 1), t, 0.0)
        pv = t if pv is None else pv + t
    acc_ref[...] = alpha * acc_ref[...] + pv
    m_ref[...] = m_new

    @pl.when(step == last)
    def _():
        s_cur = None
        v_rows = None
        for h in range(ATT_HEADS):
            kn = qkv_ref[:, D_MODEL + h * hd:D_MODEL + (h + 1) * hd].astype(BF16).astype(F32)
            vn = qkv_ref[:, 2 * D_MODEL + h * hd:2 * D_MODEL + (h + 1) * hd].astype(BF16).astype(F32)
            t = jnp.sum(qs_ref[h].astype(F32) * kn, axis=-1, keepdims=True)
            s_cur = t if s_cur is None else s_cur + t
            vr = _row_select(vn, h, 0, LANES)
            v_rows = vr if v_rows is None else v_rows + vr
        s_cur = s_cur + bc_ref[...][:, 0:1]
        m_prev = m_ref[...]
        m_new = jnp.maximum(m_prev, s_cur)
        alpha = jnp.exp(m_prev - m_new)
        p = jnp.exp(s_cur - m_new[:, 0:1])
        l_fin = alpha * l_ref[...] + p
        acc = alpha * acc_ref[...] + p.astype(BF16).astype(F32) * v_rows
        o = acc / l_fin
        er = lax.broadcasted_iota(jnp.int32, (ATT_HEADS, ATT_ROWS), 0)
        ec = lax.broadcasted_iota(jnp.int32, (ATT_HEADS, ATT_ROWS), 1)
        pick1 = jnp.where(ec == 2 * er, 1.0, 0.0).astype(BF16)
        pick2 = jnp.where(ec == 2 * er + 1, 1.0, 0.0).astype(BF16)
        hi, mid, lo = _split3(o)
        o1 = (jnp.dot(pick1, hi, preferred_element_type=F32) + jnp.dot(pick1, mid, preferred_element_type=F32)
              + jnp.dot(pick1, lo, preferred_element_type=F32))
        o2 = (jnp.dot(pick2, hi, preferred_element_type=F32) + jnp.dot(pick2, mid, preferred_element_type=F32)
              + jnp.dot(pick2, lo, preferred_element_type=F32))
        d = o1 - lam_ref[0] * o2
        d = d * lax.rsqrt(jnp.mean(d * d, axis=-1, keepdims=True) + RMS_EPS) * g_ref[...]
        o_ref[...] = d * out_scale


def attn_decode(qkv, k_pages, v_pages, layer, page_table, lam, lam_init, rel_bias, subln_g):
    bn = qkv.shape[0]
    n_pages = page_table.shape[1]
    npg = ATT_PAGES_PER_STEP
    hd = 2 * ATT_HEAD_DIM
    dist = PAGE_SIZE - np.arange(PAGE_SIZE)
    bucket = np.where(dist < MAX_DISTANCE, T5_BUCKETS[np.minimum(dist, MAX_DISTANCE - 1)], N_BUCKETS - 1)
    far = rel_bias[N_BUCKETS - 1]
    bias_last = jnp.repeat((rel_bias[bucket] - far).T, 2, axis=0)
    bias_cur = jnp.broadcast_to(jnp.repeat(rel_bias[0] - far, 2)[:, None], (ATT_ROWS, LANES))

    def rows_of(pages):
        return pages.reshape(pages.shape[0], pages.shape[1], PAGE_SIZE * ATT_HEADS, hd)

    def page_spec(pg):
        return pl.BlockSpec((None, None, PAGE_SIZE * ATT_HEADS, hd),
                            lambda b, s, pt: (layer, pt[b, s * npg + pg], 0, 0))

    full = lambda shape: pl.BlockSpec(shape, lambda b, s, pt: (0,) * len(shape))
    stat = pltpu.VMEM((ATT_ROWS, LANES), F32)
    grid_spec = pltpu.PrefetchScalarGridSpec(
        num_scalar_prefetch=1,
        grid=(bn, n_pages // npg),
        in_specs=[pl.BlockSpec(memory_space=pltpu.SMEM),
                  pl.BlockSpec((None, 1, 3 * D_MODEL), lambda b, s, pt: (b, 0, 0)),
                  full((ATT_ROWS, PAGE_SIZE)), full((ATT_ROWS, LANES)), full((1, hd))]
                 + [page_spec(pg) for pg in range(npg)] * 2,
        out_specs=pl.BlockSpec((None, ATT_HEADS, hd), lambda b, s, pt: (b, 0, 0)),
        scratch_shapes=[pltpu.VMEM((ATT_HEADS, ATT_ROWS, LANES), BF16), stat, stat, stat],
    )
    out = pl.pallas_call(
        functools.partial(_attn_decode_kernel, out_scale=1.0 - lam_init),
        grid_spec=grid_spec,
        out_shape=jax.ShapeDtypeStruct((bn, ATT_HEADS, hd), F32),
        compiler_params=_cparams(2),
        name="attn_decode",
    )(page_table, lam.reshape(1), qkv.reshape(bn, 1, 3 * D_MODEL), bias_last, bias_cur,
      subln_g.reshape(1, hd), *([rows_of(k_pages)] * npg), *([rows_of(v_pages)] * npg))
    return out.reshape(bn, D_MODEL)


def _split3(v):
    hi = v.astype(BF16)
    r1 = v - hi.astype(F32)
    mid = r1.astype(BF16)
    lo = (r1 - mid.astype(F32)).astype(BF16)
    return hi, mid, lo


def _dot3(v, w_bf16, dims=None):
    out = None
    for piece in _split3(v):
        if dims is None:
            t = jnp.dot(piece, w_bf16, preferred_element_type=F32)
        else:
            t = lax.dot_general(piece, w_bf16, dims, preferred_element_type=F32)
        out = t if out is None else out + t
    return out


def _softplus(x):
    return jnp.maximum(x, 0.0) + jnp.log1p(jnp.exp(-jnp.abs(x)))


_NT = (((1,), (1,)), ((), ()))
_TN = (((0,), (0,)), ((), ()))
SSM_GROUP_W = SSM_HPG * SSM_HEAD_DIM
SSM_GN = SSM_GROUPS * SSM_STATE


def _ssd_prompt_kernel(proj_ref, dt_ref, cw_ref, cb_ref, dtb_ref, aneg_ref, dskip_ref, ng_ref, e_ref,
                       y_ref, h_ref, xp_ref, ht_ref):
    ci = pl.program_id(1)
    c = SSM_CHUNK
    di = SSM_D_INNER

    @pl.when(ci == 0)
    def _():
        xp_ref[0:8, :] = jnp.zeros((8, SSM_CONV_DIM), F32)
        ht_ref[...] = jnp.zeros(ht_ref.shape, F32)

    xbc = proj_ref[:, di:di + SSM_CONV_DIM]
    xp_ref[8:8 + c, :] = xbc
    conv = cb_ref[...] + xp_ref[5:5 + c, :] * cw_ref[0:1, :]
    for i in range(1, CONV_W):
        conv = conv + xp_ref[5 + i:5 + i + c, :] * cw_ref[i:i + 1, :]
    xp_ref[0:8, :] = xbc[c - 8:c, :]
    act = _silu(conv)
    xs = act[:, :di]

    dt = _softplus(dt_ref[...] + dtb_ref[...])
    a = dt * aneg_ref[...]
    row = lax.broadcasted_iota(jnp.int32, (c, c), 0)
    col = lax.broadcasted_iota(jnp.int32, (c, c), 1)
    incl = col <= row
    tri = jnp.where(incl, 1.0, 0.0).astype(BF16)
    hi, mid, lo = _split3(a)
    acum = (jnp.dot(tri, hi, preferred_element_type=F32) + jnp.dot(tri, mid, preferred_element_type=F32)
            + jnp.dot(tri, lo, preferred_element_type=F32))
    upper = jnp.where(row <= col, 1.0, 0.0).astype(BF16)
    acum_t = _dot3(a, upper, _TN)
    a_end = acum[c - 1:c, :]
    eac = jnp.exp(acum)
    wend = jnp.exp(a_end - acum) * dt
    ex = _dot3(jnp.concatenate([dt, eac, wend], axis=0), e_ref[...])
    dt_x, eac_x, wend_x = ex[0:c], ex[c:2 * c], ex[2 * c:3 * c]
    xdt = (xs * dt_x).astype(BF16)
    xw = (xs * wend_x).astype(BF16)
    lane = lax.broadcasted_iota(jnp.int32, (c, LANES), 1)
    lo_half = lane < SSM_HEAD_DIM
    neg_inf = jnp.float32(-jnp.inf)

    for g in range(SSM_GROUPS):
        gs = slice(g * SSM_GROUP_W, (g + 1) * SSM_GROUP_W)
        bg = act[:, di + g * SSM_STATE:di + (g + 1) * SSM_STATE].astype(BF16)
        cg = act[:, di + SSM_GN + g * SSM_STATE:di + SSM_GN + (g + 1) * SSM_STATE].astype(BF16)
        cb = lax.dot_general(cg, bg, _NT, preferred_element_type=F32)
        pieces = []
        for pr in range(SSM_GROUP_W // LANES):
            pair = g * (SSM_GROUP_W // LANES) + pr
            xd = xdt[:, pair * LANES:(pair + 1) * LANES]
            acc = None
            for half in range(2):
                h = 2 * pair + half
                diff = acum[:, h:h + 1] - acum_t[h:h + 1, :]
                s = (cb * jnp.exp(jnp.where(incl, diff, neg_inf))).astype(BF16)
                xm = jnp.where(lo_half if half == 0 else jnp.logical_not(lo_half), xd, jnp.zeros_like(xd))
                t = jnp.dot(s, xm, preferred_element_type=F32)
                acc = t if acc is None else acc + t
            pieces.append(acc)
        y_intra = jnp.concatenate(pieces, axis=1)
        h_old = ht_ref[:, gs]
        y_inter = jnp.dot(cg, h_old.astype(BF16), preferred_element_type=F32) * eac_x[:, gs]
        dstate = lax.dot_general(bg, xw[:, gs], _TN, preferred_element_type=F32)
        ht_ref[:, gs] = h_old * eac_x[c - 1:c, gs] + dstate
        y = y_intra + y_inter + dskip_ref[:, gs] * xs[:, gs]
        y = y * _silu(proj_ref[:, gs])
        y = y * lax.rsqrt(jnp.mean(y * y, axis=-1, keepdims=True) + RMS_EPS) * ng_ref[:, gs]
        y_ref[:, gs] = y

    @pl.when(ci == pl.num_programs(1) - 1)
    def _():
        h_ref[...] = ht_ref[...].T


def _head_expander():
    e = np.zeros((LANES, SSM_D_INNER), np.float32)
    for h in range(SSM_HEADS):
        e[h, h * SSM_HEAD_DIM:(h + 1) * SSM_HEAD_DIM] = 1.0
    return jnp.asarray(e, BF16)


def _pad_lanes(v):
    return jnp.pad(v.astype(F32), (0, LANES - v.shape[0])).reshape(1, LANES)


def ssd_prompt(proj, dtp, conv_w, conv_b, a_log, dt_bias, d_skip, norm_g):
    bn, l, pw = proj.shape
    c = SSM_CHUNK
    di = SSM_D_INNER
    full = lambda shape: pl.BlockSpec(shape, lambda b, i: (0,) * len(shape))
    y, h = pl.pallas_call(
        _ssd_prompt_kernel,
        grid=(bn, l // c),
        in_specs=[pl.BlockSpec((None, c, pw), lambda b, i: (b, i, 0)),
                  pl.BlockSpec((None, c, LANES), lambda b, i: (b, i, 0)),
                  full((CONV_W, SSM_CONV_DIM)), full((1, SSM_CONV_DIM)),
                  full((1, LANES)), full((1, LANES)), full((1, di)), full((1, di)),
                  full((LANES, di))],
        out_specs=[pl.BlockSpec((None, c, di), lambda b, i: (b, i, 0)),
                   pl.BlockSpec((None, di, SSM_STATE), lambda b, i: (b, 0, 0))],
        out_shape=[jax.ShapeDtypeStruct((bn, l, di), F32),
                   jax.ShapeDtypeStruct((bn, di, SSM_STATE), F32)],
        scratch_shapes=[pltpu.VMEM((8 + c, SSM_CONV_DIM), F32), pltpu.VMEM((SSM_STATE, di), F32)],
        compiler_params=_cparams(2),
        name="ssd_prompt",
    )(proj, dtp, conv_w, conv_b.reshape(1, -1), _pad_lanes(dt_bias), _pad_lanes(-jnp.exp(a_log)),
      jnp.repeat(d_skip, SSM_HEAD_DIM).reshape(1, di), norm_g.reshape(1, di), _head_expander())
    return y, h.reshape(bn, SSM_HEADS, SSM_HEAD_DIM, SSM_STATE)


GDN_BLK = 512
GDN_HEADS_PER_STEP = 4


def _split2(v):
    hi = v.astype(BF16)
    return hi, (v - hi.astype(F32)).astype(BF16)


def _mm_split(a, b):
    (a_hi, a_lo), (b_hi, b_lo) = a, b
    return (jnp.dot(a_hi, b_hi, preferred_element_type=F32) + jnp.dot(a_hi, b_lo, preferred_element_type=F32)
            + jnp.dot(a_lo, b_hi, preferred_element_type=F32))


def _conv_silu_block(x_ref, xp_ref, cw_ref):
    n = x_ref.shape[0]
    x = x_ref[...]
    xp_ref[8:8 + n, :] = x
    y = xp_ref[5:5 + n, :] * cw_ref[0:1, :]
    for i in range(1, CONV_W):
        y = y + xp_ref[5 + i:5 + i + n, :] * cw_ref[i:i + 1, :]
    xp_ref[0:8, :] = x[n - 8:n, :]
    return _silu(y)


def _gdn_prompt_kernel(alog_ref, dtb_ref, q_ref, k_ref, v_ref, z_ref, ba_ref, cwq_ref, cwk_ref, cwv_ref,
                       ng_ref, tri_ref, upper_ref, o_ref, s_out_ref, xq_ref, xk_ref, xv_ref, s_ref):
    hp = pl.program_id(1)
    bi = pl.program_id(2)
    blk = q_ref.shape[0]
    c = GDN_CHUNK
    hd = GDN_DK
    nhs = GDN_HEADS_PER_STEP

    @pl.when(bi == 0)
    def _():
        for r in (xq_ref, xk_ref, xv_ref):
            r[0:8, :] = jnp.zeros((8, r.shape[1]), F32)
        s_ref[...] = jnp.zeros(s_ref.shape, F32)

    q_all = _conv_silu_block(q_ref, xq_ref, cwq_ref)
    k_all = _conv_silu_block(k_ref, xk_ref, cwk_ref)
    v_all = _conv_silu_block(v_ref, xv_ref, cwv_ref)
    lane = lax.broadcasted_iota(jnp.int32, (blk, LANES), 1)
    ba = ba_ref[...]
    tri_bd = tri_ref[...]
    upper_bd = upper_ref[...]

    row = lax.broadcasted_iota(jnp.int32, (c, c), 0)
    col = lax.broadcasted_iota(jnp.int32, (c, c), 1)
    incl = col <= row
    strict = col < row
    eye = jnp.where(row == col, 1.0, 0.0)
    neg_inf = jnp.float32(-jnp.inf)
    nch = blk // c

    qh, kh, vh, betah, gcolh, growh, egh = [], [], [], [], [], [], []
    for e in range(nhs):
        h = hp * nhs + e
        hs = slice(e * hd, (e + 1) * hd)
        q = q_all[:, hs]
        k = k_all[:, hs]
        qh.append(q * lax.rsqrt(jnp.sum(q * q, axis=-1, keepdims=True) + 1e-6) * (GDN_DK ** -0.5))
        kh.append(k * lax.rsqrt(jnp.sum(k * k, axis=-1, keepdims=True) + 1e-6))
        vh.append(v_all[:, hs])
        b_col = jnp.sum(jnp.where(lane == h, ba, 0.0), axis=-1, keepdims=True)
        a_col = jnp.sum(jnp.where(lane == GDN_HEADS + h, ba, 0.0), axis=-1, keepdims=True)
        betah.append(1.0 / (1.0 + jnp.exp(-b_col)))
        g = -jnp.exp(alog_ref[h]) * _softplus(a_col + dtb_ref[h])
        g_rep = jnp.broadcast_to(g, (blk, LANES))
        hi, mid, lo = _split3(g_rep)
        gcolh.append(jnp.dot(tri_bd, hi, preferred_element_type=F32)
                     + jnp.dot(tri_bd, mid, preferred_element_type=F32)
                     + jnp.dot(tri_bd, lo, preferred_element_type=F32))
        growh.append(_dot3(g_rep[:, :c], upper_bd, _TN))
        egh.append(jnp.exp(gcolh[e][:, 0:1]))

    units = [(e, n) for n in range(nch) for e in range(nhs)]
    rws = lambda n: slice(n * c, (n + 1) * c)
    decays = [jnp.exp(jnp.where(incl, gcolh[e][rws(n), 0:c] - growh[e][:, n * LANES:n * LANES + c], neg_inf))
              for e, n in units]
    kbs = [kh[e][rws(n)].astype(BF16) for e, n in units]
    kks = [lax.dot_general(kb, kb, _NT, preferred_element_type=F32) for kb in kbs]
    qks = [lax.dot_general(qh[e][rws(n)].astype(BF16), kbs[u], _NT, preferred_element_type=F32) * decays[u]
           for u, (e, n) in enumerate(units)]
    mps = [jnp.where(strict, betah[e][rws(n)] * kks[u] * decays[u], 0.0) for u, (e, n) in enumerate(units)]
    ps = [eye - m for m in mps]
    for _ in range(5):
        mps = [_mm_split(_split2(mp), _split2(mp)) for mp in mps]
        mp_parts = [_split2(mp) for mp in mps]
        ps = [p + _mm_split(_split2(p), mpp) for p, mpp in zip(ps, mp_parts)]
    rhs = [jnp.concatenate([betah[e][rws(n)] * vh[e][rws(n)],
                            (betah[e][rws(n)] * egh[e][rws(n)]) * kh[e][rws(n)]], axis=1) for e, n in units]
    ws = [_mm_split(_split2(p), _split2(x)) for p, x in zip(ps, rhs)]
    uvs = [w[:, :GDN_DV].astype(BF16) for w in ws]
    uks = [w[:, GDN_DV:].astype(BF16) for w in ws]
    g_last = [gcolh[e][n * c + c - 1:n * c + c, 0:1] for e, n in units]
    kes = [(kh[e][rws(n)] * jnp.exp(g_last[u] - gcolh[e][rws(n), 0:1])).astype(BF16)
           for u, (e, n) in enumerate(units)]
    qkb = [x.astype(BF16) for x in qks]
    kmats = [lax.dot_general(kes[u], uks[u], _TN, preferred_element_type=F32).astype(BF16)
             for u in range(len(units))]
    hmats = [lax.dot_general(kes[u], uvs[u], _TN, preferred_element_type=F32) for u in range(len(units))]
    qps = [(qh[e][rws(n)] * egh[e][rws(n)] - jnp.dot(qkb[u], uks[u], preferred_element_type=F32)).astype(BF16)
           for u, (e, n) in enumerate(units)]
    ops = [jnp.dot(qkb[u], uvs[u], preferred_element_type=F32) for u in range(len(units))]

    states = [s_ref[e] for e in range(nhs)]
    for u, (e, n) in enumerate(units):
        hs = slice(e * hd, (e + 1) * hd)
        sb = states[e].astype(BF16)
        o = jnp.dot(qps[u], sb, preferred_element_type=F32) + ops[u]
        states[e] = (jnp.exp(g_last[u]) * states[e] - jnp.dot(kmats[u], sb, preferred_element_type=F32)
                     + hmats[u])
        o = o * lax.rsqrt(jnp.mean(o * o, axis=-1, keepdims=True) + RMS_EPS) * ng_ref[...]
        o_ref[rws(n), hs] = o * _silu(z_ref[rws(n), hs])
    for e in range(nhs):
        s_ref[e] = states[e]

    @pl.when(bi == pl.num_programs(2) - 1)
    def _():
        for e in range(nhs):
            s_out_ref[e] = states[e]


def gdn_prompt(proj, ba, conv_w, a_log, dt_bias, norm_g):
    bn, l, _ = proj.shape
    blk = GDN_BLK
    nh = GDN_HEADS
    hd = GDN_DK
    nhs = GDN_HEADS_PER_STEP
    hw = nhs * hd
    ngrp = nh // nhs
    col = lambda part: pl.BlockSpec((None, blk, hw), lambda b, h, i: (b, i, part * ngrp + h))
    cw = lambda part: pl.BlockSpec((CONV_W, hw), lambda b, h, i: (0, part * ngrp + h))
    tail = pltpu.VMEM((8 + blk, hw), F32)
    nch = blk // GDN_CHUNK
    t = np.arange(blk)
    same = (t[:, None] // GDN_CHUNK) == (t[None, :] // GDN_CHUNK)
    tri_bd = jnp.asarray(same & (t[None, :] <= t[:, None]), BF16)
    jj = np.arange(nch * LANES)
    upper = ((t[:, None] // GDN_CHUNK) == (jj[None, :] // LANES)) & \
            ((t[:, None] % GDN_CHUNK) <= (jj[None, :] % LANES)) & ((jj[None, :] % LANES) < GDN_CHUNK)
    upper_bd = jnp.asarray(upper, BF16)
    o, s = pl.pallas_call(
        _gdn_prompt_kernel,
        grid=(bn, ngrp, l // blk),
        in_specs=[pl.BlockSpec(memory_space=pltpu.SMEM), pl.BlockSpec(memory_space=pltpu.SMEM),
                  col(0), col(1), col(2), col(3),
                  pl.BlockSpec((None, blk, LANES), lambda b, h, i: (b, i, 0)),
                  cw(0), cw(1), cw(2),
                  pl.BlockSpec((1, hd), lambda b, h, i: (0, 0)),
                  pl.BlockSpec((blk, blk), lambda b, h, i: (0, 0)),
                  pl.BlockSpec((blk, nch * LANES), lambda b, h, i: (0, 0))],
        out_specs=[pl.BlockSpec((None, blk, hw), lambda b, h, i: (b, i, h)),
                   pl.BlockSpec((None, nhs, hd, hd), lambda b, h, i: (b, h, 0, 0))],
        out_shape=[jax.ShapeDtypeStruct((bn, l, nh * hd), F32),
                   jax.ShapeDtypeStruct((bn, nh, hd, hd), F32)],
        scratch_shapes=[tail, tail, tail, pltpu.VMEM((nhs, hd, hd), F32)],
        compiler_params=_cparams(3),
        name="gdn_prompt",
    )(a_log.astype(F32), dt_bias.astype(F32), proj, proj, proj, proj, ba, conv_w, conv_w, conv_w,
      norm_g.reshape(1, hd), tri_bd, upper_bd)
    return o, s


SUB = 16


def _rows8(row, zero_fill, dtype=F32):
    full = jnp.broadcast_to(row.astype(F32), (SUB, row.shape[1]))
    if zero_fill:
        rid = lax.broadcasted_iota(jnp.int32, full.shape, 0)
        full = jnp.where(rid == 0, full, 0.0)
    return full.astype(dtype)


def _conv_step(x_row, c_ref, cw_ref, cb_row):
    y = x_row * cw_ref[CONV_W - 1:CONV_W, :]
    for i in range(CONV_W - 1):
        y = y + c_ref[i:i + 1, :] * cw_ref[i:i + 1, :]
    if cb_row is not None:
        y = y + cb_row
    return _silu(y)


def _gdn_step_kernel(alog_ref, dtb_ref, proj_ref, ba_ref, c_ref, cw_ref, ng_ref, s_ref, o_ref, s_out_ref):
    hd = GDN_DK
    act = _conv_step(proj_ref[:, :GDN_CONV_DIM], c_ref, cw_ref, None)
    ba = ba_ref[...]
    qs, kbs, us, egs, qks = [], [], [], [], []
    for h in range(GDN_HEADS):
        q = act[:, h * hd:(h + 1) * hd]
        k = act[:, GDN_KD + h * hd:GDN_KD + (h + 1) * hd]
        q = q * lax.rsqrt(jnp.sum(q * q, axis=-1, keepdims=True) + 1e-6) * (GDN_DK ** -0.5)
        k = k * lax.rsqrt(jnp.sum(k * k, axis=-1, keepdims=True) + 1e-6)
        qs.append(q)
        kbs.append(k)
        beta = 1.0 / (1.0 + jnp.exp(-ba[:, h:h + 1]))
        g = -jnp.exp(alog_ref[h]) * _softplus(ba[:, GDN_HEADS + h:GDN_HEADS + h + 1] + dtb_ref[h])
        egs.append(jnp.exp(g))
        us.append(beta)
        qks.append(jnp.sum(q * k, axis=-1, keepdims=True))
    ks_ = [jnp.dot(_rows8(kbs[h], False), s_ref[h], preferred_element_type=F32, precision=_HIGHEST)[0:1]
           for h in range(GDN_HEADS)]
    qs_ = [jnp.dot(_rows8(qs[h], False), s_ref[h], preferred_element_type=F32, precision=_HIGHEST)[0:1]
           for h in range(GDN_HEADS)]
    for h in range(GDN_HEADS):
        v = act[:, 2 * GDN_KD + h * hd:2 * GDN_KD + (h + 1) * hd]
        u = us[h] * (v - egs[h] * ks_[h])
        o = egs[h] * qs_[h] + qks[h] * u
        outer = lax.dot_general(_rows8(kbs[h], True), _rows8(u, False), _TN,
                                preferred_element_type=F32, precision=_HIGHEST)
        s_out_ref[h] = egs[h] * s_ref[h] + outer
        o = o * lax.rsqrt(jnp.mean(o * o, axis=-1, keepdims=True) + RMS_EPS) * ng_ref[...]
        z = proj_ref[:, GDN_CONV_DIM + h * hd:GDN_CONV_DIM + (h + 1) * hd]
        o_ref[:, h * hd:(h + 1) * hd] = o * _silu(z)


def gdn_step(proj, ba, s0, conv0, conv_w, a_log, dt_bias, norm_g):
    bn, pw = proj.shape
    hd = GDN_DK
    o, s = pl.pallas_call(
        _gdn_step_kernel,
        grid=(bn,),
        in_specs=[pl.BlockSpec(memory_space=pltpu.SMEM), pl.BlockSpec(memory_space=pltpu.SMEM),
                  pl.BlockSpec((None, 1, pw), lambda b: (b, 0, 0)),
                  pl.BlockSpec((None, 1, LANES), lambda b: (b, 0, 0)),
                  pl.BlockSpec((None, CONV_W - 1, GDN_CONV_DIM), lambda b: (b, 0, 0)),
                  pl.BlockSpec((CONV_W, GDN_CONV_DIM), lambda b: (0, 0)),
                  pl.BlockSpec((1, hd), lambda b: (0, 0)),
                  pl.BlockSpec((None, GDN_HEADS, hd, hd), lambda b: (b, 0, 0, 0))],
        out_specs=[pl.BlockSpec((None, 1, GDN_VD), lambda b: (b, 0, 0)),
                   pl.BlockSpec((None, GDN_HEADS, hd, hd), lambda b: (b, 0, 0, 0))],
        out_shape=[jax.ShapeDtypeStruct((bn, 1, GDN_VD), F32),
                   jax.ShapeDtypeStruct(s0.shape, F32)],
        compiler_params=_cparams(1),
        name="gdn_step",
    )(a_log.astype(F32), dt_bias.astype(F32), proj.reshape(bn, 1, pw), ba.reshape(bn, 1, LANES),
      conv0, conv_w, norm_g.reshape(1, hd), s0)
    return o.reshape(bn, GDN_VD), s


def _ssd_step_kernel(proj_ref, dt_ref, c_ref, cw_ref, cb_ref, dtb_ref, aneg_ref, dskip_ref, ng_ref, e_ref,
                     h_ref, y_ref, h_out_ref):
    di = SSM_D_INNER
    act = _conv_step(proj_ref[:, di:di + SSM_CONV_DIM], c_ref, cw_ref, cb_ref[...])
    xs = act[:, :di]
    dt = _softplus(dt_ref[...] + dtb_ref[...])
    ea = jnp.exp(dt * aneg_ref[...])
    ex = _dot3(jnp.concatenate([_rows8(dt, True), _rows8(ea, True)], axis=0), e_ref[...])
    dt_x, ea_x = ex[0:1], ex[SUB:SUB + 1]
    xdt = xs * dt_x
    ones = jnp.ones((SUB, SSM_STATE), BF16)
    for g in range(SSM_GROUPS):
        gs = slice(g * SSM_GROUP_W, (g + 1) * SSM_GROUP_W)
        bg = act[:, di + g * SSM_STATE:di + (g + 1) * SSM_STATE]
        cg = act[:, di + SSM_GN + g * SSM_STATE:di + SSM_GN + (g + 1) * SSM_STATE]
        dstate = lax.dot_general(_rows8(xdt[:, gs], True), _rows8(bg, False), _TN,
                                 preferred_element_type=F32, precision=_HIGHEST)
        decay = _dot3(_rows8(ea_x[:, gs], True), ones, _TN)
        h_old = h_ref[g * SSM_HPG:(g + 1) * SSM_HPG].reshape(SSM_GROUP_W, SSM_STATE)
        h_new = decay * h_old + dstate
        h_out_ref[g * SSM_HPG:(g + 1) * SSM_HPG] = h_new.reshape(SSM_HPG, SSM_HEAD_DIM, SSM_STATE)
        y = lax.dot_general(_rows8(cg, False), h_new, _NT, preferred_element_type=F32,
                            precision=_HIGHEST)[0:1]
        y = y + dskip_ref[:, gs] * xs[:, gs]
        y = y * _silu(proj_ref[:, gs])
        y = y * lax.rsqrt(jnp.mean(y * y, axis=-1, keepdims=True) + RMS_EPS) * ng_ref[:, gs]
        y_ref[:, gs] = y


def ssd_step(proj, dtp, h0, conv0, conv_w, conv_b, a_log, dt_bias, d_skip, norm_g):
    bn, pw = proj.shape
    di = SSM_D_INNER
    full = lambda shape: pl.BlockSpec(shape, lambda b: (0,) * len(shape))
    hblk = pl.BlockSpec((None, SSM_HEADS, SSM_HEAD_DIM, SSM_STATE), lambda b: (b, 0, 0, 0))
    y, h = pl.pallas_call(
        _ssd_step_kernel,
        grid=(bn,),
        in_specs=[pl.BlockSpec((None, 1, pw), lambda b: (b, 0, 0)),
                  pl.BlockSpec((None, 1, LANES), lambda b: (b, 0, 0)),
                  pl.BlockSpec((None, CONV_W - 1, SSM_CONV_DIM), lambda b: (b, 0, 0)),
                  full((CONV_W, SSM_CONV_DIM)), full((1, SSM_CONV_DIM)),
                  full((1, LANES)), full((1, LANES)), full((1, di)), full((1, di)), full((LANES, di)),
                  hblk],
        out_specs=[pl.BlockSpec((None, 1, di), lambda b: (b, 0, 0)), hblk],
        out_shape=[jax.ShapeDtypeStruct((bn, 1, di), F32), jax.ShapeDtypeStruct(h0.shape, F32)],
        compiler_params=_cparams(1),
        name="ssd_step",
    )(proj.reshape(bn, 1, pw), dtp.reshape(bn, 1, LANES), conv0, conv_w, conv_b.reshape(1, -1),
      _pad_lanes(dt_bias), _pad_lanes(-jnp.exp(a_log)), jnp.repeat(d_skip, SSM_HEAD_DIM).reshape(1, di),
      norm_g.reshape(1, di), _head_expander(), h0)
    return y.reshape(bn, di), h


def _pad_cols(w):
    return jnp.pad(w, ((0, 0), (0, LANES - w.shape[1])))


def kernel(x_prompt, x_sample, state_gdn_S, state_gdn_conv, state_ssm_h, state_ssm_conv,
           cache_attn_k, cache_attn_v, page_table,
           gdn_w_in, gdn_conv_w, gdn_a_log, gdn_dt_bias, gdn_norm_g, gdn_w_out,
           ssm_w_in, ssm_conv_w, ssm_conv_b, ssm_a_log, ssm_dt_bias, ssm_d_skip, ssm_norm_g, ssm_w_out,
           att_w_qkv, att_lam_q1, att_lam_k1, att_lam_q2, att_lam_k2, att_subln_g, att_w_out, rel_bias,
           ffn_w_gu, ffn_w_down, moe_w_router, moe_w_gu, moe_w_down, ln_g, ln_b):
    bp, lp, d = x_prompt.shape
    bs, ls, _ = x_sample.shape
    assert ls == 1, "the sample group advances one token per sequence"
    tp = bp * lp
    ts = bs * ls
    xp = x_prompt.reshape(tp, d)
    xs = x_sample.reshape(ts, d)
    moe_gu = moe_w_gu.reshape((-1,) + moe_w_gu.shape[2:])
    moe_down = moe_w_down.reshape((-1,) + moe_w_down.shape[2:])

    tm_p = 1024
    tm_s = ts

    outs = {name: [] for name in ("gS_p", "gS_s", "gc_p", "gc_s", "sh_p", "sh_s", "sc_p", "sc_s",
                                  "ak_p", "ak_s", "av_p", "av_s")}
    for i in range(DEPTH):
        j = i // N_MIXERS
        if i % N_MIXERS == 0:
            n_main = GDN_CONV_DIM + GDN_VD
            w_ba = _pad_cols(gdn_w_in[j][:, n_main:])
            prm = (gdn_conv_w[j], gdn_a_log[j], gdn_dt_bias[j], gdn_norm_g[j])
            proj_p = matmul_cols(xp, gdn_w_in[j], n_main, tm_p, 512).reshape(bp, lp, n_main)
            ba_p = matmul_narrow(xp, w_ba, tm_p).reshape(bp, lp, LANES)
            op, s_p = gdn_prompt(proj_p, ba_p, *prm)
            c_p = proj_p[:, lp - (CONV_W - 1):, :GDN_CONV_DIM]
            proj_s = matmul_cols(xs, gdn_w_in[j], n_main, tm_s, 512, precise=True)
            ba_s = matmul_narrow(xs, w_ba, tm_s)
            os_, s_s = gdn_step(proj_s, ba_s, state_gdn_S[j], state_gdn_conv[j], *prm)
            c_s = jnp.concatenate([state_gdn_conv[j][:, 1:], proj_s[:, None, :GDN_CONV_DIM]], axis=1)
            outs["gS_p"].append(s_p); outs["gS_s"].append(s_s)
            outs["gc_p"].append(c_p); outs["gc_s"].append(c_s)
            w_out = gdn_w_out[j]
        elif i % N_MIXERS == 1:
            n_main = SSM_D_INNER + SSM_CONV_DIM
            w_dt = _pad_cols(ssm_w_in[j][:, n_main:])
            prm = (ssm_conv_w[j], ssm_conv_b[j], ssm_a_log[j], ssm_dt_bias[j], ssm_d_skip[j], ssm_norm_g[j])
            proj_p = matmul_cols(xp, ssm_w_in[j], n_main, tm_p, 512).reshape(bp, lp, n_main)
            dt_p = matmul_narrow(xp, w_dt, tm_p).reshape(bp, lp, LANES)
            op, h_p = ssd_prompt(proj_p, dt_p, *prm)
            c_p = proj_p[:, lp - (CONV_W - 1):, SSM_D_INNER:]
            proj_s = matmul_cols(xs, ssm_w_in[j], n_main, tm_s, 512, precise=True)
            dt_s = matmul_narrow(xs, w_dt, tm_s)
            os_, h_s = ssd_step(proj_s, dt_s, state_ssm_h[j], state_ssm_conv[j], *prm)
            c_s = jnp.concatenate([state_ssm_conv[j][:, 1:], proj_s[:, None, SSM_D_INNER:]], axis=1)
            outs["sh_p"].append(h_p); outs["sh_s"].append(h_s)
            outs["sc_p"].append(c_p); outs["sc_s"].append(c_s)
            w_out = ssm_w_out[j]
        else:
            lam_init = 0.8 - 0.6 * math.exp(-0.3 * i)
            lam = (jnp.exp(jnp.sum(att_lam_q1[j] * att_lam_k1[j]))
                   - jnp.exp(jnp.sum(att_lam_q2[j] * att_lam_k2[j])) + lam_init)
            qkv_p = matmul_cols(xp, att_w_qkv[j], 3 * d, tm_p, 512).reshape(bp, lp, 3 * d)
            op = attn_prompt(qkv_p, lam, lam_init, attn_bias_tiles(rel_bias), att_subln_g[j])
            k_p = qkv_p[..., d:2 * d].reshape(bp, lp, ATT_HEADS, 2 * ATT_HEAD_DIM)
            v_p = qkv_p[..., 2 * d:].reshape(bp, lp, ATT_HEADS, 2 * ATT_HEAD_DIM)
            qkv_s = matmul_cols(xs, att_w_qkv[j], 3 * d, tm_s, 512, precise=True).reshape(bs, ls, 3 * d)
            os_ = attn_decode(qkv_s.reshape(ts, 3 * d), cache_attn_k, cache_attn_v, j, page_table,
                              lam, lam_init, rel_bias, att_subln_g[j])
            k_s = qkv_s[..., d:2 * d].reshape(bs, ls, ATT_HEADS, 2 * ATT_HEAD_DIM)
            v_s = qkv_s[..., 2 * d:].reshape(bs, ls, ATT_HEADS, 2 * ATT_HEAD_DIM)
            outs["ak_p"].append(k_p); outs["ak_s"].append(k_s)
            outs["av_p"].append(v_p); outs["av_s"].append(v_s)
            w_out = att_w_out[j]
        xp = proj_residual_ln(op.reshape(tp, -1), w_out, xp, ln_g[i, 0], ln_b[i, 0], 512)
        xs = proj_residual_ln(os_.reshape(ts, -1), w_out, xs, ln_g[i, 0], ln_b[i, 0], tm_s, precise=True)
        f = i // 2
        if i % 2 == 0:
            xp = dense_ffn_ln(xp, ffn_w_gu, ffn_w_down, f, ln_g[i, 1], ln_b[i, 1], tm_p)
            xs = dense_ffn_ln(xs, ffn_w_gu, ffn_w_down, f, ln_g[i, 1], ln_b[i, 1], tm_s, precise=True)
        else:
            xp = moe_ffn_ln(xp, moe_w_router[f], moe_gu, moe_down, f, ln_g[i, 1], ln_b[i, 1],
                            512, 1024, 256)
            xs = moe_ffn_ln(xs, moe_w_router[f], moe_gu, moe_down, f, ln_g[i, 1], ln_b[i, 1],
                            ts, ts, ts, precise=True)
    st = lambda name: jnp.stack(outs[name])
    return (xp.reshape(bp, lp, d), xs.reshape(bs, ls, d),
            st("gS_p"), st("gS_s"), st("gc_p"), st("gc_s"),
            st("sh_p"), st("sh_s"), st("sc_p"), st("sc_s"),
            st("ak_p"), st("ak_s"), st("av_p"), st("av_s"))
```

```python
import functools
import math

import numpy as np
import jax
import jax.numpy as jnp
from jax import lax
from jax.experimental import pallas as pl
from jax.experimental.pallas import tpu as pltpu

F32 = jnp.float32
BF16 = jnp.bfloat16

D_MODEL = 1024
DEPTH = 4
PAGE_SIZE = 128
N_MIXERS = 3
CONV_W = 4

GDN_HEADS = 8
GDN_DK = 128
GDN_DV = 128
GDN_KD = GDN_HEADS * GDN_DK
GDN_VD = GDN_HEADS * GDN_DV
GDN_CONV_DIM = 2 * GDN_KD + GDN_VD
GDN_CHUNK = 64

SSM_D_INNER = 2 * D_MODEL
SSM_HEAD_DIM = 64
SSM_HEADS = SSM_D_INNER // SSM_HEAD_DIM
SSM_GROUPS = 4
SSM_HPG = SSM_HEADS // SSM_GROUPS
SSM_STATE = 128
SSM_CONV_DIM = SSM_D_INNER + 2 * SSM_GROUPS * SSM_STATE
SSM_CHUNK = 128

ATT_HEADS = 8
ATT_HEAD_DIM = D_MODEL // (2 * ATT_HEADS)
N_BUCKETS = 32
MAX_DISTANCE = 128
Q_BLOCK = 128

D_FF = 3584
N_EXPERTS = 8
TOP_K = 2

DEEPNORM_ALPHA = (2.0 * DEPTH) ** 0.25
LN_EPS = 1e-5
RMS_EPS = 1e-6

V7X_VMEM_BYTES = 64 * 1024 * 1024
VMEM_LIMIT_BYTES = V7X_VMEM_BYTES - 8 * 1024 * 1024
LANES = 128

FF_CHUNK = 512
N_FF_CHUNKS = D_FF // FF_CHUNK


def _cparams(n_axes, flags=None):
    return pltpu.CompilerParams(dimension_semantics=("arbitrary",) * n_axes,
                                vmem_limit_bytes=VMEM_LIMIT_BYTES, flags=flags)


def _layer_norm_rows(x, g, b):
    mu = jnp.mean(x, axis=-1, keepdims=True)
    xc = x - mu
    var = jnp.mean(xc * xc, axis=-1, keepdims=True)
    return xc * lax.rsqrt(var + LN_EPS) * g + b


def _silu(x):
    return x * (1.0 / (1.0 + jnp.exp(-x)))


_HIGHEST = lax.Precision.HIGHEST


def _matmul_kernel(x_ref, w_ref, o_ref, xb_ref, *, precise):
    if precise:
        o_ref[...] = jnp.dot(x_ref[...], w_ref[...], preferred_element_type=F32, precision=_HIGHEST)
        return

    @pl.when(pl.program_id(1) == 0)
    def _():
        xb_ref[...] = x_ref[...].astype(BF16)

    o_ref[...] = jnp.dot(xb_ref[...], w_ref[...].astype(BF16), preferred_element_type=F32)


def matmul_cols(x, w, n_cols, tm, tn, precise=False):
    m, k = x.shape
    return pl.pallas_call(
        functools.partial(_matmul_kernel, precise=precise),
        grid=(m // tm, n_cols // tn),
        in_specs=[pl.BlockSpec((tm, k), lambda i, j: (i, 0)),
                  pl.BlockSpec((k, tn), lambda i, j: (0, j))],
        out_specs=pl.BlockSpec((tm, tn), lambda i, j: (i, j)),
        out_shape=jax.ShapeDtypeStruct((m, n_cols), F32),
        scratch_shapes=[pltpu.VMEM((tm, k), BF16)],
        compiler_params=_cparams(2),
        name="matmul_cols",
    )(x, w)


def _matmul_narrow_kernel(x_ref, w_ref, o_ref):
    o_ref[...] = jnp.dot(x_ref[...], w_ref[...], preferred_element_type=F32, precision=_HIGHEST)


def matmul_narrow(x, w_pad, tm):
    m, k = x.shape
    return pl.pallas_call(
        _matmul_narrow_kernel,
        grid=(m // tm,),
        in_specs=[pl.BlockSpec((tm, k), lambda i: (i, 0)),
                  pl.BlockSpec((k, LANES), lambda i: (0, 0))],
        out_specs=pl.BlockSpec((tm, LANES), lambda i: (i, 0)),
        out_shape=jax.ShapeDtypeStruct((m, LANES), F32),
        compiler_params=_cparams(1),
        name="matmul_narrow",
    )(x, w_pad)


def _proj_ln_kernel(o_ref, w_ref, r_ref, g_ref, b_ref, y_ref, wb_ref, *, precise):
    if precise:
        m = jnp.dot(o_ref[...], w_ref[...], preferred_element_type=F32, precision=_HIGHEST)
    else:
        @pl.when(pl.program_id(0) == 0)
        def _():
            wb_ref[...] = w_ref[...].astype(BF16)

        m = jnp.dot(o_ref[...].astype(BF16), wb_ref[...], preferred_element_type=F32)
    y_ref[...] = _layer_norm_rows(DEEPNORM_ALPHA * r_ref[...] + m, g_ref[...], b_ref[...])


def proj_residual_ln(o, w, resid, g, b, tm, precise=False):
    m, k = o.shape
    d = w.shape[1]
    return pl.pallas_call(
        functools.partial(_proj_ln_kernel, precise=precise),
        grid=(m // tm,),
        in_specs=[pl.BlockSpec((tm, k), lambda i: (i, 0)),
                  pl.BlockSpec((k, d), lambda i: (0, 0)),
                  pl.BlockSpec((tm, d), lambda i: (i, 0)),
                  pl.BlockSpec((1, d), lambda i: (0, 0)),
                  pl.BlockSpec((1, d), lambda i: (0, 0))],
        out_specs=pl.BlockSpec((tm, d), lambda i: (i, 0)),
        out_shape=jax.ShapeDtypeStruct((m, d), F32),
        scratch_shapes=[pltpu.VMEM((k, d), BF16)],
        compiler_params=_cparams(1),
        name="proj_residual_ln",
    )(o, w, resid, g.reshape(1, d), b.reshape(1, d))


def _ffn_kernel(te_ref, na_ref, x_ref, wg_ref, wu_ref, wd_ref, g_ref, b_ref, o_ref,
                xb_ref, acc_ref, *, fuse_ln, precise):
    i = pl.program_id(0)
    j = pl.program_id(1)
    active = i < na_ref[0]

    @pl.when(jnp.logical_and(active, j == 0))
    def _():
        if not precise:
            xb_ref[...] = x_ref[...].astype(BF16)
        acc_ref[...] = jnp.zeros(acc_ref.shape, F32)

    @pl.when(active)
    def _():
        if precise:
            xs = _split2(x_ref[...])
            h = _mm_split(xs, _split2(wg_ref[...]))
            u = _mm_split(xs, _split2(wu_ref[...]))
            acc_ref[...] += _mm_split(_split2(_silu(h) * u), _split2(wd_ref[...]))
        else:
            xb = xb_ref[...]
            h = jnp.dot(xb, wg_ref[...].astype(BF16), preferred_element_type=F32)
            u = jnp.dot(xb, wu_ref[...].astype(BF16), preferred_element_type=F32)
            a = (_silu(h) * u).astype(BF16)
            acc_ref[...] += jnp.dot(a, wd_ref[...].astype(BF16), preferred_element_type=F32)

    @pl.when(jnp.logical_and(active, j == N_FF_CHUNKS - 1))
    def _():
        if fuse_ln:
            o_ref[...] = _layer_norm_rows(DEEPNORM_ALPHA * x_ref[...] + acc_ref[...],
                                          g_ref[...], b_ref[...])
        else:
            o_ref[...] = acc_ref[...]

    @pl.when(jnp.logical_and(jnp.logical_not(active), j == N_FF_CHUNKS - 1))
    def _():
        o_ref[...] = jnp.zeros_like(o_ref)


def swiglu_tiles(x, w_gu, w_down, tile_expert, n_active, ln_g, ln_b, tm, fuse_ln, precise=False):
    m, d = x.shape
    n_tiles = m // tm

    def row_map(i, j, te, na):
        return (jnp.minimum(i, na[0] - 1), 0)

    def expert(i, te, na):
        return te[jnp.minimum(i, na[0] - 1)]

    w_specs = [pl.BlockSpec((None, d, FF_CHUNK), lambda i, j, te, na: (expert(i, te, na), 0, j)),
               pl.BlockSpec((None, d, FF_CHUNK),
                            lambda i, j, te, na: (expert(i, te, na), 0, j + N_FF_CHUNKS)),
               pl.BlockSpec((None, FF_CHUNK, d), lambda i, j, te, na: (expert(i, te, na), j, 0))]

    grid_spec = pltpu.PrefetchScalarGridSpec(
        num_scalar_prefetch=2,
        grid=(n_tiles, N_FF_CHUNKS),
        in_specs=[pl.BlockSpec((tm, d), row_map)] + w_specs + [
                  pl.BlockSpec((1, d), lambda i, j, te, na: (0, 0)),
                  pl.BlockSpec((1, d), lambda i, j, te, na: (0, 0))],
        out_specs=pl.BlockSpec((tm, d), lambda i, j, te, na: (i, 0)),
        scratch_shapes=[pltpu.VMEM((tm, d), BF16), pltpu.VMEM((tm, d), F32)],
    )
    return pl.pallas_call(
        functools.partial(_ffn_kernel, fuse_ln=fuse_ln, precise=precise),
        grid_spec=grid_spec,
        out_shape=jax.ShapeDtypeStruct((m, d), F32),
        compiler_params=_cparams(2),
        name="swiglu_ln" if fuse_ln else "swiglu_grouped",
    )(tile_expert, n_active, x, w_gu, w_gu, w_down, ln_g.reshape(1, d), ln_b.reshape(1, d))


def dense_ffn_ln(x, w_gu_all, w_down_all, layer, ln_g, ln_b, tm, precise=False):
    n_tiles = x.shape[0] // tm
    te = jnp.full((n_tiles,), layer, jnp.int32)
    na = jnp.full((1,), n_tiles, jnp.int32)
    return swiglu_tiles(x, w_gu_all, w_down_all, te, na, ln_g, ln_b, tm, True, precise)


def _router_kernel(x_ref, w_ref, meta_ref, cnt_ref, carry_ref, *, tm):
    i = pl.program_id(0)

    @pl.when(i == 0)
    def _():
        carry_ref[...] = jnp.zeros_like(carry_ref)

    logits = jnp.dot(x_ref[...], w_ref[...], preferred_element_type=F32, precision=_HIGHEST)
    lane = lax.broadcasted_iota(jnp.int32, (tm, LANES), 1)
    neg = jnp.float32(-jnp.inf)
    logits = jnp.where(lane < N_EXPERTS, logits, neg)
    m1 = jnp.max(logits, axis=-1, keepdims=True)
    i1 = jnp.min(jnp.where(logits == m1, lane, LANES), axis=-1, keepdims=True)
    rest = jnp.where(lane == i1, neg, logits)
    m2 = jnp.max(rest, axis=-1, keepdims=True)
    i2 = jnp.min(jnp.where(rest == m2, lane, LANES), axis=-1, keepdims=True)
    e = jnp.exp(m2 - m1)
    g1 = 1.0 / (1.0 + e)
    g2 = e / (1.0 + e)

    chosen = jnp.logical_or(lane == i1, lane == i2)
    onehot = jnp.where(chosen, 1.0, 0.0).astype(BF16)
    row = lax.broadcasted_iota(jnp.int32, (tm, tm), 0)
    col = lax.broadcasted_iota(jnp.int32, (tm, tm), 1)
    strict_lower = jnp.where(col < row, 1.0, 0.0).astype(BF16)
    before = jnp.dot(strict_lower, onehot, preferred_element_type=F32) + carry_ref[...]
    r1 = jnp.sum(jnp.where(lane == i1, before, 0.0), axis=-1, keepdims=True)
    r2 = jnp.sum(jnp.where(lane == i2, before, 0.0), axis=-1, keepdims=True)
    carry_ref[...] += jnp.sum(onehot.astype(F32), axis=0, keepdims=True)

    meta = jnp.where(lane == 0, i1.astype(F32),
           jnp.where(lane == 1, i2.astype(F32),
           jnp.where(lane == 2, g1,
           jnp.where(lane == 3, g2,
           jnp.where(lane == 4, r1,
           jnp.where(lane == 5, r2, 0.0))))))
    meta_ref[...] = meta
    cnt_ref[...] = carry_ref[...]


def route_top2(x, w_router, tm):
    t, d = x.shape
    w_pad = jnp.pad(w_router, ((0, 0), (0, LANES - N_EXPERTS)))
    return pl.pallas_call(
        functools.partial(_router_kernel, tm=tm),
        grid=(t // tm,),
        in_specs=[pl.BlockSpec((tm, d), lambda i: (i, 0)),
                  pl.BlockSpec((d, LANES), lambda i: (0, 0))],
        out_specs=[pl.BlockSpec((tm, LANES), lambda i: (i, 0)),
                   pl.BlockSpec((1, LANES), lambda i: (0, 0))],
        out_shape=[jax.ShapeDtypeStruct((t, LANES), F32),
                   jax.ShapeDtypeStruct((1, LANES), F32)],
        scratch_shapes=[pltpu.VMEM((1, LANES), F32)],
        compiler_params=_cparams(1),
        name="route_top2",
    )(x, w_pad)


def _dispatch_kernel(pos_ref, x_ref, xs_in_hbm, xs_hbm, sem, *, tokens_per_step):
    del xs_in_hbm

    def row_copy(t, k):
        return pltpu.make_async_copy(x_ref.at[pl.ds(t, 1)],
                                     xs_hbm.at[pl.ds(pos_ref[0, 2 * t + k], 1)], sem)

    def issue(t, carry):
        row_copy(t, 0).start(priority=0)
        row_copy(t, 1).start(priority=1)
        return carry

    lax.fori_loop(0, tokens_per_step, issue, 0, unroll=8)

    def drain(t, carry):
        row_copy(t, 0).wait()
        row_copy(t, 1).wait()
        return carry

    lax.fori_loop(0, tokens_per_step, drain, 0, unroll=8)


def dispatch_rows(x, pos, n_rows, tokens_per_step):
    t, d = x.shape
    n_steps = t // tokens_per_step
    pos2 = pos.reshape(n_steps, 1, 2 * tokens_per_step)
    zeros = jnp.zeros((n_rows, d), x.dtype)
    return pl.pallas_call(
        functools.partial(_dispatch_kernel, tokens_per_step=tokens_per_step),
        grid=(n_steps,),
        in_specs=[pl.BlockSpec((None, 1, 2 * tokens_per_step), lambda i: (i, 0, 0),
                               memory_space=pltpu.SMEM),
                  pl.BlockSpec((tokens_per_step, d), lambda i: (i, 0)),
                  pl.BlockSpec(memory_space=pl.ANY)],
        out_specs=pl.BlockSpec(memory_space=pl.ANY),
        out_shape=jax.ShapeDtypeStruct((n_rows, d), x.dtype),
        scratch_shapes=[pltpu.SemaphoreType.DMA(())],
        input_output_aliases={2: 0},
        compiler_params=_cparams(1),
        name="dispatch_rows",
    )(pos2, x, zeros)


def _combine_ln_kernel(pos_ref, x_ref, gate_ref, g_ref, b_ref, ys_hbm, o_ref, buf_ref, sem, *, tm):
    def row_copy(t, k):
        return pltpu.make_async_copy(ys_hbm.at[pl.ds(pos_ref[0, 2 * t + k], 1)],
                                     buf_ref.at[k, pl.ds(t, 1)], sem)

    def issue(t, carry):
        row_copy(t, 0).start(priority=0)
        row_copy(t, 1).start(priority=1)
        return carry

    lax.fori_loop(0, tm, issue, 0, unroll=8)

    def drain(t, carry):
        row_copy(t, 0).wait()
        row_copy(t, 1).wait()
        return carry

    lax.fori_loop(0, tm, drain, 0, unroll=8)

    gates = gate_ref[...]
    y = gates[:, 2:3] * buf_ref[0] + gates[:, 3:4] * buf_ref[1]
    o_ref[...] = _layer_norm_rows(DEEPNORM_ALPHA * x_ref[...] + y, g_ref[...], b_ref[...])


def combine_ln(x, ys, pos, meta, ln_g, ln_b, tm):
    t, d = x.shape
    n_steps = t // tm
    pos2 = pos.reshape(n_steps, 1, 2 * tm)
    return pl.pallas_call(
        functools.partial(_combine_ln_kernel, tm=tm),
        grid=(n_steps,),
        in_specs=[pl.BlockSpec((None, 1, 2 * tm), lambda i: (i, 0, 0), memory_space=pltpu.SMEM),
                  pl.BlockSpec((tm, d), lambda i: (i, 0)),
                  pl.BlockSpec((tm, LANES), lambda i: (i, 0)),
                  pl.BlockSpec((1, d), lambda i: (0, 0)),
                  pl.BlockSpec((1, d), lambda i: (0, 0)),
                  pl.BlockSpec(memory_space=pl.ANY)],
        out_specs=pl.BlockSpec((tm, d), lambda i: (i, 0)),
        out_shape=jax.ShapeDtypeStruct((t, d), F32),
        scratch_shapes=[pltpu.VMEM((2, tm, d), F32), pltpu.SemaphoreType.DMA(())],
        compiler_params=_cparams(1),
        name="combine_ln",
    )(pos2, x, meta, ln_g.reshape(1, d), ln_b.reshape(1, d), ys)


def moe_ffn_ln(x, w_router, w_gu_all, w_down_all, layer, ln_g, ln_b, tm_route, tm_group, tm_combine,
               precise=False):
    t, d = x.shape
    meta, counts = route_top2(x, w_router, tm_route)
    ids = meta[:, 0:2].astype(jnp.int32)
    ranks = meta[:, 4:6].astype(jnp.int32)
    cnt = counts[0, :N_EXPERTS].astype(jnp.int32)
    tiles_per = (cnt + tm_group - 1) // tm_group
    tile_end = jnp.cumsum(tiles_per)
    group_start = (tile_end - tiles_per) * tm_group
    pos = group_start[ids] + ranks
    n_tiles = min((TOP_K * t + N_EXPERTS * (tm_group - 1)) // tm_group,
                  N_EXPERTS * ((t + tm_group - 1) // tm_group))
    n_active = tile_end[N_EXPERTS - 1:N_EXPERTS]
    tile_ids = jnp.arange(n_tiles, dtype=jnp.int32)
    tile_expert = jnp.sum((tile_ids[:, None] >= tile_end[None, :]).astype(jnp.int32), axis=1)
    tile_expert = jnp.minimum(tile_expert, N_EXPERTS - 1) + layer * N_EXPERTS
    xs = dispatch_rows(x, pos, n_tiles * tm_group, min(t, 1024))
    ys = swiglu_tiles(xs, w_gu_all, w_down_all, tile_expert.astype(jnp.int32),
                      n_active.astype(jnp.int32), ln_g, ln_b, tm_group, False, precise)
    return combine_ln(x, ys, pos, meta, ln_g, ln_b, tm_combine)


def _t5_bucket_runs():
    max_exact = N_BUCKETS // 2
    n = np.arange(MAX_DISTANCE)
    scaled = (np.log(np.maximum(n, max_exact).astype(np.float64) / max_exact)
              / math.log(MAX_DISTANCE / max_exact) * (N_BUCKETS - max_exact))
    frac = scaled - np.floor(scaled)
    assert np.all((np.minimum(frac, 1 - frac) > 1e-3) | (n <= max_exact))
    bucket = np.where(n < max_exact, n, np.minimum(max_exact + scaled.astype(np.int64), N_BUCKETS - 1))
    runs = []
    for dist in range(MAX_DISTANCE):
        if runs and runs[-1][1] == int(bucket[dist]):
            runs[-1] = (dist, int(bucket[dist]))
        else:
            runs.append((dist, int(bucket[dist])))
    return runs, bucket


T5_RUNS, T5_BUCKETS = _t5_bucket_runs()
ATT_TQ = 512
ATT_ROW_BLOCK = 512


def _bias_tile_kernel(rel_ref, o_ref):
    h = pl.program_id(0)
    tq, tk2 = o_ref.shape
    row = lax.broadcasted_iota(jnp.int32, (tq, tk2), 0)
    col = lax.broadcasted_iota(jnp.int32, (tq, tk2), 1)
    dist = row - col + tq
    far = rel_ref[N_BUCKETS - 1, h]
    acc = jnp.zeros((tq, tk2), F32)
    for last, bucket in reversed(T5_RUNS):
        acc = jnp.where(dist <= last, rel_ref[bucket, h] - far, acc)
    o_ref[...] = jnp.where(dist < 0, -jnp.inf, acc)


def attn_bias_tiles(rel_bias):
    return pl.pallas_call(
        _bias_tile_kernel,
        grid=(ATT_HEADS,),
        in_specs=[pl.BlockSpec(memory_space=pltpu.SMEM)],
        out_specs=pl.BlockSpec((None, ATT_TQ, 2 * ATT_TQ), lambda h: (h, 0, 0)),
        out_shape=jax.ShapeDtypeStruct((ATT_HEADS, ATT_TQ, 2 * ATT_TQ), F32),
        compiler_params=_cparams(1),
        name="attn_bias_tiles",
    )(rel_bias)


def _softmax_step(s, v, m_ref, l_ref, acc_ref):
    m_prev = m_ref[...]
    m_new = jnp.maximum(m_prev, jnp.max(s, axis=-1, keepdims=True))
    alpha = jnp.exp(m_prev - m_new)
    p = jnp.exp(s - jnp.concatenate([m_new] * (s.shape[1] // LANES), axis=1))
    l_ref[...] = alpha * l_ref[...] + jnp.sum(p, axis=-1, keepdims=True)
    acc_ref[...] = alpha * acc_ref[...] + jnp.dot(p.astype(BF16), v, preferred_element_type=F32)
    m_ref[...] = m_new


def _attn_prompt_kernel(lam_ref, q_ref, k_ref, v_ref, bt_ref, g_ref, o_ref,
                        kb_ref, vb_ref, m1_ref, l1_ref, a1_ref, m2_ref, l2_ref, a2_ref, *, out_scale):
    qi = pl.program_id(2)
    tq = q_ref.shape[0]
    half = ATT_HEAD_DIM

    @pl.when(qi == 0)
    def _():
        kb_ref[...] = k_ref[...].astype(BF16)
        vb_ref[...] = v_ref[...].astype(BF16)

    q = q_ref[...] * (ATT_HEAD_DIM ** -0.5)
    lane = lax.broadcasted_iota(jnp.int32, q.shape, 1)
    q1 = jnp.where(lane < half, q, 0.0).astype(BF16)
    q2 = jnp.where(lane >= half, q, 0.0).astype(BF16)

    for m_ref, l_ref, a_ref in ((m1_ref, l1_ref, a1_ref), (m2_ref, l2_ref, a2_ref)):
        m_ref[...] = jnp.full(m_ref.shape, -jnp.inf, F32)
        l_ref[...] = jnp.zeros(l_ref.shape, F32)
        a_ref[...] = jnp.zeros(a_ref.shape, F32)

    nt = (((1,), (1,)), ((), ()))
    rb = ATT_ROW_BLOCK
    stats = ((q1, m1_ref, l1_ref, a1_ref), (q2, m2_ref, l2_ref, a2_ref))

    def chunks(specs):
        ks = [kb_ref[pl.ds(start, tq), :] for start, _ in specs]
        vs = [vb_ref[pl.ds(start, tq), :] for start, _ in specs]
        units = [(ci, r, mp) for ci in range(len(specs)) for r in range(tq // rb) for mp in range(2)]

        def scores(u):
            ci, r, mp = u
            s = lax.dot_general(stats[mp][0][r * rb:(r + 1) * rb], ks[ci], nt, preferred_element_type=F32)
            bias = specs[ci][1]
            return s if bias is None else s + bias[r * rb:(r + 1) * rb, :]

        s_cur = scores(units[0])
        for i, (ci, r, mp) in enumerate(units):
            s_next = scores(units[i + 1]) if i + 1 < len(units) else None
            _, m_ref, l_ref, a_ref = stats[mp]
            rows = slice(r * rb, (r + 1) * rb)
            _softmax_step(s_cur, vs[ci], m_ref.at[rows], l_ref.at[rows], a_ref.at[rows])
            s_cur = s_next

    n_far = qi - 1

    def far_pair(c, carry):
        chunks([(pl.multiple_of(2 * c * tq, tq), None), (pl.multiple_of((2 * c + 1) * tq, tq), None)])
        return carry

    lax.fori_loop(0, n_far // 2, far_pair, 0)

    @pl.when(jnp.logical_and(n_far > 0, n_far % 2 == 1))
    def _():
        chunks([(pl.multiple_of((n_far - 1) * tq, tq), None)])

    @pl.when(qi > 0)
    def _():
        chunks([(pl.multiple_of((qi - 1) * tq, tq), bt_ref[:, 0:tq]),
                (pl.multiple_of(qi * tq, tq), bt_ref[:, tq:2 * tq])])

    @pl.when(qi == 0)
    def _():
        chunks([(0, bt_ref[:, tq:2 * tq])])

    o = a1_ref[...] / l1_ref[...] - lam_ref[0] * (a2_ref[...] / l2_ref[...])
    o = o * lax.rsqrt(jnp.mean(o * o, axis=-1, keepdims=True) + RMS_EPS) * g_ref[...]
    o_ref[...] = o * out_scale


def attn_prompt(qkv, lam, lam_init, bias_tiles, subln_g):
    bn, l, _ = qkv.shape
    hd = 2 * ATT_HEAD_DIM
    tq = ATT_TQ
    stat = pltpu.VMEM((tq, hd), F32)
    return pl.pallas_call(
        functools.partial(_attn_prompt_kernel, out_scale=1.0 - lam_init),
        grid=(bn, ATT_HEADS, l // tq),
        in_specs=[pl.BlockSpec(memory_space=pltpu.SMEM),
                  pl.BlockSpec((None, tq, hd), lambda b, h, i: (b, i, h)),
                  pl.BlockSpec((None, l, hd), lambda b, h, i: (b, 0, ATT_HEADS + h)),
                  pl.BlockSpec((None, l, hd), lambda b, h, i: (b, 0, 2 * ATT_HEADS + h)),
                  pl.BlockSpec((None, tq, 2 * tq), lambda b, h, i: (h, 0, 0)),
                  pl.BlockSpec((1, hd), lambda b, h, i: (0, 0))],
        out_specs=pl.BlockSpec((None, tq, hd), lambda b, h, i: (b, i, h)),
        out_shape=jax.ShapeDtypeStruct((bn, l, ATT_HEADS * hd), F32),
        scratch_shapes=[pltpu.VMEM((l, hd), BF16), pltpu.VMEM((l, hd), BF16),
                        stat, stat, stat, stat, stat, stat],
        compiler_params=_cparams(3),
        name="attn_prompt",
    )(lam.reshape(1), qkv, qkv, qkv, bias_tiles, subln_g.reshape(1, hd))


ATT_PAGES_PER_STEP = 8
ATT_ROWS = 2 * ATT_HEADS


def _row_select(vec_row, h, lane_lo, lane_hi):
    rid = lax.broadcasted_iota(jnp.int32, (ATT_ROWS, LANES), 0)
    lane = lax.broadcasted_iota(jnp.int32, (ATT_ROWS, LANES), 1)
    full = jnp.broadcast_to(vec_row, (ATT_ROWS, LANES))
    keep = jnp.logical_or(jnp.logical_and(rid == 2 * h, lane < lane_hi),
                          jnp.logical_and(rid == 2 * h + 1, lane >= lane_lo))
    return jnp.where(keep, full, 0.0)


def _attn_decode_kernel(pt_ref, lam_ref, qkv_ref, bl_ref, bc_ref, g_ref, *refs, out_scale):
    npg = ATT_PAGES_PER_STEP
    k_refs, v_refs = refs[:npg], refs[npg:2 * npg]
    o_ref, qs_ref, m_ref, l_ref, acc_ref = refs[2 * npg:]
    del pt_ref
    step = pl.program_id(1)
    last = pl.num_programs(1) - 1
    rid = lax.broadcasted_iota(jnp.int32, (ATT_ROWS, LANES), 0)
    half = ATT_HEAD_DIM
    hd = 2 * ATT_HEAD_DIM

    @pl.when(step == 0)
    def _():
        m_ref[...] = jnp.full(m_ref.shape, -jnp.inf, F32)
        l_ref[...] = jnp.zeros(l_ref.shape, F32)
        acc_ref[...] = jnp.zeros(acc_ref.shape, F32)
        for h in range(ATT_HEADS):
            qh = qkv_ref[:, h * hd:(h + 1) * hd] * (ATT_HEAD_DIM ** -0.5)
            qs_ref[h] = _row_select(qh, h, half, half).astype(BF16)

    def head_rows(page_refs, h):
        return jnp.concatenate(
            [r[pl.ds(h, PAGE_SIZE, stride=ATT_HEADS), :].astype(BF16) for r in page_refs], axis=0)

    s = None
    for h in range(ATT_HEADS):
        t = lax.dot_general(qs_ref[h], head_rows(k_refs, h), _NT, preferred_element_type=F32)
        s = t if s is None else s + t
    tail_bias = jnp.where(step == last, 1.0, 0.0) * bl_ref[...]
    s = jnp.concatenate([s[:, :(npg - 1) * PAGE_SIZE], s[:, (npg - 1) * PAGE_SIZE:] + tail_bias], axis=1)

    m_prev = m_ref[...]
    m_new = jnp.maximum(m_prev, jnp.max(s, axis=-1, keepdims=True))
    alpha = jnp.exp(m_prev - m_new)
    p = jnp.exp(s - m_new[:, 0:1])
    l_ref[...] = alpha * l_ref[...] + jnp.sum(p, axis=-1, keepdims=True)
    pb = p.astype(BF16)
    pv = None
    for h in range(ATT_HEADS):
        t = jnp.dot(pb, head_rows(v_refs, h), preferred_element_type=F32)
        t = jnp.where(jnp.logical_or(rid == 2 * h, rid == 2 * h + 1), t, 0.0)
        pv = t if pv is None else pv + t
    acc_ref[...] = alpha * acc_ref[...] + pv
    m_ref[...] = m_new

    @pl.when(step == last)
    def _():
        s_cur = None
        v_rows = None
        for h in range(ATT_HEADS):
            kn = qkv_ref[:, D_MODEL + h * hd:D_MODEL + (h + 1) * hd].astype(BF16).astype(F32)
            vn = qkv_ref[:, 2 * D_MODEL + h * hd:2 * D_MODEL + (h + 1) * hd].astype(BF16).astype(F32)
            t = jnp.sum(qs_ref[h].astype(F32) * kn, axis=-1, keepdims=True)
            s_cur = t if s_cur is None else s_cur + t
            vr = _row_select(vn, h, 0, LANES)
            v_rows = vr if v_rows is None else v_rows + vr
        s_cur = s_cur + bc_ref[...][:, 0:1]
        m_prev = m_ref[...]
        m_new = jnp.maximum(m_prev, s_cur)
        alpha = jnp.exp(m_prev - m_new)
        p = jnp.exp(s_cur - m_new[:, 0:1])
        l_fin = alpha * l_ref[...] + p
        acc = alpha * acc_ref[...] + p.astype(BF16).astype(F32) * v_rows
        o = acc / l_fin
        er = lax.broadcasted_iota(jnp.int32, (ATT_HEADS, ATT_ROWS), 0)
        ec = lax.broadcasted_iota(jnp.int32, (ATT_HEADS, ATT_ROWS), 1)
        pick1 = jnp.where(ec == 2 * er, 1.0, 0.0).astype(BF16)
        pick2 = jnp.where(ec == 2 * er + 1, 1.0, 0.0).astype(BF16)
        hi, mid, lo = _split3(o)
        o1 = (jnp.dot(pick1, hi, preferred_element_type=F32) + jnp.dot(pick1, mid, preferred_element_type=F32)
              + jnp.dot(pick1, lo, preferred_element_type=F32))
        o2 = (jnp.dot(pick2, hi, preferred_element_type=F32) + jnp.dot(pick2, mid, preferred_element_type=F32)
              + jnp.dot(pick2, lo, preferred_element_type=F32))
        d = o1 - lam_ref[0] * o2
        d = d * lax.rsqrt(jnp.mean(d * d, axis=-1, keepdims=True) + RMS_EPS) * g_ref[...]
        o_ref[...] = d * out_scale


def attn_decode(qkv, k_pages, v_pages, layer, page_table, lam, lam_init, rel_bias, subln_g):
    bn = qkv.shape[0]
    n_pages = page_table.shape[1]
    npg = ATT_PAGES_PER_STEP
    hd = 2 * ATT_HEAD_DIM
    dist = PAGE_SIZE - np.arange(PAGE_SIZE)
    bucket = np.where(dist < MAX_DISTANCE, T5_BUCKETS[np.minimum(dist, MAX_DISTANCE - 1)], N_BUCKETS - 1)
    far = rel_bias[N_BUCKETS - 1]
    bias_last = jnp.repeat((rel_bias[bucket] - far).T, 2, axis=0)
    bias_cur = jnp.broadcast_to(jnp.repeat(rel_bias[0] - far, 2)[:, None], (ATT_ROWS, LANES))

    def rows_of(pages):
        return pages.reshape(pages.shape[0], pages.shape[1], PAGE_SIZE * ATT_HEADS, hd)

    def page_spec(pg):
        return pl.BlockSpec((None, None, PAGE_SIZE * ATT_HEADS, hd),
                            lambda b, s, pt: (layer, pt[b, s * npg + pg], 0, 0))

    full = lambda shape: pl.BlockSpec(shape, lambda b, s, pt: (0,) * len(shape))
    stat = pltpu.VMEM((ATT_ROWS, LANES), F32)
    grid_spec = pltpu.PrefetchScalarGridSpec(
        num_scalar_prefetch=1,
        grid=(bn, n_pages // npg),
        in_specs=[pl.BlockSpec(memory_space=pltpu.SMEM),
                  pl.BlockSpec((None, 1, 3 * D_MODEL), lambda b, s, pt: (b, 0, 0)),
                  full((ATT_ROWS, PAGE_SIZE)), full((ATT_ROWS, LANES)), full((1, hd))]
                 + [page_spec(pg) for pg in range(npg)] * 2,
        out_specs=pl.BlockSpec((None, ATT_HEADS, hd), lambda b, s, pt: (b, 0, 0)),
        scratch_shapes=[pltpu.VMEM((ATT_HEADS, ATT_ROWS, LANES), BF16), stat, stat, stat],
    )
    out = pl.pallas_call(
        functools.partial(_attn_decode_kernel, out_scale=1.0 - lam_init),
        grid_spec=grid_spec,
        out_shape=jax.ShapeDtypeStruct((bn, ATT_HEADS, hd), F32),
        compiler_params=_cparams(2),
        name="attn_decode",
    )(page_table, lam.reshape(1), qkv.reshape(bn, 1, 3 * D_MODEL), bias_last, bias_cur,
      subln_g.reshape(1, hd), *([rows_of(k_pages)] * npg), *([rows_of(v_pages)] * npg))
    return out.reshape(bn, D_MODEL)


def _split3(v):
    hi = v.astype(BF16)
    r1 = v - hi.astype(F32)
    mid = r1.astype(BF16)
    lo = (r1 - mid.astype(F32)).astype(BF16)
    return hi, mid, lo


def _dot3(v, w_bf16, dims=None):
    out = None
    for piece in _split3(v):
        if dims is None:
            t = jnp.dot(piece, w_bf16, preferred_element_type=F32)
        else:
            t = lax.dot_general(piece, w_bf16, dims, preferred_element_type=F32)
        out = t if out is None else out + t
    return out


def _softplus(x):
    return jnp.maximum(x, 0.0) + jnp.log1p(jnp.exp(-jnp.abs(x)))


_NT = (((1,), (1,)), ((), ()))
_TN = (((0,), (0,)), ((), ()))
SSM_GROUP_W = SSM_HPG * SSM_HEAD_DIM
SSM_GN = SSM_GROUPS * SSM_STATE


def _ssd_prompt_kernel(proj_ref, dt_ref, cw_ref, cb_ref, dtb_ref, aneg_ref, dskip_ref, ng_ref, e_ref,
                       y_ref, h_ref, xp_ref, ht_ref):
    ci = pl.program_id(1)
    c = SSM_CHUNK
    di = SSM_D_INNER

    @pl.when(ci == 0)
    def _():
        xp_ref[0:8, :] = jnp.zeros((8, SSM_CONV_DIM), F32)
        ht_ref[...] = jnp.zeros(ht_ref.shape, F32)

    xbc = proj_ref[:, di:di + SSM_CONV_DIM]
    xp_ref[8:8 + c, :] = xbc
    conv = cb_ref[...] + xp_ref[5:5 + c, :] * cw_ref[0:1, :]
    for i in range(1, CONV_W):
        conv = conv + xp_ref[5 + i:5 + i + c, :] * cw_ref[i:i + 1, :]
    xp_ref[0:8, :] = xbc[c - 8:c, :]
    act = _silu(conv)
    xs = act[:, :di]

    dt = _softplus(dt_ref[...] + dtb_ref[...])
    a = dt * aneg_ref[...]
    row = lax.broadcasted_iota(jnp.int32, (c, c), 0)
    col = lax.broadcasted_iota(jnp.int32, (c, c), 1)
    incl = col <= row
    tri = jnp.where(incl, 1.0, 0.0).astype(BF16)
    hi, mid, lo = _split3(a)
    acum = (jnp.dot(tri, hi, preferred_element_type=F32) + jnp.dot(tri, mid, preferred_element_type=F32)
            + jnp.dot(tri, lo, preferred_element_type=F32))
    upper = jnp.where(row <= col, 1.0, 0.0).astype(BF16)
    acum_t = _dot3(a, upper, _TN)
    a_end = acum[c - 1:c, :]
    eac = jnp.exp(acum)
    wend = jnp.exp(a_end - acum) * dt
    ex = _dot3(jnp.concatenate([dt, eac, wend], axis=0), e_ref[...])
    dt_x, eac_x, wend_x = ex[0:c], ex[c:2 * c], ex[2 * c:3 * c]
    xdt = (xs * dt_x).astype(BF16)
    xw = (xs * wend_x).astype(BF16)
    lane = lax.broadcasted_iota(jnp.int32, (c, LANES), 1)
    lo_half = lane < SSM_HEAD_DIM
    neg_inf = jnp.float32(-jnp.inf)

    for g in range(SSM_GROUPS):
        gs = slice(g * SSM_GROUP_W, (g + 1) * SSM_GROUP_W)
        bg = act[:, di + g * SSM_STATE:di + (g + 1) * SSM_STATE].astype(BF16)
        cg = act[:, di + SSM_GN + g * SSM_STATE:di + SSM_GN + (g + 1) * SSM_STATE].astype(BF16)
        cb = lax.dot_general(cg, bg, _NT, preferred_element_type=F32)
        pieces = []
        for pr in range(SSM_GROUP_W // LANES):
            pair = g * (SSM_GROUP_W // LANES) + pr
            xd = xdt[:, pair * LANES:(pair + 1) * LANES]
            acc = None
            for half in range(2):
                h = 2 * pair + half
                diff = acum[:, h:h + 1] - acum_t[h:h + 1, :]
                s = (cb * jnp.exp(jnp.where(incl, diff, neg_inf))).astype(BF16)
                xm = jnp.where(lo_half if half == 0 else jnp.logical_not(lo_half), xd, jnp.zeros_like(xd))
                t = jnp.dot(s, xm, preferred_element_type=F32)
                acc = t if acc is None else acc + t
            pieces.append(acc)
        y_intra = jnp.concatenate(pieces, axis=1)
        h_old = ht_ref[:, gs]
        y_inter = jnp.dot(cg, h_old.astype(BF16), preferred_element_type=F32) * eac_x[:, gs]
        dstate = lax.dot_general(bg, xw[:, gs], _TN, preferred_element_type=F32)
        ht_ref[:, gs] = h_old * eac_x[c - 1:c, gs] + dstate
        y = y_intra + y_inter + dskip_ref[:, gs] * xs[:, gs]
        y = y * _silu(proj_ref[:, gs])
        y = y * lax.rsqrt(jnp.mean(y * y, axis=-1, keepdims=True) + RMS_EPS) * ng_ref[:, gs]
        y_ref[:, gs] = y

    @pl.when(ci == pl.num_programs(1) - 1)
    def _():
        h_ref[...] = ht_ref[...].T


def _head_expander():
    e = np.zeros((LANES, SSM_D_INNER), np.float32)
    for h in range(SSM_HEADS):
        e[h, h * SSM_HEAD_DIM:(h + 1) * SSM_HEAD_DIM] = 1.0
    return jnp.asarray(e, BF16)


def _pad_lanes(v):
    return jnp.pad(v.astype(F32), (0, LANES - v.shape[0])).reshape(1, LANES)


def ssd_prompt(proj, dtp, conv_w, conv_b, a_log, dt_bias, d_skip, norm_g):
    bn, l, pw = proj.shape
    c = SSM_CHUNK
    di = SSM_D_INNER
    full = lambda shape: pl.BlockSpec(shape, lambda b, i: (0,) * len(shape))
    y, h = pl.pallas_call(
        _ssd_prompt_kernel,
        grid=(bn, l // c),
        in_specs=[pl.BlockSpec((None, c, pw), lambda b, i: (b, i, 0)),
                  pl.BlockSpec((None, c, LANES), lambda b, i: (b, i, 0)),
                  full((CONV_W, SSM_CONV_DIM)), full((1, SSM_CONV_DIM)),
                  full((1, LANES)), full((1, LANES)), full((1, di)), full((1, di)),
                  full((LANES, di))],
        out_specs=[pl.BlockSpec((None, c, di), lambda b, i: (b, i, 0)),
                   pl.BlockSpec((None, di, SSM_STATE), lambda b, i: (b, 0, 0))],
        out_shape=[jax.ShapeDtypeStruct((bn, l, di), F32),
                   jax.ShapeDtypeStruct((bn, di, SSM_STATE), F32)],
        scratch_shapes=[pltpu.VMEM((8 + c, SSM_CONV_DIM), F32), pltpu.VMEM((SSM_STATE, di), F32)],
        compiler_params=_cparams(2),
        name="ssd_prompt",
    )(proj, dtp, conv_w, conv_b.reshape(1, -1), _pad_lanes(dt_bias), _pad_lanes(-jnp.exp(a_log)),
      jnp.repeat(d_skip, SSM_HEAD_DIM).reshape(1, di), norm_g.reshape(1, di), _head_expander())
    return y, h.reshape(bn, SSM_HEADS, SSM_HEAD_DIM, SSM_STATE)


GDN_BLK = 512
GDN_HEADS_PER_STEP = 4


def _split2(v):
    hi = v.astype(BF16)
    return hi, (v - hi.astype(F32)).astype(BF16)


def _mm_split(a, b):
    (a_hi, a_lo), (b_hi, b_lo) = a, b
    return (jnp.dot(a_hi, b_hi, preferred_element_type=F32) + jnp.dot(a_hi, b_lo, preferred_element_type=F32)
            + jnp.dot(a_lo, b_hi, preferred_element_type=F32))


def _conv_silu_block(x_ref, xp_ref, cw_ref):
    n = x_ref.shape[0]
    x = x_ref[...]
    xp_ref[8:8 + n, :] = x
    y = xp_ref[5:5 + n, :] * cw_ref[0:1, :]
    for i in range(1, CONV_W):
        y = y + xp_ref[5 + i:5 + i + n, :] * cw_ref[i:i + 1, :]
    xp_ref[0:8, :] = x[n - 8:n, :]
    return _silu(y)


def _gdn_prompt_kernel(alog_ref, dtb_ref, q_ref, k_ref, v_ref, z_ref, ba_ref, cwq_ref, cwk_ref, cwv_ref,
                       ng_ref, tri_ref, upper_ref, o_ref, s_out_ref, xq_ref, xk_ref, xv_ref, s_ref):
    hp = pl.program_id(1)
    bi = pl.program_id(2)
    blk = q_ref.shape[0]
    c = GDN_CHUNK
    hd = GDN_DK
    nhs = GDN_HEADS_PER_STEP

    @pl.when(bi == 0)
    def _():
        for r in (xq_ref, xk_ref, xv_ref):
            r[0:8, :] = jnp.zeros((8, r.shape[1]), F32)
        s_ref[...] = jnp.zeros(s_ref.shape, F32)

    q_all = _conv_silu_block(q_ref, xq_ref, cwq_ref)
    k_all = _conv_silu_block(k_ref, xk_ref, cwk_ref)
    v_all = _conv_silu_block(v_ref, xv_ref, cwv_ref)
    lane = lax.broadcasted_iota(jnp.int32, (blk, LANES), 1)
    ba = ba_ref[...]
    tri_bd = tri_ref[...]
    upper_bd = upper_ref[...]

    row = lax.broadcasted_iota(jnp.int32, (c, c), 0)
    col = lax.broadcasted_iota(jnp.int32, (c, c), 1)
    incl = col <= row
    strict = col < row
    eye = jnp.where(row == col, 1.0, 0.0)
    neg_inf = jnp.float32(-jnp.inf)
    nch = blk // c

    qh, kh, vh, betah, gcolh, growh, egh = [], [], [], [], [], [], []
    for e in range(nhs):
        h = hp * nhs + e
        hs = slice(e * hd, (e + 1) * hd)
        q = q_all[:, hs]
        k = k_all[:, hs]
        qh.append(q * lax.rsqrt(jnp.sum(q * q, axis=-1, keepdims=True) + 1e-6) * (GDN_DK ** -0.5))
        kh.append(k * lax.rsqrt(jnp.sum(k * k, axis=-1, keepdims=True) + 1e-6))
        vh.append(v_all[:, hs])
        b_col = jnp.sum(jnp.where(lane == h, ba, 0.0), axis=-1, keepdims=True)
        a_col = jnp.sum(jnp.where(lane == GDN_HEADS + h, ba, 0.0), axis=-1, keepdims=True)
        betah.append(1.0 / (1.0 + jnp.exp(-b_col)))
        g = -jnp.exp(alog_ref[h]) * _softplus(a_col + dtb_ref[h])
        g_rep = jnp.broadcast_to(g, (blk, LANES))
        hi, mid, lo = _split3(g_rep)
        gcolh.append(jnp.dot(tri_bd, hi, preferred_element_type=F32)
                     + jnp.dot(tri_bd, mid, preferred_element_type=F32)
                     + jnp.dot(tri_bd, lo, preferred_element_type=F32))
        growh.append(_dot3(g_rep[:, :c], upper_bd, _TN))
        egh.append(jnp.exp(gcolh[e][:, 0:1]))

    units = [(e, n) for n in range(nch) for e in range(nhs)]
    rws = lambda n: slice(n * c, (n + 1) * c)
    decays = [jnp.exp(jnp.where(incl, gcolh[e][rws(n), 0:c] - growh[e][:, n * LANES:n * LANES + c], neg_inf))
              for e, n in units]
    kbs = [kh[e][rws(n)].astype(BF16) for e, n in units]
    kks = [lax.dot_general(kb, kb, _NT, preferred_element_type=F32) for kb in kbs]
    qks = [lax.dot_general(qh[e][rws(n)].astype(BF16), kbs[u], _NT, preferred_element_type=F32) * decays[u]
           for u, (e, n) in enumerate(units)]
    mps = [jnp.where(strict, betah[e][rws(n)] * kks[u] * decays[u], 0.0) for u, (e, n) in enumerate(units)]
    ps = [eye - m for m in mps]
    for _ in range(5):
        mps = [_mm_split(_split2(mp), _split2(mp)) for mp in mps]
        mp_parts = [_split2(mp) for mp in mps]
        ps = [p + _mm_split(_split2(p), mpp) for p, mpp in zip(ps, mp_parts)]
    rhs = [jnp.concatenate([betah[e][rws(n)] * vh[e][rws(n)],
                            (betah[e][rws(n)] * egh[e][rws(n)]) * kh[e][rws(n)]], axis=1) for e, n in units]
    ws = [_mm_split(_split2(p), _split2(x)) for p, x in zip(ps, rhs)]
    uvs = [w[:, :GDN_DV].astype(BF16) for w in ws]
    uks = [w[:, GDN_DV:].astype(BF16) for w in ws]
    g_last = [gcolh[e][n * c + c - 1:n * c + c, 0:1] for e, n in units]
    kes = [(kh[e][rws(n)] * jnp.exp(g_last[u] - gcolh[e][rws(n), 0:1])).astype(BF16)
           for u, (e, n) in enumerate(units)]
    qkb = [x.astype(BF16) for x in qks]
    kmats = [lax.dot_general(kes[u], uks[u], _TN, preferred_element_type=F32).astype(BF16)
             for u in range(len(units))]
    hmats = [lax.dot_general(kes[u], uvs[u], _TN, preferred_element_type=F32) for u in range(len(units))]
    qps = [(qh[e][rws(n)] * egh[e][rws(n)] - jnp.dot(qkb[u], uks[u], preferred_element_type=F32)).astype(BF16)
           for u, (e, n) in enumerate(units)]
    ops = [jnp.dot(qkb[u], uvs[u], preferred_element_type=F32) for u in range(len(units))]

    states = [s_ref[e] for e in range(nhs)]
    for u, (e, n) in enumerate(units):
        hs = slice(e * hd, (e + 1) * hd)
        sb = states[e].astype(BF16)
        o = jnp.dot(qps[u], sb, preferred_element_type=F32) + ops[u]
        states[e] = (jnp.exp(g_last[u]) * states[e] - jnp.dot(kmats[u], sb, preferred_element_type=F32)
                     + hmats[u])
        o = o * lax.rsqrt(jnp.mean(o * o, axis=-1, keepdims=True) + RMS_EPS) * ng_ref[...]
        o_ref[rws(n), hs] = o * _silu(z_ref[rws(n), hs])
    for e in range(nhs):
        s_ref[e] = states[e]

    @pl.when(bi == pl.num_programs(2) - 1)
    def _():
        for e in range(nhs):
            s_out_ref[e] = states[e]


def gdn_prompt(proj, ba, conv_w, a_log, dt_bias, norm_g):
    bn, l, _ = proj.shape
    blk = GDN_BLK
    nh = GDN_HEADS
    hd = GDN_DK
    nhs = GDN_HEADS_PER_STEP
    hw = nhs * hd
    ngrp = nh // nhs
    col = lambda part: pl.BlockSpec((None, blk, hw), lambda b, h, i: (b, i, part * ngrp + h))
    cw = lambda part: pl.BlockSpec((CONV_W, hw), lambda b, h, i: (0, part * ngrp + h))
    tail = pltpu.VMEM((8 + blk, hw), F32)
    nch = blk // GDN_CHUNK
    t = np.arange(blk)
    same = (t[:, None] // GDN_CHUNK) == (t[None, :] // GDN_CHUNK)
    tri_bd = jnp.asarray(same & (t[None, :] <= t[:, None]), BF16)
    jj = np.arange(nch * LANES)
    upper = ((t[:, None] // GDN_CHUNK) == (jj[None, :] // LANES)) & \
            ((t[:, None] % GDN_CHUNK) <= (jj[None, :] % LANES)) & ((jj[None, :] % LANES) < GDN_CHUNK)
    upper_bd = jnp.asarray(upper, BF16)
    o, s = pl.pallas_call(
        _gdn_prompt_kernel,
        grid=(bn, ngrp, l // blk),
        in_specs=[pl.BlockSpec(memory_space=pltpu.SMEM), pl.BlockSpec(memory_space=pltpu.SMEM),
                  col(0), col(1), col(2), col(3),
                  pl.BlockSpec((None, blk, LANES), lambda b, h, i: (b, i, 0)),
                  cw(0), cw(1), cw(2),
                  pl.BlockSpec((1, hd), lambda b, h, i: (0, 0)),
                  pl.BlockSpec((blk, blk), lambda b, h, i: (0, 0)),
                  pl.BlockSpec((blk, nch * LANES), lambda b, h, i: (0, 0))],
        out_specs=[pl.BlockSpec((None, blk, hw), lambda b, h, i: (b, i, h)),
                   pl.BlockSpec((None, nhs, hd, hd), lambda b, h, i: (b, h, 0, 0))],
        out_shape=[jax.ShapeDtypeStruct((bn, l, nh * hd), F32),
                   jax.ShapeDtypeStruct((bn, nh, hd, hd), F32)],
        scratch_shapes=[tail, tail, tail, pltpu.VMEM((nhs, hd, hd), F32)],
        compiler_params=_cparams(3),
        name="gdn_prompt",
    )(a_log.astype(F32), dt_bias.astype(F32), proj, proj, proj, proj, ba, conv_w, conv_w, conv_w,
      norm_g.reshape(1, hd), tri_bd, upper_bd)
    return o, s


SUB = 16


def _rows8(row, zero_fill, dtype=F32):
    full = jnp.broadcast_to(row.astype(F32), (SUB, row.shape[1]))
    if zero_fill:
        rid = lax.broadcasted_iota(jnp.int32, full.shape, 0)
        full = jnp.where(rid == 0, full, 0.0)
    return full.astype(dtype)


def _conv_step(x_row, c_ref, cw_ref, cb_row):
    y = x_row * cw_ref[CONV_W - 1:CONV_W, :]
    for i in range(CONV_W - 1):
        y = y + c_ref[i:i + 1, :] * cw_ref[i:i + 1, :]
    if cb_row is not None:
        y = y + cb_row
    return _silu(y)


def _gdn_step_kernel(alog_ref, dtb_ref, proj_ref, ba_ref, c_ref, cw_ref, ng_ref, s_ref, o_ref, s_out_ref):
    hd = GDN_DK
    act = _conv_step(proj_ref[:, :GDN_CONV_DIM], c_ref, cw_ref, None)
    ba = ba_ref[...]
    qs, kbs, us, egs, qks = [], [], [], [], []
    for h in range(GDN_HEADS):
        q = act[:, h * hd:(h + 1) * hd]
        k = act[:, GDN_KD + h * hd:GDN_KD + (h + 1) * hd]
        q = q * lax.rsqrt(jnp.sum(q * q, axis=-1, keepdims=True) + 1e-6) * (GDN_DK ** -0.5)
        k = k * lax.rsqrt(jnp.sum(k * k, axis=-1, keepdims=True) + 1e-6)
        qs.append(q)
        kbs.append(k)
        beta = 1.0 / (1.0 + jnp.exp(-ba[:, h:h + 1]))
        g = -jnp.exp(alog_ref[h]) * _softplus(ba[:, GDN_HEADS + h:GDN_HEADS + h + 1] + dtb_ref[h])
        egs.append(jnp.exp(g))
        us.append(beta)
        qks.append(jnp.sum(q * k, axis=-1, keepdims=True))
    ks_ = [jnp.dot(_rows8(kbs[h], False), s_ref[h], preferred_element_type=F32, precision=_HIGHEST)[0:1]
           for h in range(GDN_HEADS)]
    qs_ = [jnp.dot(_rows8(qs[h], False), s_ref[h], preferred_element_type=F32, precision=_HIGHEST)[0:1]
           for h in range(GDN_HEADS)]
    for h in range(GDN_HEADS):
        v = act[:, 2 * GDN_KD + h * hd:2 * GDN_KD + (h + 1) * hd]
        u = us[h] * (v - egs[h] * ks_[h])
        o = egs[h] * qs_[h] + qks[h] * u
        outer = lax.dot_general(_rows8(kbs[h], True), _rows8(u, False), _TN,
                                preferred_element_type=F32, precision=_HIGHEST)
        s_out_ref[h] = egs[h] * s_ref[h] + outer
        o = o * lax.rsqrt(jnp.mean(o * o, axis=-1, keepdims=True) + RMS_EPS) * ng_ref[...]
        z = proj_ref[:, GDN_CONV_DIM + h * hd:GDN_CONV_DIM + (h + 1) * hd]
        o_ref[:, h * hd:(h + 1) * hd] = o * _silu(z)


def gdn_step(proj, ba, s0, conv0, conv_w, a_log, dt_bias, norm_g):
    bn, pw = proj.shape
    hd = GDN_DK
    o, s = pl.pallas_call(
        _gdn_step_kernel,
        grid=(bn,),
        in_specs=[pl.BlockSpec(memory_space=pltpu.SMEM), pl.BlockSpec(memory_space=pltpu.SMEM),
                  pl.BlockSpec((None, 1, pw), lambda b: (b, 0, 0)),
                  pl.BlockSpec((None, 1, LANES), lambda b: (b, 0, 0)),
                  pl.BlockSpec((None, CONV_W - 1, GDN_CONV_DIM), lambda b: (b, 0, 0)),
                  pl.BlockSpec((CONV_W, GDN_CONV_DIM), lambda b: (0, 0)),
                  pl.BlockSpec((1, hd), lambda b: (0, 0)),
                  pl.BlockSpec((None, GDN_HEADS, hd, hd), lambda b: (b, 0, 0, 0))],
        out_specs=[pl.BlockSpec((None, 1, GDN_VD), lambda b: (b, 0, 0)),
                   pl.BlockSpec((None, GDN_HEADS, hd, hd), lambda b: (b, 0, 0, 0))],
        out_shape=[jax.ShapeDtypeStruct((bn, 1, GDN_VD), F32),
                   jax.ShapeDtypeStruct(s0.shape, F32)],
        compiler_params=_cparams(1),
        name="gdn_step",
    )(a_log.astype(F32), dt_bias.astype(F32), proj.reshape(bn, 1, pw), ba.reshape(bn, 1, LANES),
      conv0, conv_w, norm_g.reshape(1, hd), s0)
    return o.reshape(bn, GDN_VD), s


def _ssd_step_kernel(proj_ref, dt_ref, c_ref, cw_ref, cb_ref, dtb_ref, aneg_ref, dskip_ref, ng_ref, e_ref,
                     h_ref, y_ref, h_out_ref):
    di = SSM_D_INNER
    act = _conv_step(proj_ref[:, di:di + SSM_CONV_DIM], c_ref, cw_ref, cb_ref[...])
    xs = act[:, :di]
    dt = _softplus(dt_ref[...] + dtb_ref[...])
    ea = jnp.exp(dt * aneg_ref[...])
    ex = _dot3(jnp.concatenate([_rows8(dt, True), _rows8(ea, True)], axis=0), e_ref[...])
    dt_x, ea_x = ex[0:1], ex[SUB:SUB + 1]
    xdt = xs * dt_x
    ones = jnp.ones((SUB, SSM_STATE), BF16)
    for g in range(SSM_GROUPS):
        gs = slice(g * SSM_GROUP_W, (g + 1) * SSM_GROUP_W)
        bg = act[:, di + g * SSM_STATE:di + (g + 1) * SSM_STATE]
        cg = act[:, di + SSM_GN + g * SSM_STATE:di + SSM_GN + (g + 1) * SSM_STATE]
        dstate = lax.dot_general(_rows8(xdt[:, gs], True), _rows8(bg, False), _TN,
                                 preferred_element_type=F32, precision=_HIGHEST)
        decay = _dot3(_rows8(ea_x[:, gs], True), ones, _TN)
        h_old = h_ref[g * SSM_HPG:(g + 1) * SSM_HPG].reshape(SSM_GROUP_W, SSM_STATE)
        h_new = decay * h_old + dstate
        h_out_ref[g * SSM_HPG:(g + 1) * SSM_HPG] = h_new.reshape(SSM_HPG, SSM_HEAD_DIM, SSM_STATE)
        y = lax.dot_general(_rows8(cg, False), h_new, _NT, preferred_element_type=F32,
                            precision=_HIGHEST)[0:1]
        y = y + dskip_ref[:, gs] * xs[:, gs]
        y = y * _silu(proj_ref[:, gs])
        y = y * lax.rsqrt(jnp.mean(y * y, axis=-1, keepdims=True) + RMS_EPS) * ng_ref[:, gs]
        y_ref[:, gs] = y


def ssd_step(proj, dtp, h0, conv0, conv_w, conv_b, a_log, dt_bias, d_skip, norm_g):
    bn, pw = proj.shape
    di = SSM_D_INNER
    full = lambda shape: pl.BlockSpec(shape, lambda b: (0,) * len(shape))
    hblk = pl.BlockSpec((None, SSM_HEADS, SSM_HEAD_DIM, SSM_STATE), lambda b: (b, 0, 0, 0))
    y, h = pl.pallas_call(
        _ssd_step_kernel,
        grid=(bn,),
        in_specs=[pl.BlockSpec((None, 1, pw), lambda b: (b, 0, 0)),
                  pl.BlockSpec((None, 1, LANES), lambda b: (b, 0, 0)),
                  pl.BlockSpec((None, CONV_W - 1, SSM_CONV_DIM), lambda b: (b, 0, 0)),
                  full((CONV_W, SSM_CONV_DIM)), full((1, SSM_CONV_DIM)),
                  full((1, LANES)), full((1, LANES)), full((1, di)), full((1, di)), full((LANES, di)),
                  hblk],
        out_specs=[pl.BlockSpec((None, 1, di), lambda b: (b, 0, 0)), hblk],
        out_shape=[jax.ShapeDtypeStruct((bn, 1, di), F32), jax.ShapeDtypeStruct(h0.shape, F32)],
        compiler_params=_cparams(1),
        name="ssd_step",
    )(proj.reshape(bn, 1, pw), dtp.reshape(bn, 1, LANES), conv0, conv_w, conv_b.reshape(1, -1),
      _pad_lanes(dt_bias), _pad_lanes(-jnp.exp(a_log)), jnp.repeat(d_skip, SSM_HEAD_DIM).reshape(1, di),
      norm_g.reshape(1, di), _head_expander(), h0)
    return y.reshape(bn, di), h


def _pad_cols(w):
    return jnp.pad(w, ((0, 0), (0, LANES - w.shape[1])))


def kernel(x_prompt, x_sample, state_gdn_S, state_gdn_conv, state_ssm_h, state_ssm_conv,
           cache_attn_k, cache_attn_v, page_table,
           gdn_w_in, gdn_conv_w, gdn_a_log, gdn_dt_bias, gdn_norm_g, gdn_w_out,
           ssm_w_in, ssm_conv_w, ssm_conv_b, ssm_a_log, ssm_dt_bias, ssm_d_skip, ssm_norm_g, ssm_w_out,
           att_w_qkv, att_lam_q1, att_lam_k1, att_lam_q2, att_lam_k2, att_subln_g, att_w_out, rel_bias,
           ffn_w_gu, ffn_w_down, moe_w_router, moe_w_gu, moe_w_down, ln_g, ln_b):
    bp, lp, d = x_prompt.shape
    bs, ls, _ = x_sample.shape
    assert ls == 1, "the sample group advances one token per sequence"
    tp = bp * lp
    ts = bs * ls
    xp = x_prompt.reshape(tp, d)
    xs = x_sample.reshape(ts, d)
    moe_gu = moe_w_gu.reshape((-1,) + moe_w_gu.shape[2:])
    moe_down = moe_w_down.reshape((-1,) + moe_w_down.shape[2:])

    tm_p = 1024
    tm_s = ts

    outs = {name: [] for name in ("gS_p", "gS_s", "gc_p", "gc_s", "sh_p", "sh_s", "sc_p", "sc_s",
                                  "ak_p", "ak_s", "av_p", "av_s")}
    for i in range(DEPTH):
        j = i // N_MIXERS
        if i % N_MIXERS == 0:
            n_main = GDN_CONV_DIM + GDN_VD
            w_ba = _pad_cols(gdn_w_in[j][:, n_main:])
            prm = (gdn_conv_w[j], gdn_a_log[j], gdn_dt_bias[j], gdn_norm_g[j])
            proj_p = matmul_cols(xp, gdn_w_in[j], n_main, tm_p, 512).reshape(bp, lp, n_main)
            ba_p = matmul_narrow(xp, w_ba, tm_p).reshape(bp, lp, LANES)
            op, s_p = gdn_prompt(proj_p, ba_p, *prm)
            c_p = proj_p[:, lp - (CONV_W - 1):, :GDN_CONV_DIM]
            proj_s = matmul_cols(xs, gdn_w_in[j], n_main, tm_s, 512, precise=True)
            ba_s = matmul_narrow(xs, w_ba, tm_s)
            os_, s_s = gdn_step(proj_s, ba_s, state_gdn_S[j], state_gdn_conv[j], *prm)
            c_s = jnp.concatenate([state_gdn_conv[j][:, 1:], proj_s[:, None, :GDN_CONV_DIM]], axis=1)
            outs["gS_p"].append(s_p); outs["gS_s"].append(s_s)
            outs["gc_p"].append(c_p); outs["gc_s"].append(c_s)
            w_out = gdn_w_out[j]
        elif i % N_MIXERS == 1:
            n_main = SSM_D_INNER + SSM_CONV_DIM
            w_dt = _pad_cols(ssm_w_in[j][:, n_main:])
            prm = (ssm_conv_w[j], ssm_conv_b[j], ssm_a_log[j], ssm_dt_bias[j], ssm_d_skip[j], ssm_norm_g[j])
            proj_p = matmul_cols(xp, ssm_w_in[j], n_main, tm_p, 512).reshape(bp, lp, n_main)
            dt_p = matmul_narrow(xp, w_dt, tm_p).reshape(bp, lp, LANES)
            op, h_p = ssd_prompt(proj_p, dt_p, *prm)
            c_p = proj_p[:, lp - (CONV_W - 1):, SSM_D_INNER:]
            proj_s = matmul_cols(xs, ssm_w_in[j], n_main, tm_s, 512, precise=True)
            dt_s = matmul_narrow(xs, w_dt, tm_s)
            os_, h_s = ssd_step(proj_s, dt_s, state_ssm_h[j], state_ssm_conv[j], *prm)
            c_s = jnp.concatenate([state_ssm_conv[j][:, 1:], proj_s[:, None, SSM_D_INNER:]], axis=1)
            outs["sh_p"].append(h_p); outs["sh_s"].append(h_s)
            outs["sc_p"].append(c_p); outs["sc_s"].append(c_s)
            w_out = ssm_w_out[j]
        else:
            lam_init = 0.8 - 0.6 * math.exp(-0.3 * i)
            lam = (jnp.exp(jnp.sum(att_lam_q1[j] * att_lam_k1[j]))
                   - jnp.exp(jnp.sum(att_lam_q2[j] * att_lam_k2[j])) + lam_init)
            qkv_p = matmul_cols(xp, att_w_qkv[j], 3 * d, tm_p, 512).reshape(bp, lp, 3 * d)
            op = attn_prompt(qkv_p, lam, lam_init, attn_bias_tiles(rel_bias), att_subln_g[j])
            k_p = qkv_p[..., d:2 * d].reshape(bp, lp, ATT_HEADS, 2 * ATT_HEAD_DIM)
            v_p = qkv_p[..., 2 * d:].reshape(bp, lp, ATT_HEADS, 2 * ATT_HEAD_DIM)
            qkv_s = matmul_cols(xs, att_w_qkv[j], 3 * d, tm_s, 512, precise=True).reshape(bs, ls, 3 * d)
            os_ = attn_decode(qkv_s.reshape(ts, 3 * d), cache_attn_k, cache_attn_v, j, page_table,
                              lam, lam_init, rel_bias, att_subln_g[j])
            k_s = qkv_s[..., d:2 * d].reshape(bs, ls, ATT_HEADS, 2 * ATT_HEAD_DIM)
            v_s = qkv_s[..., 2 * d:].reshape(bs, ls, ATT_HEADS, 2 * ATT_HEAD_DIM)
            outs["ak_p"].append(k_p); outs["ak_s"].append(k_s)
            outs["av_p"].append(v_p); outs["av_s"].append(v_s)
            w_out = att_w_out[j]
        xp = proj_residual_ln(op.reshape(tp, -1), w_out, xp, ln_g[i, 0], ln_b[i, 0], 512)
        xs = proj_residual_ln(os_.reshape(ts, -1), w_out, xs, ln_g[i, 0], ln_b[i, 0], tm_s, precise=True)
        f = i // 2
        if i % 2 == 0:
            xp = dense_ffn_ln(xp, ffn_w_gu, ffn_w_down, f, ln_g[i, 1], ln_b[i, 1], tm_p)
            xs = dense_ffn_ln(xs, ffn_w_gu, ffn_w_down, f, ln_g[i, 1], ln_b[i, 1], tm_s, precise=True)
        else:
            xp = moe_ffn_ln(xp, moe_w_router[f], moe_gu, moe_down, f, ln_g[i, 1], ln_b[i, 1],
                            512, 1024, 256)
            xs = moe_ffn_ln(xs, moe_w_router[f], moe_gu, moe_down, f, ln_g[i, 1], ln_b[i, 1],
                            ts, ts, ts, precise=True)
    st = lambda name: jnp.stack(outs[name])
    return (xp.reshape(bp, lp, d), xs.reshape(bs, ls, d),
            st("gS_p"), st("gS_s"), st("gc_p"), st("gc_s"),
            st("sh_p"), st("sh_s"), st("sc_p"), st("sc_s"),
            st("ak_p"), st("ak_s"), st("av_p"), st("av_s"))
```
